```python
import math
import jax, jax.numpy as jnp
from jax import lax
import numpy as np

D_MODEL = 1024
BATCH = 8
SEQ = 2048
DEPTH = 4
DEC_BATCH = 128
DEC_SEQ = 4
PAST_LEN = 16384
PAGE_SIZE = 128

MIX_WIDTH = D_MODEL
DN_HEADS = 4
DN_HEAD_DIM = MIX_WIDTH // 8
DN_WIDTH = DN_HEADS * DN_HEAD_DIM
DN_CONV = 4
DN_CHUNK = 64
SSM_WIDTH = MIX_WIDTH // 4
SSM_GROUP = 16
SSM_GROUPS = SSM_WIDTH // SSM_GROUP
SSM_STATE = 64
POOL_WIDTH = MIX_WIDTH - DN_WIDTH - SSM_WIDTH
POOL_WINDOWS = (2, 4, 8, 16)
POOL_GROUPS = len(POOL_WINDOWS)
POOL_GROUP = POOL_WIDTH // POOL_GROUPS
POOL_BUF = max(POOL_WINDOWS) - 1
D_FF = -(-8 * D_MODEL // (3 * 256)) * 256
EPS = 1e-6

QKV_WIDTH = 3 * DN_WIDTH
OFF_A = QKV_WIDTH
OFF_B = OFF_A + DN_HEADS
OFF_G = OFF_B + DN_HEADS
OFF_SSM = OFF_G + DN_WIDTH
OFF_POOL = OFF_SSM + SSM_WIDTH
IN_WIDTH = OFF_POOL + POOL_WIDTH

F32 = jnp.float32

kernel_name = 'hybrid_deltanet_s5_pool_decoder_step'


def rms_norm(x, w):
    x32 = x.astype(F32)
    y = x32 * lax.rsqrt(jnp.mean(x32 * x32, axis=-1, keepdims=True) + EPS)
    return (y * w.astype(F32)).astype(x.dtype)


def l2_norm(x):
    return x * lax.rsqrt(jnp.sum(x * x, axis=-1, keepdims=True) + EPS)


def causal_dwconv(u, buf, w):
    T = u.shape[1]
    ext = jnp.concatenate([buf.astype(u.dtype), u], axis=1)
    ext32 = ext.astype(F32)
    w32 = w.astype(F32)
    y = ext32[:, 0:T] * w32[0]
    for j in range(1, DN_CONV):
        y = y + ext32[:, j:j + T] * w32[j]
    return y, ext[:, -(DN_CONV - 1):]


def gated_delta_rule(q, k, v, beta, log_g, s0):
    B, T, H, Dk = q.shape
    Dv = v.shape[-1]
    C = math.gcd(T, DN_CHUNK)
    N = T // C

    def chunk(t):
        t = t.reshape((B, N, C, H) + t.shape[3:])
        return jnp.moveaxis(t, (1, 3), (0, 2))

    qc, kc, vc = chunk(q), chunk(k), chunk(v)
    bc, gc = chunk(beta), chunk(log_g)
    gam = jnp.cumsum(gc, axis=-1)
    idx = jnp.arange(C)
    incl = idx[:, None] >= idx[None, :]
    strict = idx[:, None] > idx[None, :]
    diff = gam[..., :, None] - gam[..., None, :]
    decay = jnp.where(incl, jnp.exp(jnp.where(incl, diff, 0.0)), 0.0)
    kb = kc * bc[..., None]
    a_mat = jnp.where(strict, jnp.einsum('nbhtd,nbhsd->nbhts', kb, kc) * decay, 0.0)
    rhs = jnp.concatenate([vc * bc[..., None], kb * jnp.exp(gam)[..., None]], axis=-1)
    sol = lax.linalg.triangular_solve(a_mat, rhs, left_side=True, lower=True, unit_diagonal=True)
    w_val, k_cum = sol[..., :Dv], sol[..., Dv:]
    qk = jnp.einsum('nbhtd,nbhsd->nbhts', qc, kc) * decay
    q_dec = qc * jnp.exp(gam)[..., None]
    k_dec = kc * jnp.exp(gam[..., -1:] - gam)[..., None]
    g_last = jnp.exp(gam[..., -1])[..., None, None]

    def step(s, xs):
        w_n, kcum_n, qk_n, qdec_n, kdec_n, gl_n = xs
        u = w_n - jnp.einsum('bhck,bhkv->bhcv', kcum_n, s)
        o = jnp.einsum('bhck,bhkv->bhcv', qdec_n, s) + jnp.einsum('bhts,bhsv->bhtv', qk_n, u)
        s = s * gl_n + jnp.einsum('bhck,bhcv->bhkv', kdec_n, u)
        return s, o

    s_fin, o = lax.scan(step, s0, (w_val, k_cum, qk, q_dec, k_dec, g_last))
    o = jnp.moveaxis(o, (0, 2), (1, 3)).reshape(B, T, H, Dv)
    return o, s_fin


def delta_mixer(proj, conv_buf, s0, conv_w, a_log, dt_bias, out_norm):
    B, T, _ = proj.shape
    qkv, conv_new = causal_dwconv(proj[..., :QKV_WIDTH], conv_buf, conv_w)
    qkv = jax.nn.silu(qkv).reshape(B, T, 3, DN_HEADS, DN_HEAD_DIM)
    q = l2_norm(qkv[:, :, 0]) * (DN_HEAD_DIM ** -0.5)
    k = l2_norm(qkv[:, :, 1])
    v = qkv[:, :, 2]
    a = proj[..., OFF_A:OFF_B].astype(F32)
    b = proj[..., OFF_B:OFF_G].astype(F32)
    log_g = -jnp.exp(a_log.astype(F32)) * jax.nn.softplus(a + dt_bias.astype(F32))
    beta = jax.nn.sigmoid(b)
    o, s_new = gated_delta_rule(q, k, v, beta, log_g, s0.astype(F32))
    gate = proj[..., OFF_G:OFF_SSM].astype(F32).reshape(B, T, DN_HEADS, DN_HEAD_DIM)
    o = o * lax.rsqrt(jnp.mean(o * o, axis=-1, keepdims=True) + EPS) * out_norm.astype(F32) * jax.nn.silu(gate)
    return o.reshape(B, T, DN_WIDTH), conv_new, s_new


def ssm_mixer(u, h0_re, h0_im, a_re, a_im, log_dt, b_re, b_im, c_re, c_im, d_skip, glu_w, glu_b):
    B, T, _ = u.shape
    u32 = u.astype(F32).reshape(B, T, SSM_GROUPS, SSM_GROUP)
    lam = lax.complex(a_re.astype(F32), a_im.astype(F32))
    dt = jnp.exp(log_dt.astype(F32))[:, None]
    lam_bar = jnp.exp(lam * dt)
    b_c = lax.complex(b_re.astype(F32), b_im.astype(F32))
    b_bar = ((lam_bar - 1.0) / lam)[..., None] * b_c
    bu = jnp.einsum('gph,btgh->btgp', b_bar, u32.astype(jnp.complex64))
    a_seq = jnp.broadcast_to(lam_bar, bu.shape)

    def combine(e1, e2):
        a1, x1 = e1
        a2, x2 = e2
        return a2 * a1, a2 * x1 + x2

    a_cum, h = lax.associative_scan(combine, (a_seq, bu), axis=1)
    h0 = lax.complex(h0_re.astype(F32), h0_im.astype(F32))
    h = h + a_cum * h0[:, None]
    c_c = lax.complex(c_re.astype(F32), c_im.astype(F32))
    y = jnp.real(jnp.einsum('ghp,btgp->btgh', c_c, h)) + d_skip.astype(F32).reshape(SSM_GROUPS, SSM_GROUP) * u32
    y = jax.nn.gelu(y.reshape(B, T, SSM_WIDTH))
    y = y * jax.nn.sigmoid(y @ glu_w.astype(F32) + glu_b.astype(F32))
    h_last = h[:, -1]
    return y, jnp.real(h_last), jnp.imag(h_last)


def pool_mixer(u, buf, pos0, pool_w, pool_scale):
    B, T, _ = u.shape
    u32 = u.astype(F32)
    ext = jnp.concatenate([buf.astype(u.dtype), u], axis=1)
    cs = jnp.concatenate([jnp.zeros((B, 1, POOL_WIDTH), F32), jnp.cumsum(ext.astype(F32), axis=1)], axis=1)
    pos = pos0 + jnp.arange(T)
    outs = []
    for g, w in enumerate(POOL_WINDOWS):
        lo, hi = g * POOL_GROUP, (g + 1) * POOL_GROUP
        win_sum = cs[:, POOL_BUF + 1:POOL_BUF + 1 + T, lo:hi] - cs[:, POOL_BUF + 1 - w:POOL_BUF + 1 - w + T, lo:hi]
        cnt = jnp.minimum(pos + 1, w).astype(F32)[None, :, None]
        r = win_sum / cnt - u32[..., lo:hi]
        outs.append(r @ pool_w[g].astype(F32))
    y = jnp.concatenate(outs, axis=-1) * pool_scale.astype(F32)
    return y, ext[:, -POOL_BUF:]


def layer(x, st, pos0, lp):
    s_delta, s_conv, s_re, s_im, s_pool = st
    h = rms_norm(x, lp['norm_mix_pre'])
    proj = h @ lp['w_in']
    o_dn, conv_new, delta_new = delta_mixer(proj, s_conv, s_delta, lp['conv_w'], lp['dn_a_log'],
                                            lp['dn_dt_bias'], lp['dn_out_norm'])
    o_ssm, re_new, im_new = ssm_mixer(proj[..., OFF_SSM:OFF_POOL], s_re, s_im, lp['ssm_a_re'], lp['ssm_a_im'],
                                      lp['ssm_log_dt'], lp['ssm_b_re'], lp['ssm_b_im'], lp['ssm_c_re'],
                                      lp['ssm_c_im'], lp['ssm_d'], lp['ssm_glu_w'], lp['ssm_glu_b'])
    o_pool, pool_new = pool_mixer(proj[..., OFF_POOL:], s_pool, pos0, lp['pool_w'], lp['pool_scale'])
    mix = jnp.concatenate([o_dn, o_ssm, o_pool], axis=-1).astype(x.dtype) @ lp['w_out']
    x = x + rms_norm(mix, lp['norm_mix_post'])
    h = rms_norm(x, lp['norm_ffn_pre'])
    f = (jax.nn.silu(h @ lp['ffn_w_gate']) * (h @ lp['ffn_w_up'])) @ lp['ffn_w_down']
    x = x + rms_norm(f, lp['norm_ffn_post'])
    return x, (delta_new, conv_new, re_new, im_new, pool_new)


def setup_inputs(seed: int = 0) -> dict:
    key = jax.random.key(seed)
    ks = iter(jax.random.split(key, 40))

    def nrm(shape, scale):
        return jax.random.normal(next(ks), shape, F32) * scale

    def gain(shape):
        return 1.0 + nrm(shape, 0.01)

    dn_dt = jnp.exp(jax.random.uniform(next(ks), (DEPTH, DN_HEADS), F32, math.log(1e-3), math.log(1e-1)))
    return {
        'x_prompt': nrm((BATCH, SEQ, D_MODEL), 1.0),
        'x_sample': nrm((DEC_BATCH, DEC_SEQ, D_MODEL), 1.0),
        'state_delta': nrm((DEPTH, DEC_BATCH, DN_HEADS, DN_HEAD_DIM, DN_HEAD_DIM), 0.05),
        'state_conv': nrm((DEPTH, DEC_BATCH, DN_CONV - 1, QKV_WIDTH), 1.0),
        'state_ssm_re': nrm((DEPTH, DEC_BATCH, SSM_GROUPS, SSM_STATE), 0.1),
        'state_ssm_im': nrm((DEPTH, DEC_BATCH, SSM_GROUPS, SSM_STATE), 0.1),
        'state_pool': nrm((DEPTH, DEC_BATCH, POOL_BUF, POOL_WIDTH), 1.0),
        'norm_mix_pre': gain((DEPTH, D_MODEL)),
        'norm_mix_post': gain((DEPTH, D_MODEL)),
        'norm_ffn_pre': gain((DEPTH, D_MODEL)),
        'norm_ffn_post': gain((DEPTH, D_MODEL)),
        'w_in': nrm((DEPTH, D_MODEL, IN_WIDTH), D_MODEL ** -0.5),
        'conv_w': nrm((DEPTH, DN_CONV, QKV_WIDTH), DN_CONV ** -0.5),
        'dn_a_log': jnp.log(jax.random.uniform(next(ks), (DEPTH, DN_HEADS), F32, 1.0, 16.0)),
        'dn_dt_bias': dn_dt + jnp.log(-jnp.expm1(-dn_dt)),
        'dn_out_norm': gain((DEPTH, DN_HEAD_DIM)),
        'ssm_a_re': -0.5 + nrm((DEPTH, SSM_GROUPS, SSM_STATE), 0.01),
        'ssm_a_im': jnp.pi * jnp.arange(SSM_STATE, dtype=F32) + nrm((DEPTH, SSM_GROUPS, SSM_STATE), 0.01),
        'ssm_log_dt': jax.random.uniform(next(ks), (DEPTH, SSM_GROUPS), F32, math.log(1e-3), math.log(1e-1)),
        'ssm_b_re': nrm((DEPTH, SSM_GROUPS, SSM_STATE, SSM_GROUP), (2 * SSM_GROUP) ** -0.5),
        'ssm_b_im': nrm((DEPTH, SSM_GROUPS, SSM_STATE, SSM_GROUP), (2 * SSM_GROUP) ** -0.5),
        'ssm_c_re': nrm((DEPTH, SSM_GROUPS, SSM_GROUP, SSM_STATE), SSM_STATE ** -0.5),
        'ssm_c_im': nrm((DEPTH, SSM_GROUPS, SSM_GROUP, SSM_STATE), SSM_STATE ** -0.5),
        'ssm_d': nrm((DEPTH, SSM_WIDTH), 1.0),
        'ssm_glu_w': nrm((DEPTH, SSM_WIDTH, SSM_WIDTH), SSM_WIDTH ** -0.5),
        'ssm_glu_b': nrm((DEPTH, SSM_WIDTH), 0.01),
        'pool_w': nrm((DEPTH, POOL_GROUPS, POOL_GROUP, POOL_GROUP), POOL_GROUP ** -0.5),
        'pool_scale': gain((DEPTH, POOL_WIDTH)),
        'w_out': nrm((DEPTH, MIX_WIDTH, D_MODEL), MIX_WIDTH ** -0.5),
        'ffn_w_gate': nrm((DEPTH, D_MODEL, D_FF), D_MODEL ** -0.5),
        'ffn_w_up': nrm((DEPTH, D_MODEL, D_FF), D_MODEL ** -0.5),
        'ffn_w_down': nrm((DEPTH, D_FF, D_MODEL), D_FF ** -0.5),
    }


def reference(x_prompt, x_sample, state_delta, state_conv, state_ssm_re, state_ssm_im, state_pool,
              norm_mix_pre, norm_mix_post, norm_ffn_pre, norm_ffn_post, w_in, conv_w, dn_a_log, dn_dt_bias,
              dn_out_norm, ssm_a_re, ssm_a_im, ssm_log_dt, ssm_b_re, ssm_b_im, ssm_c_re, ssm_c_im, ssm_d,
              ssm_glu_w, ssm_glu_b, pool_w, pool_scale, w_out, ffn_w_gate, ffn_w_up, ffn_w_down):
    B = x_prompt.shape[0]
    zero_state = (
        jnp.zeros((B, DN_HEADS, DN_HEAD_DIM, DN_HEAD_DIM), F32),
        jnp.zeros((B, DN_CONV - 1, QKV_WIDTH), x_prompt.dtype),
        jnp.zeros((B, SSM_GROUPS, SSM_STATE), F32),
        jnp.zeros((B, SSM_GROUPS, SSM_STATE), F32),
        jnp.zeros((B, POOL_BUF, POOL_WIDTH), x_prompt.dtype),
    )
    xp, xs = x_prompt, x_sample
    new_p, new_s = [], []
    for l in range(DEPTH):
        lp = {
            'norm_mix_pre': norm_mix_pre[l], 'norm_mix_post': norm_mix_post[l],
            'norm_ffn_pre': norm_ffn_pre[l], 'norm_ffn_post': norm_ffn_post[l],
            'w_in': w_in[l], 'conv_w': conv_w[l], 'dn_a_log': dn_a_log[l], 'dn_dt_bias': dn_dt_bias[l],
            'dn_out_norm': dn_out_norm[l], 'ssm_a_re': ssm_a_re[l], 'ssm_a_im': ssm_a_im[l],
            'ssm_log_dt': ssm_log_dt[l], 'ssm_b_re': ssm_b_re[l], 'ssm_b_im': ssm_b_im[l],
            'ssm_c_re': ssm_c_re[l], 'ssm_c_im': ssm_c_im[l], 'ssm_d': ssm_d[l],
            'ssm_glu_w': ssm_glu_w[l], 'ssm_glu_b': ssm_glu_b[l], 'pool_w': pool_w[l],
            'pool_scale': pool_scale[l], 'w_out': w_out[l], 'ffn_w_gate': ffn_w_gate[l],
            'ffn_w_up': ffn_w_up[l], 'ffn_w_down': ffn_w_down[l],
        }
        xp, sp = layer(xp, zero_state, 0, lp)
        st_l = (state_delta[l], state_conv[l], state_ssm_re[l], state_ssm_im[l], state_pool[l])
        xs, ss = layer(xs, st_l, PAST_LEN, lp)
        new_p.append(sp)
        new_s.append(ss)
    delta_p = jnp.stack([s[0] for s in new_p])
    conv_p = jnp.stack([s[1] for s in new_p])
    ssm_re_p = jnp.stack([s[2] for s in new_p])
    ssm_im_p = jnp.stack([s[3] for s in new_p])
    pool_p = jnp.stack([s[4] for s in new_p])
    delta_s = jnp.stack([s[0] for s in new_s])
    conv_s = jnp.stack([s[1] for s in new_s])
    ssm_re_s = jnp.stack([s[2] for s in new_s])
    ssm_im_s = jnp.stack([s[3] for s in new_s])
    pool_s = jnp.stack([s[4] for s in new_s])
    return (xp, xs, delta_p, conv_p, ssm_re_p, ssm_im_p, pool_p, delta_s, conv_s, ssm_re_s, ssm_im_s, pool_s)
```

```python
import functools
import math

import jax
import jax.numpy as jnp
from jax import lax
from jax.experimental import pallas as pl
from jax.experimental.pallas import tpu as pltpu

F32 = jnp.float32
BF16 = jnp.bfloat16
HIGHEST = lax.Precision.HIGHEST

D_MODEL = 1024
DN_HEADS = 4
DN_HEAD_DIM = 128
DN_WIDTH = DN_HEADS * DN_HEAD_DIM
DN_CONV = 4
QKV_WIDTH = 3 * DN_WIDTH
SSM_WIDTH = 256
SSM_GROUP = 16
SSM_GROUPS = 16
SSM_STATE = 64
SSM_NS = SSM_GROUPS * SSM_STATE
POOL_WIDTH = 256
POOL_WINDOWS = (2, 4, 8, 16)
POOL_GROUP = 64
POOL_BUF = 15
D_FF = 2816
EPS = 1e-6
PAST_LEN = 16384

COL_GATE = QKV_WIDTH
COL_SSM = COL_GATE + DN_WIDTH
COL_POOL = COL_SSM + SSM_WIDTH
COL_AB = COL_POOL + POOL_WIDTH
LANES = 128
SUBLANES = 8
PROJ_WIDTH = COL_AB + LANES

_OFF_A = QKV_WIDTH
_OFF_G = _OFF_A + 2 * DN_HEADS

VMEM_LIMIT_BYTES = 56 * 1024 * 1024

DELTA_CHUNK = 64
SAMPLE_CHUNK = 8


def _cparams(n_axes):
    return pltpu.CompilerParams(dimension_semantics=("arbitrary",) * n_axes,
                                vmem_limit_bytes=VMEM_LIMIT_BYTES)


def _bdot(a, b):
    return jnp.dot(a.astype(BF16), b.astype(BF16), preferred_element_type=F32)


def _bdot_nt(a, b):
    return lax.dot_general(a.astype(BF16), b.astype(BF16), (((1,), (1,)), ((), ())),
                           preferred_element_type=F32)


def _bdot_tn(a, b):
    return lax.dot_general(a.astype(BF16), b.astype(BF16), (((0,), (0,)), ((), ())),
                           preferred_element_type=F32)


def _hdot(a, b):
    return jnp.dot(a, b, precision=HIGHEST, preferred_element_type=F32)


def _rms(x, w):
    return x * lax.rsqrt(jnp.mean(x * x, axis=-1, keepdims=True) + EPS) * w


def _silu(x):
    return x * jax.nn.sigmoid(x)


def _in_proj_kernel(x_ref, nw_ref, w_ref, o_ref):
    h = _rms(x_ref[...], nw_ref[...])
    o_ref[...] = _bdot(h, w_ref[...])


def _in_proj(x, nw, w, tm):
    m = x.shape[0]
    return pl.pallas_call(
        _in_proj_kernel,
        grid=(m // tm,),
        in_specs=[pl.BlockSpec((tm, D_MODEL), lambda i: (i, 0)),
                  pl.BlockSpec((1, D_MODEL), lambda i: (0, 0)),
                  pl.BlockSpec((D_MODEL, PROJ_WIDTH), lambda i: (0, 0), pipeline_mode=pl.Buffered(1))],
        out_specs=pl.BlockSpec((tm, PROJ_WIDTH), lambda i: (i, 0)),
        out_shape=jax.ShapeDtypeStruct((m, PROJ_WIDTH), F32),
        compiler_params=_cparams(1),
        name="in_proj",
    )(x, nw, w)


def _post_mix_kernel(odn_ref, ossm_ref, opool_ref, x_ref, wo_ref, nmp_ref, nfp_ref, wg_ref, wu_ref, wd_ref,
                     nfo_ref, o_ref):
    mix = (_bdot(odn_ref[...], wo_ref[0:DN_WIDTH, :])
           + _bdot(ossm_ref[...], wo_ref[DN_WIDTH:DN_WIDTH + SSM_WIDTH, :])
           + _bdot(opool_ref[...], wo_ref[DN_WIDTH + SSM_WIDTH:, :]))
    x1 = x_ref[...] + _rms(mix, nmp_ref[...])
    h = _rms(x1, nfp_ref[...]).astype(BF16)
    g = jnp.dot(h, wg_ref[...], preferred_element_type=F32)
    u = jnp.dot(h, wu_ref[...], preferred_element_type=F32)
    f = _bdot(_silu(g) * u, wd_ref[...])
    o_ref[...] = x1 + _rms(f, nfo_ref[...])


def _post_mix(odn, ossm, opool, x, wo, nmp, nfp, wg, wu, wd, nfo, tm):
    m = x.shape[0]
    row = lambda w: pl.BlockSpec((tm, w), lambda i: (i, 0))
    full = lambda a: pl.BlockSpec(a.shape, lambda i: (0,) * a.ndim, pipeline_mode=pl.Buffered(1))
    return pl.pallas_call(
        _post_mix_kernel,
        grid=(m // tm,),
        in_specs=[row(DN_WIDTH), row(SSM_WIDTH), row(POOL_WIDTH), row(D_MODEL),
                  full(wo), full(nmp), full(nfp), full(wg), full(wu), full(wd), full(nfo)],
        out_specs=row(D_MODEL),
        out_shape=jax.ShapeDtypeStruct((m, D_MODEL), F32),
        compiler_params=_cparams(1),
        name="post_mix",
    )(odn, ossm, opool, x, wo, nmp, nfp, wg, wu, wd, nfo)


def _tri_masks(c):
    r = lax.broadcasted_iota(jnp.int32, (c, c), 0)
    s = lax.broadcasted_iota(jnp.int32, (c, c), 1)
    return r >= s, r > s, (r == s).astype(F32)


def _unit_lower_inverse(a, eye, c):
    t = eye - a
    p = a
    n = 2
    while n < c:
        p = _hdot(p, p)
        t = t + _hdot(t, p)
        n *= 2
    return t


def _delta_chunk(q, k, v, beta, logg, s, c):
    incl, strict, eye = _tri_masks(c)
    incl_f = incl.astype(F32)
    diff = _hdot(incl_f, logg * strict.astype(F32))
    gam = _hdot(incl_f, jnp.broadcast_to(logg, (c, LANES)))[:, 0:1]
    decay = jnp.where(incl, jnp.exp(jnp.where(incl, diff, 0.0)), 0.0)
    kb = k * beta
    a = jnp.where(strict, _bdot_nt(kb, k) * decay, 0.0)
    egam = jnp.exp(gam)
    rhs = jnp.concatenate([v * beta, kb * egam], axis=1)
    sol = _hdot(_unit_lower_inverse(a, eye, c), rhs)
    w_val, k_cum = sol[:, :DN_HEAD_DIM], sol[:, DN_HEAD_DIM:]
    qk = _bdot_nt(q, k) * decay
    tot = gam[c - 1:c, :]
    q_dec = q * egam
    k_dec = k * jnp.exp(tot - gam)
    u = w_val - _bdot(k_cum, s)
    o = _bdot(q_dec, s) + _bdot(qk, u)
    s_new = s * jnp.exp(tot) + _bdot_tn(k_dec, u)
    return o, s_new


def _delta_front(y, ab, alog, dtb):
    y = _silu(y)
    qs, ks = [], []
    for h in range(DN_HEADS):
        qh = y[:, h * DN_HEAD_DIM:(h + 1) * DN_HEAD_DIM]
        kh = y[:, DN_WIDTH + h * DN_HEAD_DIM:DN_WIDTH + (h + 1) * DN_HEAD_DIM]
        qs.append(qh * lax.rsqrt(jnp.sum(qh * qh, axis=-1, keepdims=True) + EPS) * (DN_HEAD_DIM ** -0.5))
        ks.append(kh * lax.rsqrt(jnp.sum(kh * kh, axis=-1, keepdims=True) + EPS))
    v = y[:, 2 * DN_WIDTH:]
    z = ab + dtb
    softplus = jnp.maximum(z, 0.0) + jnp.log1p(jnp.exp(-jnp.abs(z)))
    logg = -jnp.exp(alog) * softplus
    beta = jax.nn.sigmoid(ab)
    return qs, ks, v, logg, beta


def _delta_out(o, gate, onorm):
    return o * lax.rsqrt(jnp.mean(o * o, axis=-1, keepdims=True) + EPS) * onorm * _silu(gate)


def _conv4(xfull, w, rows, off):
    y = pltpu.roll(xfull, 3, 0)[off:off + rows] * w[0:1]
    y = y + pltpu.roll(xfull, 2, 0)[off:off + rows] * w[1:2]
    y = y + pltpu.roll(xfull, 1, 0)[off:off + rows] * w[2:3]
    return y + xfull[off:off + rows] * w[3:4]


def _delta_prompt_kernel(qkv_ref, ab_ref, gate_ref, cst_ref, s0_ref, cw_ref, alog_ref, dtb_ref, onorm_ref,
                         o_ref, sfin_ref, tail, s_scr, q_scr, k_scr, v_scr, g_scr, b_scr, *, tb, c):
    i = pl.program_id(1)

    @pl.when(i == 0)
    def _():
        s_scr[...] = s0_ref[0]
        tail[...] = cst_ref[0]

    x = qkv_ref[0]
    xfull = jnp.concatenate([tail[...], x], axis=0)
    y = _conv4(xfull, cw_ref[...], tb, SUBLANES)
    tail[...] = x[tb - SUBLANES:, :]
    qs, ks, v, logg, beta = _delta_front(y, ab_ref[0], alog_ref[...], dtb_ref[...])
    for h in range(DN_HEADS):
        q_scr[:, h * DN_HEAD_DIM:(h + 1) * DN_HEAD_DIM] = qs[h]
        k_scr[:, h * DN_HEAD_DIM:(h + 1) * DN_HEAD_DIM] = ks[h]
    v_scr[...] = v
    g_scr[...] = logg
    b_scr[...] = beta

    def chunk_body(j, carry):
        r0 = pl.multiple_of(j * c, c)
        gblk = g_scr[pl.ds(r0, c), :]
        bblk = b_scr[pl.ds(r0, c), :]
        for h in range(DN_HEADS):
            hs = slice(h * DN_HEAD_DIM, (h + 1) * DN_HEAD_DIM)
            o, s_new = _delta_chunk(q_scr[pl.ds(r0, c), hs], k_scr[pl.ds(r0, c), hs], v_scr[pl.ds(r0, c), hs],
                                    bblk[:, DN_HEADS + h:DN_HEADS + h + 1], gblk[:, h:h + 1], s_scr[h], c)
            s_scr[h] = s_new
            q_scr[pl.ds(r0, c), hs] = o
        return carry

    lax.fori_loop(0, tb // c, chunk_body, 0)
    gate = gate_ref[0]
    for h in range(DN_HEADS):
        hs = slice(h * DN_HEAD_DIM, (h + 1) * DN_HEAD_DIM)
        o_ref[0, :, hs] = _delta_out(q_scr[:, hs], gate[:, hs], onorm_ref[...])
    sfin_ref[0] = s_scr[...]


def _delta_prompt(p3, cst, s0, cw, alog, dtb, onorm, tb):
    nb, t, _ = p3.shape
    kern = functools.partial(_delta_prompt_kernel, tb=tb, c=DELTA_CHUNK)
    const = lambda a: pl.BlockSpec(a.shape, lambda b, i: (0,) * a.ndim)
    return pl.pallas_call(
        kern,
        grid=(nb, t // tb),
        in_specs=[pl.BlockSpec((1, tb, QKV_WIDTH), lambda b, i: (b, i, 0)),
                  pl.BlockSpec((1, tb, LANES), lambda b, i: (b, i, COL_AB // LANES)),
                  pl.BlockSpec((1, tb, DN_WIDTH), lambda b, i: (b, i, COL_GATE // DN_WIDTH)),
                  pl.BlockSpec((1, SUBLANES, QKV_WIDTH), lambda b, i: (b, 0, 0)),
                  pl.BlockSpec((1, DN_HEADS, DN_HEAD_DIM, DN_HEAD_DIM), lambda b, i: (b, 0, 0, 0)),
                  const(cw), const(alog), const(dtb), const(onorm)],
        out_specs=[pl.BlockSpec((1, tb, DN_WIDTH), lambda b, i: (b, i, 0)),
                   pl.BlockSpec((1, DN_HEADS, DN_HEAD_DIM, DN_HEAD_DIM), lambda b, i: (b, 0, 0, 0))],
        out_shape=[jax.ShapeDtypeStruct((nb, t, DN_WIDTH), F32),
                   jax.ShapeDtypeStruct((nb, DN_HEADS, DN_HEAD_DIM, DN_HEAD_DIM), F32)],
        scratch_shapes=[pltpu.VMEM((SUBLANES, QKV_WIDTH), F32),
                        pltpu.VMEM((DN_HEADS, DN_HEAD_DIM, DN_HEAD_DIM), F32),
                        pltpu.VMEM((tb, DN_WIDTH), F32), pltpu.VMEM((tb, DN_WIDTH), F32),
                        pltpu.VMEM((tb, DN_WIDTH), F32),
                        pltpu.VMEM((tb, LANES), F32), pltpu.VMEM((tb, LANES), F32)],
        compiler_params=_cparams(2),
        name="delta_prompt",
    )(p3, p3, p3, cst, s0, cw, alog, dtb, onorm)


def _delta_sample_kernel(qkv_ref, ab_ref, gate_ref, s0_ref, cw_ref, alog_ref, dtb_ref, onorm_ref,
                         o_ref, snew_ref, q_scr, k_scr, v_scr, g_scr, b_scr, *, nseq, c):
    rows = nseq * c
    x = qkv_ref[...]
    y = _conv4(x, cw_ref[...], rows, 0)
    qs, ks, v, logg, beta = _delta_front(y, ab_ref[...], alog_ref[...], dtb_ref[...])
    valid = (lax.broadcasted_iota(jnp.int32, (rows, LANES), 0) % c) >= (c // 2)
    for h in range(DN_HEADS):
        q_scr[:, h * DN_HEAD_DIM:(h + 1) * DN_HEAD_DIM] = qs[h]
        k_scr[:, h * DN_HEAD_DIM:(h + 1) * DN_HEAD_DIM] = ks[h]
    v_scr[...] = v
    g_scr[...] = jnp.where(valid, logg, 0.0)
    b_scr[...] = jnp.where(valid, beta, 0.0)

    def seq_body(j, carry):
        r0 = pl.multiple_of(j * c, c)
        gblk = g_scr[pl.ds(r0, c), :]
        bblk = b_scr[pl.ds(r0, c), :]
        for h in range(DN_HEADS):
            hs = slice(h * DN_HEAD_DIM, (h + 1) * DN_HEAD_DIM)
            o, s_new = _delta_chunk(q_scr[pl.ds(r0, c), hs], k_scr[pl.ds(r0, c), hs], v_scr[pl.ds(r0, c), hs],
                                    bblk[:, DN_HEADS + h:DN_HEADS + h + 1], gblk[:, h:h + 1], s0_ref[j, h], c)
            snew_ref[j, h] = s_new
            q_scr[pl.ds(r0, c), hs] = o
        return carry

    lax.fori_loop(0, nseq, seq_body, 0)
    gate = gate_ref[...]
    for h in range(DN_HEADS):
        hs = slice(h * DN_HEAD_DIM, (h + 1) * DN_HEAD_DIM)
        o_ref[:, hs] = _delta_out(q_scr[:, hs], gate[:, hs], onorm_ref[...])


def _delta_sample(ext, s0, cw, alog, dtb, onorm, nseq):
    c = SAMPLE_CHUNK
    nb = s0.shape[0]
    rows = nseq * c
    kern = functools.partial(_delta_sample_kernel, nseq=nseq, c=c)
    const = lambda a: pl.BlockSpec(a.shape, lambda i: (0,) * a.ndim)
    sspec = pl.BlockSpec((nseq, DN_HEADS, DN_HEAD_DIM, DN_HEAD_DIM), lambda i: (i, 0, 0, 0))
    return pl.pallas_call(
        kern,
        grid=(nb // nseq,),
        in_specs=[pl.BlockSpec((rows, QKV_WIDTH), lambda i: (i, 0)),
                  pl.BlockSpec((rows, LANES), lambda i: (i, COL_AB // LANES)),
                  pl.BlockSpec((rows, DN_WIDTH), lambda i: (i, COL_GATE // DN_WIDTH)),
                  sspec, const(cw), const(alog), const(dtb), const(onorm)],
        out_specs=[pl.BlockSpec((rows, DN_WIDTH), lambda i: (i, 0)), sspec],
        out_shape=[jax.ShapeDtypeStruct((nb * c, DN_WIDTH), F32),
                   jax.ShapeDtypeStruct(s0.shape, F32)],
        scratch_shapes=[pltpu.VMEM((rows, DN_WIDTH), F32), pltpu.VMEM((rows, DN_WIDTH), F32),
                        pltpu.VMEM((rows, DN_WIDTH), F32),
                        pltpu.VMEM((rows, LANES), F32), pltpu.VMEM((rows, LANES), F32)],
        compiler_params=_cparams(1),
        name="delta_sample",
    )(ext, ext, ext, s0, cw, alog, dtb, onorm)


def _ssm_prep_kernel(are_ref, aim_ref, dt_ref, bre_ref, bim_ref, lre_ref, lim_ref, bmat_ref):
    ar, ai, dt = are_ref[0], aim_ref[0], jnp.exp(dt_ref[0])
    mag = jnp.exp(ar * dt)
    lr = mag * jnp.cos(ai * dt)
    li = mag * jnp.sin(ai * dt)
    lre_ref[0] = lr
    lim_ref[0] = li
    den = ar * ar + ai * ai
    fr = ((lr - 1.0) * ar + li * ai) / den
    fi = (li * ar - (lr - 1.0) * ai) / den
    br, bi = bre_ref[0], bim_ref[0]
    bbr = fr * br - fi * bi
    bbi = fr * bi + fi * br
    lane_group = lax.broadcasted_iota(jnp.int32, (SSM_GROUP, SSM_NS), 1) // SSM_STATE
    for g in range(SSM_GROUPS):
        m = lane_group == g
        bmat_ref[0, g * SSM_GROUP:(g + 1) * SSM_GROUP, 0:SSM_NS] = jnp.where(m, bbr, 0.0)
        bmat_ref[0, g * SSM_GROUP:(g + 1) * SSM_GROUP, SSM_NS:] = jnp.where(m, bbi, 0.0)


def _ssm_prep(are, aim, dt, bre, bim):
    depth = are.shape[0]
    vec = pl.BlockSpec((1, 1, SSM_NS), lambda l: (l, 0, 0))
    mat = pl.BlockSpec((1, SSM_GROUP, SSM_NS), lambda l: (l, 0, 0))
    return pl.pallas_call(
        _ssm_prep_kernel,
        grid=(depth,),
        in_specs=[vec, vec, vec, mat, mat],
        out_specs=[vec, vec, pl.BlockSpec((1, SSM_WIDTH, 2 * SSM_NS), lambda l: (l, 0, 0))],
        out_shape=[jax.ShapeDtypeStruct((depth, 1, SSM_NS), F32), jax.ShapeDtypeStruct((depth, 1, SSM_NS), F32),
                   jax.ShapeDtypeStruct((depth, SSM_WIDTH, 2 * SSM_NS), F32)],
        compiler_params=_cparams(1),
        name="ssm_prep",
    )(are, aim, dt, bre, bim)


def _gelu_tanh(x):
    return 0.5 * x * (1.0 + jnp.tanh(math.sqrt(2.0 / math.pi) * (x + 0.044715 * (x * x * x))))


def _ssm_kernel(u_ref, h0_ref, bmat_ref, lre_ref, lim_ref, cre_ref, cim_ref, d_ref, gw_ref, gb_ref,
                y_ref, hl_ref, buf, h_scr, *, nb, nt, exact_in):
    i = pl.program_id(0)

    @pl.when(i == 0)
    def _():
        h_scr[...] = h0_ref[...]

    u = u_ref[...]
    buf[...] = _hdot(u, bmat_ref[...]) if exact_in else _bdot(u, bmat_ref[...])
    lr = jnp.broadcast_to(lre_ref[...], (SUBLANES, SSM_NS))
    li = jnp.broadcast_to(lim_ref[...], (SUBLANES, SSM_NS))
    for g in range(nb // SUBLANES):
        def step(t, h, g=g):
            r0 = pl.multiple_of(t * nb + g * SUBLANES, SUBLANES)
            bu = buf[pl.ds(r0, SUBLANES), :]
            hre, him = h[:, :SSM_NS], h[:, SSM_NS:]
            h = jnp.concatenate([lr * hre - li * him + bu[:, :SSM_NS],
                                 lr * him + li * hre + bu[:, SSM_NS:]], axis=1)
            buf[pl.ds(r0, SUBLANES), :] = h
            return h

        gs = slice(g * SUBLANES, (g + 1) * SUBLANES)
        h_fin = lax.fori_loop(0, nt, step, h_scr[gs, :])
        h_scr[gs, :] = h_fin
    hl_ref[...] = h_scr[...]
    y = _bdot(buf[:, :SSM_NS], cre_ref[...]) - _bdot(buf[:, SSM_NS:], cim_ref[...]) + d_ref[...] * u
    y = _gelu_tanh(y)
    y_ref[...] = y * jax.nn.sigmoid(_bdot(y, gw_ref[...]) + gb_ref[...])


def _ssm(u_tm, h0, bmat, lre, lim, cre, cim, dskip, gw, gb, nb, nt, exact_in):
    rows = nt * nb
    m = u_tm.shape[0]
    kern = functools.partial(_ssm_kernel, nb=nb, nt=nt, exact_in=exact_in)
    const = lambda a: pl.BlockSpec(a.shape, lambda i: (0,) * a.ndim)
    return pl.pallas_call(
        kern,
        grid=(m // rows,),
        in_specs=[pl.BlockSpec((rows, SSM_WIDTH), lambda i: (i, 0)), const(h0), const(bmat), const(lre),
                  const(lim), const(cre), const(cim), const(dskip), const(gw), const(gb)],
        out_specs=[pl.BlockSpec((rows, SSM_WIDTH), lambda i: (i, 0)), const(h0)],
        out_shape=[jax.ShapeDtypeStruct((m, SSM_WIDTH), F32), jax.ShapeDtypeStruct(h0.shape, F32)],
        scratch_shapes=[pltpu.VMEM((rows, 2 * SSM_NS), F32), pltpu.VMEM(h0.shape, F32)],
        compiler_params=_cparams(1),
        name="ssm",
    )(u_tm, h0, bmat, lre, lim, cre, cim, dskip, gw, gb)


def _pool_windows(xfull):
    s2 = xfull + pltpu.roll(xfull, 1, 0)
    s4 = s2 + pltpu.roll(s2, 2, 0)
    s8 = s4 + pltpu.roll(s4, 4, 0)
    s16 = s8 + pltpu.roll(s8, 8, 0)
    return s2, s4, s8, s16


def _pool_mix(sums, x, pos, w_ref, scale_ref):
    lane = lax.broadcasted_iota(jnp.int32, (1, POOL_WIDTH), 1) // POOL_GROUP
    win = None
    for gidx in reversed(range(len(POOL_WINDOWS))):
        cnt = jnp.minimum(pos + 1, POOL_WINDOWS[gidx]).astype(F32)
        term = sums[gidx] / cnt
        win = term if win is None else jnp.where(lane == gidx, term, win)
    r = win - x
    return _bdot(r, w_ref[...]) * scale_ref[...]


def _pool_prompt_kernel(u_ref, st_ref, w_ref, scale_ref, y_ref, tail, *, tb, pos0):
    i = pl.program_id(1)
    halo = 2 * SUBLANES

    @pl.when(i == 0)
    def _():
        tail[...] = st_ref[0]

    x = u_ref[0]
    xfull = jnp.concatenate([tail[...], x], axis=0)
    tail[...] = x[tb - halo:, :]
    sums = [s[halo:] for s in _pool_windows(xfull)]
    pos = pos0 + i * tb + lax.broadcasted_iota(jnp.int32, (tb, 1), 0)
    y_ref[0] = _pool_mix(sums, x, pos, w_ref, scale_ref)


def _pool_prompt(p3, st, wbd, scale, tb, pos0):
    nb, t, _ = p3.shape
    kern = functools.partial(_pool_prompt_kernel, tb=tb, pos0=pos0)
    const = lambda a: pl.BlockSpec(a.shape, lambda b, i: (0,) * a.ndim)
    return pl.pallas_call(
        kern,
        grid=(nb, t // tb),
        in_specs=[pl.BlockSpec((1, tb, POOL_WIDTH), lambda b, i: (b, i, COL_POOL // POOL_WIDTH)),
                  pl.BlockSpec((1, 2 * SUBLANES, POOL_WIDTH), lambda b, i: (b, 0, 0)),
                  const(wbd), const(scale)],
        out_specs=pl.BlockSpec((1, tb, POOL_WIDTH), lambda b, i: (b, i, 0)),
        out_shape=jax.ShapeDtypeStruct((nb, t, POOL_WIDTH), F32),
        scratch_shapes=[pltpu.VMEM((2 * SUBLANES, POOL_WIDTH), F32)],
        compiler_params=_cparams(2),
        name="pool_prompt",
    )(p3, st, wbd, scale)


def _pool_sample_kernel(x_ref, w_ref, scale_ref, y_ref, *, group, first, pos0):
    x = x_ref[...]
    rows = x.shape[0]
    sums = _pool_windows(x)
    pos = pos0 + (lax.broadcasted_iota(jnp.int32, (rows, 1), 0) % group) - first
    y_ref[...] = _pool_mix(sums, x, jnp.maximum(pos, 0), w_ref, scale_ref)


def _pool_sample(ext, wbd, scale, group, first, pos0):
    kern = functools.partial(_pool_sample_kernel, group=group, first=first, pos0=pos0)
    const = lambda a: pl.BlockSpec(a.shape, lambda i: (0,) * a.ndim)
    return pl.pallas_call(
        kern,
        grid=(1,),
        in_specs=[const(ext), const(wbd), const(scale)],
        out_specs=const(ext),
        out_shape=jax.ShapeDtypeStruct(ext.shape, F32),
        compiler_params=_cparams(1),
        name="pool_sample",
    )(ext, wbd, scale)


def _block_diag(blocks):
    g, r, c = blocks.shape
    eye = jnp.eye(g, dtype=blocks.dtype)
    return (eye[:, None, :, None] * blocks[:, :, None, :]).reshape(g * r, g * c)


def kernel(x_prompt, x_sample, state_delta, state_conv, state_ssm_re, state_ssm_im, state_pool, norm_mix_pre, norm_mix_post, norm_ffn_pre, norm_ffn_post, w_in, conv_w, dn_a_log, dn_dt_bias, dn_out_norm, ssm_a_re, ssm_a_im, ssm_log_dt, ssm_b_re, ssm_b_im, ssm_c_re, ssm_c_im, ssm_d, ssm_glu_w, ssm_glu_b, pool_w, pool_scale, w_out, ffn_w_gate, ffn_w_up, ffn_w_down):
    depth = w_in.shape[0]
    bp, tp, _ = x_prompt.shape
    bs, ts, _ = x_sample.shape

    w_in_r = jnp.concatenate([w_in[:, :, :QKV_WIDTH], w_in[:, :, _OFF_G:], w_in[:, :, _OFF_A:_OFF_G],
                              jnp.zeros((depth, D_MODEL, LANES - 2 * DN_HEADS), F32)], axis=2).astype(BF16)
    w_out_b = w_out.astype(BF16)
    wg_b, wu_b, wd_b = ffn_w_gate.astype(BF16), ffn_w_up.astype(BF16), ffn_w_down.astype(BF16)
    row = lambda a: a.reshape(depth, 1, -1)
    nmp, nmo, nfp, nfo = row(norm_mix_pre), row(norm_mix_post), row(norm_ffn_pre), row(norm_ffn_post)
    alog = jnp.pad(dn_a_log, ((0, 0), (0, LANES - DN_HEADS))).reshape(depth, 1, LANES)
    dtb = jnp.pad(dn_dt_bias, ((0, 0), (0, LANES - DN_HEADS))).reshape(depth, 1, LANES)
    onorm = row(dn_out_norm)
    dt_full = jnp.repeat(ssm_log_dt, SSM_STATE, axis=1).reshape(depth, 1, SSM_NS)
    b_t = lambda b: jnp.transpose(b, (0, 3, 1, 2)).reshape(depth, SSM_GROUP, SSM_NS)
    lam_re, lam_im, bmat = _ssm_prep(ssm_a_re.reshape(depth, 1, SSM_NS), ssm_a_im.reshape(depth, 1, SSM_NS),
                                     dt_full, b_t(ssm_b_re), b_t(ssm_b_im))
    c_bd = lambda cc: jax.vmap(_block_diag)(jnp.transpose(cc, (0, 1, 3, 2))).astype(BF16)
    cre, cim = c_bd(ssm_c_re), c_bd(ssm_c_im)
    dskip, glu_b = row(ssm_d), row(ssm_glu_b)
    glu_w = ssm_glu_w.astype(BF16)
    pool_bd = jax.vmap(_block_diag)(pool_w).astype(BF16)
    pscale = row(pool_scale)

    xp = x_prompt.reshape(bp * tp, D_MODEL)
    xs = x_sample.reshape(bs * ts, D_MODEL)
    zero_conv = jnp.zeros((bp, SUBLANES, QKV_WIDTH), F32)
    zero_delta = jnp.zeros((bp, DN_HEADS, DN_HEAD_DIM, DN_HEAD_DIM), F32)
    zero_h = jnp.zeros((bp, 2 * SSM_NS), F32)
    zero_pool = jnp.zeros((bp, 2 * SUBLANES, POOL_WIDTH), F32)
    pad_rows = SAMPLE_CHUNK - ts - (DN_CONV - 1)
    pool_group = 24
    pool_first = 1 + POOL_BUF

    outs_p, outs_s = [], []
    for l in range(depth):
        proj = _in_proj(xp, nmp[l], w_in_r[l], 512)
        p3 = proj.reshape(bp, tp, PROJ_WIDTH)
        o_dn, delta_new = _delta_prompt(p3, zero_conv, zero_delta, conv_w[l], alog[l], dtb[l], onorm[l], 256)
        u_tm = jnp.transpose(p3[:, :, COL_SSM:COL_POOL], (1, 0, 2)).reshape(tp * bp, SSM_WIDTH)
        y_tm, h_fin = _ssm(u_tm, zero_h, bmat[l], lam_re[l], lam_im[l], cre[l], cim[l], dskip[l], glu_w[l],
                           glu_b[l], bp, 128, False)
        o_ssm = jnp.transpose(y_tm.reshape(tp, bp, SSM_WIDTH), (1, 0, 2)).reshape(bp * tp, SSM_WIDTH)
        o_pool = _pool_prompt(p3, zero_pool, pool_bd[l], pscale[l], 512, 0)
        xp = _post_mix(o_dn.reshape(bp * tp, DN_WIDTH), o_ssm, o_pool.reshape(bp * tp, POOL_WIDTH), xp,
                       w_out_b[l], nmo[l], nfp[l], wg_b[l], wu_b[l], wd_b[l], nfo[l], 256)
        outs_p.append((delta_new, p3[:, tp - (DN_CONV - 1):, :QKV_WIDTH],
                       h_fin[:, :SSM_NS].reshape(bp, SSM_GROUPS, SSM_STATE),
                       h_fin[:, SSM_NS:].reshape(bp, SSM_GROUPS, SSM_STATE),
                       p3[:, tp - POOL_BUF:, COL_POOL:COL_AB]))

        proj = _in_proj(xs, nmp[l], w_in_r[l], 256)
        s3 = proj.reshape(bs, ts, PROJ_WIDTH)
        head = jnp.concatenate([jnp.zeros((bs, pad_rows, QKV_WIDTH), F32), state_conv[l]], axis=1)
        head = jnp.pad(head, ((0, 0), (0, 0), (0, PROJ_WIDTH - QKV_WIDTH)))
        ext = jnp.concatenate([head, s3], axis=1).reshape(bs * SAMPLE_CHUNK, PROJ_WIDTH)
        o_ext, delta_new = _delta_sample(ext, state_delta[l], conv_w[l], alog[l], dtb[l], onorm[l], 16)
        o_dn = o_ext.reshape(bs, SAMPLE_CHUNK, DN_WIDTH)[:, SAMPLE_CHUNK - ts:].reshape(bs * ts, DN_WIDTH)
        u_s = s3[:, :, COL_SSM:COL_POOL]
        u_tm = jnp.transpose(u_s, (1, 0, 2)).reshape(ts * bs, SSM_WIDTH)
        h0 = jnp.concatenate([state_ssm_re[l].reshape(bs, SSM_NS), state_ssm_im[l].reshape(bs, SSM_NS)], axis=1)
        y_tm, h_fin = _ssm(u_tm, h0, bmat[l], lam_re[l], lam_im[l], cre[l], cim[l], dskip[l], glu_w[l],
                           glu_b[l], bs, ts, True)
        o_ssm = jnp.transpose(y_tm.reshape(ts, bs, SSM_WIDTH), (1, 0, 2)).reshape(bs * ts, SSM_WIDTH)
        pool_u = s3[:, :, COL_POOL:COL_AB]
        pext = jnp.concatenate([jnp.zeros((bs, 1, POOL_WIDTH), F32), state_pool[l], pool_u,
                                jnp.zeros((bs, pool_group - pool_first - ts, POOL_WIDTH), F32)], axis=1)
        y_ext = _pool_sample(pext.reshape(bs * pool_group, POOL_WIDTH), pool_bd[l], pscale[l], pool_group,
                             pool_first, PAST_LEN)
        o_pool = y_ext.reshape(bs, pool_group, POOL_WIDTH)[:, pool_first:pool_first + ts].reshape(bs * ts, POOL_WIDTH)
        xs = _post_mix(o_dn, o_ssm, o_pool, xs, w_out_b[l], nmo[l], nfp[l], wg_b[l], wu_b[l], wd_b[l], nfo[l], 256)
        outs_s.append((delta_new, s3[:, ts - (DN_CONV - 1):, :QKV_WIDTH],
                       h_fin[:, :SSM_NS].reshape(bs, SSM_GROUPS, SSM_STATE),
                       h_fin[:, SSM_NS:].reshape(bs, SSM_GROUPS, SSM_STATE),
                       jnp.concatenate([state_pool[l][:, ts:], pool_u], axis=1)))

    stack = lambda outs, k: jnp.stack([o[k] for o in outs])
    return (xp.reshape(bp, tp, D_MODEL), xs.reshape(bs, ts, D_MODEL),
            stack(outs_p, 0), stack(outs_p, 1), stack(outs_p, 2), stack(outs_p, 3), stack(outs_p, 4),
            stack(outs_s, 0), stack(outs_s, 1), stack(outs_s, 2), stack(outs_s, 3), stack(outs_s, 4))
```

```python
import functools
import math

import jax
import jax.numpy as jnp
from jax import lax
from jax.experimental import pallas as pl
from jax.experimental.pallas import tpu as pltpu

F32 = jnp.float32
BF16 = jnp.bfloat16
HIGHEST = lax.Precision.HIGHEST

D_MODEL = 1024
DN_HEADS = 4
DN_HEAD_DIM = 128
DN_WIDTH = DN_HEADS * DN_HEAD_DIM
DN_CONV = 4
QKV_WIDTH = 3 * DN_WIDTH
SSM_WIDTH = 256
SSM_GROUP = 16
SSM_GROUPS = 16
SSM_STATE = 64
SSM_NS = SSM_GROUPS * SSM_STATE
POOL_WIDTH = 256
POOL_WINDOWS = (2, 4, 8, 16)
POOL_GROUP = 64
POOL_BUF = 15
D_FF = 2816
EPS = 1e-6
PAST_LEN = 16384

COL_GATE = QKV_WIDTH
COL_SSM = COL_GATE + DN_WIDTH
COL_POOL = COL_SSM + SSM_WIDTH
COL_AB = COL_POOL + POOL_WIDTH
LANES = 128
SUBLANES = 8
PROJ_WIDTH = COL_AB + LANES

_OFF_A = QKV_WIDTH
_OFF_G = _OFF_A + 2 * DN_HEADS

VMEM_LIMIT_BYTES = 56 * 1024 * 1024

DELTA_CHUNK = 64
SAMPLE_CHUNK = 8
DELTA_SUB = 16


def _cparams(n_axes):
    return pltpu.CompilerParams(dimension_semantics=("arbitrary",) * n_axes,
                                vmem_limit_bytes=VMEM_LIMIT_BYTES)


def _bdot(a, b):
    return jnp.dot(a.astype(BF16), b.astype(BF16), preferred_element_type=F32)


def _bdot_nt(a, b):
    return lax.dot_general(a.astype(BF16), b.astype(BF16), (((1,), (1,)), ((), ())),
                           preferred_element_type=F32)


def _bdot_tn(a, b):
    return lax.dot_general(a.astype(BF16), b.astype(BF16), (((0,), (0,)), ((), ())),
                           preferred_element_type=F32)


def _hdot(a, b):
    return jnp.dot(a, b, precision=HIGHEST, preferred_element_type=F32)


def _rms(x, w):
    return x * lax.rsqrt(jnp.mean(x * x, axis=-1, keepdims=True) + EPS) * w


def _silu(x):
    return x * jax.nn.sigmoid(x)


def _in_proj_kernel(x_ref, nw_ref, w_ref, o_ref):
    h = _rms(x_ref[...], nw_ref[...])
    o_ref[...] = _bdot(h, w_ref[...])


def _in_proj(x, nw, w, tm):
    m = x.shape[0]
    return pl.pallas_call(
        _in_proj_kernel,
        grid=(m // tm,),
        in_specs=[pl.BlockSpec((tm, D_MODEL), lambda i: (i, 0)),
                  pl.BlockSpec((1, D_MODEL), lambda i: (0, 0)),
                  pl.BlockSpec((D_MODEL, PROJ_WIDTH), lambda i: (0, 0), pipeline_mode=pl.Buffered(1))],
        out_specs=pl.BlockSpec((tm, PROJ_WIDTH), lambda i: (i, 0)),
        out_shape=jax.ShapeDtypeStruct((m, PROJ_WIDTH), F32),
        compiler_params=_cparams(1),
        name="in_proj",
    )(x, nw, w)


def _post_mix_kernel(odn_ref, ossm_ref, opool_ref, x_ref, wo_ref, nmp_ref, nfp_ref, wg_ref, wu_ref, wd_ref,
                     nfo_ref, o_ref):
    mix = (_bdot(odn_ref[...], wo_ref[0:DN_WIDTH, :])
           + _bdot(ossm_ref[...], wo_ref[DN_WIDTH:DN_WIDTH + SSM_WIDTH, :])
           + _bdot(opool_ref[...], wo_ref[DN_WIDTH + SSM_WIDTH:, :]))
    x1 = x_ref[...] + _rms(mix, nmp_ref[...])
    h = _rms(x1, nfp_ref[...]).astype(BF16)
    g = jnp.dot(h, wg_ref[...], preferred_element_type=F32)
    u = jnp.dot(h, wu_ref[...], preferred_element_type=F32)
    f = _bdot(_silu(g) * u, wd_ref[...])
    o_ref[...] = x1 + _rms(f, nfo_ref[...])


def _post_mix(odn, ossm, opool, x, wo, nmp, nfp, wg, wu, wd, nfo, tm):
    m = x.shape[0]
    row = lambda w: pl.BlockSpec((tm, w), lambda i: (i, 0))
    full = lambda a: pl.BlockSpec(a.shape, lambda i: (0,) * a.ndim, pipeline_mode=pl.Buffered(1))
    return pl.pallas_call(
        _post_mix_kernel,
        grid=(m // tm,),
        in_specs=[row(DN_WIDTH), row(SSM_WIDTH), row(POOL_WIDTH), row(D_MODEL),
                  full(wo), full(nmp), full(nfp), full(wg), full(wu), full(wd), full(nfo)],
        out_specs=row(D_MODEL),
        out_shape=jax.ShapeDtypeStruct((m, D_MODEL), F32),
        compiler_params=_cparams(1),
        name="post_mix",
    )(odn, ossm, opool, x, wo, nmp, nfp, wg, wu, wd, nfo)


def _mm(a, b):
    return jnp.dot(a, b, preferred_element_type=F32)


def _split2(x):
    hi = x.astype(BF16)
    return hi, (x - hi.astype(F32)).astype(BF16)


def _group_cumsum(x, group, reverse=False):
    rows = x.shape[0]
    pos = lax.broadcasted_iota(jnp.int32, x.shape, 0) % group
    d = 1
    while d < group:
        if reverse:
            x = x + jnp.where(pos + d < group, pltpu.roll(x, rows - d, 0), 0.0)
        else:
            x = x + jnp.where(pos >= d, pltpu.roll(x, d, 0), 0.0)
        d *= 2
    return x


def _chunk_masks(rows, blk):
    r = lax.broadcasted_iota(jnp.int32, (rows, rows), 0)
    s = lax.broadcasted_iota(jnp.int32, (rows, rows), 1)
    d = r - s
    if blk < rows:
        d = jnp.where(r // blk == s // blk, d, -1)
    sub = None
    if blk > DELTA_SUB:
        sub = r // DELTA_SUB == s // DELTA_SUB
    return d >= 0, d > 0, (r == s).astype(F32), sub


def _neumann(a, eye, index):
    t = [eye - x for x in a]
    p = a
    n = 2
    while n < index:
        p = [_bdot(x, x) for x in p]
        t = [x + _bdot(x, y) for x, y in zip(t, p)]
        n *= 2
    return t


def _unit_lower_inverse(a, eye, sub, index):
    if sub is None:
        return _neumann(a, eye, index)
    assert index <= 4 * DELTA_SUB
    d = [jnp.where(sub, x, 0.0) for x in a]
    td = _neumann(d, eye, DELTA_SUB)
    n = [_bdot(t, x - y) for t, x, y in zip(td, a, d)]
    n2 = [_bdot(x, x) for x in n]
    m = [_bdot(eye - x, eye + y) for x, y in zip(n, n2)]
    return [_bdot(x, t) for x, t in zip(m, td)]


def _delta_local(probs, masks, index):
    incl, strict, eye, sub = masks
    rows = probs[0][0].shape[0]
    kb = [k * beta for (_, k, _, beta, _, _, _) in probs]
    kq = [_bdot_nt(jnp.concatenate([x, q], axis=0), k) for x, (q, k, _, _, _, _, _) in zip(kb, probs)]
    decay = [jnp.where(incl, jnp.exp(jnp.where(incl, gam - grow, 0.0)), 0.0) for (_, _, _, _, gam, _, grow) in probs]
    a = [jnp.where(strict, x[:rows] * d, 0.0) for x, d in zip(kq, decay)]
    qk = [x[rows:] * d for x, d in zip(kq, decay)]
    egam = [jnp.exp(p[4]) for p in probs]
    rhs = [jnp.concatenate([p[2] * p[3], x * e], axis=1) for p, x, e in zip(probs, kb, egam)]
    t = _unit_lower_inverse(a, eye, sub, index)
    sol = [_bdot(x, r) for x, r in zip(t, rhs)]
    a_sp = [_split2(x) for x in a]
    s_sp = [_split2(x) for x in sol]
    asol = [_mm(ah, sh) + _mm(ah, sl) + _mm(al, sh) for (ah, al), (sh, sl) in zip(a_sp, s_sp)]
    resid = [r - s - x for r, s, x in zip(rhs, sol, asol)]
    sol = [s + _bdot(x, r) for s, x, r in zip(sol, t, resid)]
    return [(s[:, :DN_HEAD_DIM], s[:, DN_HEAD_DIM:], x, p[0] * e, p[1] * jnp.exp(p[5]))
            for s, x, p, e in zip(sol, qk, probs, egam)]


def _delta_front(y, ab, alog, dtb):
    y = _silu(y)
    qs, ks = [], []
    for h in range(DN_HEADS):
        qh = y[:, h * DN_HEAD_DIM:(h + 1) * DN_HEAD_DIM]
        kh = y[:, DN_WIDTH + h * DN_HEAD_DIM:DN_WIDTH + (h + 1) * DN_HEAD_DIM]
        qs.append(qh * lax.rsqrt(jnp.sum(qh * qh, axis=-1, keepdims=True) + EPS) * (DN_HEAD_DIM ** -0.5))
        ks.append(kh * lax.rsqrt(jnp.sum(kh * kh, axis=-1, keepdims=True) + EPS))
    v = y[:, 2 * DN_WIDTH:]
    z = ab + dtb
    softplus = jnp.maximum(z, 0.0) + jnp.log1p(jnp.exp(-jnp.abs(z)))
    logg = -jnp.exp(alog) * softplus
    beta = jax.nn.sigmoid(ab)
    return qs, ks, v, logg, beta


def _delta_out(o, gate, onorm):
    return o * lax.rsqrt(jnp.mean(o * o, axis=-1, keepdims=True) + EPS) * onorm * _silu(gate)


def _conv4(xfull, w, rows, off):
    y = pltpu.roll(xfull, 3, 0)[off:off + rows] * w[0:1]
    y = y + pltpu.roll(xfull, 2, 0)[off:off + rows] * w[1:2]
    y = y + pltpu.roll(xfull, 1, 0)[off:off + rows] * w[2:3]
    return y + xfull[off:off + rows] * w[3:4]


def _delta_prompt_kernel(qkv_ref, ab_ref, gate_ref, cst_ref, s0_ref, cw_ref, alog_ref, dtb_ref, onorm_ref,
                         o_ref, sfin_ref, tail, s_scr, *, tb, c):
    i = pl.program_id(1)

    @pl.when(i == 0)
    def _():
        s_scr[...] = s0_ref[0]
        tail[...] = cst_ref[0]

    x = qkv_ref[0]
    xfull = jnp.concatenate([tail[...], x], axis=0)
    y = _conv4(xfull, cw_ref[...], tb, SUBLANES)
    tail[...] = x[tb - SUBLANES:, :]
    qs, ks, v, logg, beta = _delta_front(y, ab_ref[0], alog_ref[...], dtb_ref[...])
    gam = _group_cumsum(logg, c)
    gexc = _group_cumsum(logg, c, reverse=True) - logg
    gam_t = gam.T
    masks = _chunk_masks(c, c)
    nchunk = tb // c
    heads = range(DN_HEADS)
    hsl = [slice(h * DN_HEAD_DIM, (h + 1) * DN_HEAD_DIM) for h in heads]
    probs = []
    for j in range(nchunk):
        rs = slice(j * c, (j + 1) * c)
        for h in heads:
            probs.append((qs[h][rs], ks[h][rs], v[rs, hsl[h]], beta[rs, DN_HEADS + h:DN_HEADS + h + 1],
                          gam[rs, h:h + 1], gexc[rs, h:h + 1], gam_t[h:h + 1, rs]))
    local = _delta_local(probs, masks, c)
    s = [s_scr[h] for h in heads]
    outs = [[] for _ in heads]
    for j in range(nchunk):
        loc = local[j * DN_HEADS:(j + 1) * DN_HEADS]
        ks_ = [_bdot(jnp.concatenate([loc[h][1], loc[h][3]], axis=0), s[h]) for h in heads]
        u = [loc[h][0] - ks_[h][:c] for h in heads]
        for h in heads:
            outs[h].append(ks_[h][c:] + _bdot(loc[h][2], u[h]))
        last = (j + 1) * c - 1
        s = [s[h] * jnp.exp(gam[last:last + 1, h:h + 1]) + _bdot_tn(loc[h][4], u[h]) for h in heads]
    gate = gate_ref[0]
    for h in heads:
        s_scr[h] = s[h]
        o_ref[0, :, hsl[h]] = _delta_out(jnp.concatenate(outs[h], axis=0), gate[:, hsl[h]], onorm_ref[...])
    sfin_ref[0] = s_scr[...]


def _delta_prompt(p3, cst, s0, cw, alog, dtb, onorm, tb):
    nb, t, _ = p3.shape
    kern = functools.partial(_delta_prompt_kernel, tb=tb, c=DELTA_CHUNK)
    const = lambda a: pl.BlockSpec(a.shape, lambda b, i: (0,) * a.ndim)
    return pl.pallas_call(
        kern,
        grid=(nb, t // tb),
        in_specs=[pl.BlockSpec((1, tb, QKV_WIDTH), lambda b, i: (b, i, 0)),
                  pl.BlockSpec((1, tb, LANES), lambda b, i: (b, i, COL_AB // LANES)),
                  pl.BlockSpec((1, tb, DN_WIDTH), lambda b, i: (b, i, COL_GATE // DN_WIDTH)),
                  pl.BlockSpec((1, SUBLANES, QKV_WIDTH), lambda b, i: (b, 0, 0)),
                  pl.BlockSpec((1, DN_HEADS, DN_HEAD_DIM, DN_HEAD_DIM), lambda b, i: (b, 0, 0, 0)),
                  const(cw), const(alog), const(dtb), const(onorm)],
        out_specs=[pl.BlockSpec((1, tb, DN_WIDTH), lambda b, i: (b, i, 0)),
                   pl.BlockSpec((1, DN_HEADS, DN_HEAD_DIM, DN_HEAD_DIM), lambda b, i: (b, 0, 0, 0))],
        out_shape=[jax.ShapeDtypeStruct((nb, t, DN_WIDTH), F32),
                   jax.ShapeDtypeStruct((nb, DN_HEADS, DN_HEAD_DIM, DN_HEAD_DIM), F32)],
        scratch_shapes=[pltpu.VMEM((SUBLANES, QKV_WIDTH), F32),
                        pltpu.VMEM((DN_HEADS, DN_HEAD_DIM, DN_HEAD_DIM), F32)],
        compiler_params=_cparams(2),
        name="delta_prompt",
    )(p3, p3, p3, cst, s0, cw, alog, dtb, onorm)


def _delta_sample_kernel(qkv_ref, ab_ref, gate_ref, s0_ref, cw_ref, alog_ref, dtb_ref, onorm_ref,
                         o_ref, snew_ref, *, nseq, c, first):
    rows = nseq * c
    x = qkv_ref[...]
    y = _conv4(x, cw_ref[...], rows, 0)
    qs, ks, v, logg, beta = _delta_front(y, ab_ref[...], alog_ref[...], dtb_ref[...])
    valid = (lax.broadcasted_iota(jnp.int32, (rows, LANES), 0) % c) >= first
    logg = jnp.where(valid, logg, 0.0)
    beta = jnp.where(valid, beta, 0.0)
    gam = _group_cumsum(logg, c)
    gexc = _group_cumsum(logg, c, reverse=True) - logg
    gam_t = gam.T
    masks = _chunk_masks(rows, c)
    gate = gate_ref[...]
    heads = range(DN_HEADS)
    hsl = [slice(h * DN_HEAD_DIM, (h + 1) * DN_HEAD_DIM) for h in heads]
    local = _delta_local([(qs[h], ks[h], v[:, hsl[h]], beta[:, DN_HEADS + h:DN_HEADS + h + 1], gam[:, h:h + 1],
                           gexc[:, h:h + 1], gam_t[h:h + 1, :]) for h in heads], masks, c)
    pairs = [(b, h) for b in range(nseq) for h in heads]
    rsl = [slice(b * c, (b + 1) * c) for b in range(nseq)]
    ks_ = {(b, h): _bdot(jnp.concatenate([local[h][1][rsl[b]], local[h][3][rsl[b]]], axis=0), s0_ref[b, h])
           for b, h in pairs}
    u = {(b, h): local[h][0][rsl[b]] - ks_[b, h][:c] for b, h in pairs}
    for b, h in pairs:
        last = (b + 1) * c - 1
        snew_ref[b, h] = (s0_ref[b, h] * jnp.exp(gam[last:last + 1, h:h + 1])
                          + _bdot_tn(local[h][4][rsl[b]], u[b, h]))
    for h in heads:
        o = (jnp.concatenate([ks_[b, h][c:] for b in range(nseq)], axis=0)
             + _bdot(local[h][2], jnp.concatenate([u[b, h] for b in range(nseq)], axis=0)))
        o_ref[:, hsl[h]] = _delta_out(o, gate[:, hsl[h]], onorm_ref[...])


def _delta_sample(ext, s0, cw, alog, dtb, onorm, nseq, first):
    c = SAMPLE_CHUNK
    nb = s0.shape[0]
    rows = nseq * c
    kern = functools.partial(_delta_sample_kernel, nseq=nseq, c=c, first=first)
    const = lambda a: pl.BlockSpec(a.shape, lambda i: (0,) * a.ndim)
    sspec = pl.BlockSpec((nseq, DN_HEADS, DN_HEAD_DIM, DN_HEAD_DIM), lambda i: (i, 0, 0, 0))
    return pl.pallas_call(
        kern,
        grid=(nb // nseq,),
        in_specs=[pl.BlockSpec((rows, QKV_WIDTH), lambda i: (i, 0)),
                  pl.BlockSpec((rows, LANES), lambda i: (i, COL_AB // LANES)),
                  pl.BlockSpec((rows, DN_WIDTH), lambda i: (i, COL_GATE // DN_WIDTH)),
                  sspec, const(cw), const(alog), const(dtb), const(onorm)],
        out_specs=[pl.BlockSpec((rows, DN_WIDTH), lambda i: (i, 0)), sspec],
        out_shape=[jax.ShapeDtypeStruct((nb * c, DN_WIDTH), F32),
                   jax.ShapeDtypeStruct(s0.shape, F32)],
        compiler_params=_cparams(1),
        name="delta_sample",
    )(ext, ext, ext, s0, cw, alog, dtb, onorm)


def _ssm_prep_kernel(are_ref, aim_ref, dt_ref, bre_ref, bim_ref, lre_ref, lim_ref, bmat_ref):
    ar, ai, dt = are_ref[0], aim_ref[0], jnp.exp(dt_ref[0])
    mag = jnp.exp(ar * dt)
    lr = mag * jnp.cos(ai * dt)
    li = mag * jnp.sin(ai * dt)
    lre_ref[0] = lr
    lim_ref[0] = li
    den = ar * ar + ai * ai
    fr = ((lr - 1.0) * ar + li * ai) / den
    fi = (li * ar - (lr - 1.0) * ai) / den
    br, bi = bre_ref[0], bim_ref[0]
    bbr = fr * br - fi * bi
    bbi = fr * bi + fi * br
    lane_group = lax.broadcasted_iota(jnp.int32, (SSM_GROUP, SSM_NS), 1) // SSM_STATE
    for g in range(SSM_GROUPS):
        m = lane_group == g
        bmat_ref[0, g * SSM_GROUP:(g + 1) * SSM_GROUP, 0:SSM_NS] = jnp.where(m, bbr, 0.0)
        bmat_ref[0, g * SSM_GROUP:(g + 1) * SSM_GROUP, SSM_NS:] = jnp.where(m, bbi, 0.0)


def _ssm_prep(are, aim, dt, bre, bim):
    depth = are.shape[0]
    vec = pl.BlockSpec((1, 1, SSM_NS), lambda l: (l, 0, 0))
    mat = pl.BlockSpec((1, SSM_GROUP, SSM_NS), lambda l: (l, 0, 0))
    return pl.pallas_call(
        _ssm_prep_kernel,
        grid=(depth,),
        in_specs=[vec, vec, vec, mat, mat],
        out_specs=[vec, vec, pl.BlockSpec((1, SSM_WIDTH, 2 * SSM_NS), lambda l: (l, 0, 0))],
        out_shape=[jax.ShapeDtypeStruct((depth, 1, SSM_NS), F32), jax.ShapeDtypeStruct((depth, 1, SSM_NS), F32),
                   jax.ShapeDtypeStruct((depth, SSM_WIDTH, 2 * SSM_NS), F32)],
        compiler_params=_cparams(1),
        name="ssm_prep",
    )(are, aim, dt, bre, bim)


def _gelu_tanh(x):
    return 0.5 * x * (1.0 + jnp.tanh(math.sqrt(2.0 / math.pi) * (x + 0.044715 * (x * x * x))))


def _ssm_kernel(u_ref, h0_ref, bmat_ref, lre_ref, lim_ref, cre_ref, cim_ref, d_ref, gw_ref, gb_ref,
                y_ref, hl_ref, buf, h_scr, *, nb, nt, exact_in):
    i = pl.program_id(0)

    @pl.when(i == 0)
    def _():
        h_scr[...] = h0_ref[...]

    u = u_ref[...]
    buf[...] = _hdot(u, bmat_ref[...]) if exact_in else _bdot(u, bmat_ref[...])
    lr = jnp.broadcast_to(lre_ref[...], (SUBLANES, SSM_NS))
    li = jnp.broadcast_to(lim_ref[...], (SUBLANES, SSM_NS))
    for g in range(nb // SUBLANES):
        def step(t, h, g=g):
            r0 = pl.multiple_of(t * nb + g * SUBLANES, SUBLANES)
            bu = buf[pl.ds(r0, SUBLANES), :]
            hre, him = h[:, :SSM_NS], h[:, SSM_NS:]
            h = jnp.concatenate([lr * hre - li * him + bu[:, :SSM_NS],
                                 lr * him + li * hre + bu[:, SSM_NS:]], axis=1)
            buf[pl.ds(r0, SUBLANES), :] = h
            return h

        gs = slice(g * SUBLANES, (g + 1) * SUBLANES)
        h_fin = lax.fori_loop(0, nt, step, h_scr[gs, :])
        h_scr[gs, :] = h_fin
    hl_ref[...] = h_scr[...]
    y = _bdot(buf[:, :SSM_NS], cre_ref[...]) - _bdot(buf[:, SSM_NS:], cim_ref[...]) + d_ref[...] * u
    y = _gelu_tanh(y)
    y_ref[...] = y * jax.nn.sigmoid(_bdot(y, gw_ref[...]) + gb_ref[...])


def _ssm(u_tm, h0, bmat, lre, lim, cre, cim, dskip, gw, gb, nb, nt, exact_in):
    rows = nt * nb
    m = u_tm.shape[0]
    kern = functools.partial(_ssm_kernel, nb=nb, nt=nt, exact_in=exact_in)
    const = lambda a: pl.BlockSpec(a.shape, lambda i: (0,) * a.ndim)
    return pl.pallas_call(
        kern,
        grid=(m // rows,),
        in_specs=[pl.BlockSpec((rows, SSM_WIDTH), lambda i: (i, 0)), const(h0), const(bmat), const(lre),
                  const(lim), const(cre), const(cim), const(dskip), const(gw), const(gb)],
        out_specs=[pl.BlockSpec((rows, SSM_WIDTH), lambda i: (i, 0)), const(h0)],
        out_shape=[jax.ShapeDtypeStruct((m, SSM_WIDTH), F32), jax.ShapeDtypeStruct(h0.shape, F32)],
        scratch_shapes=[pltpu.VMEM((rows, 2 * SSM_NS), F32), pltpu.VMEM(h0.shape, F32)],
        compiler_params=_cparams(1),
        name="ssm",
    )(u_tm, h0, bmat, lre, lim, cre, cim, dskip, gw, gb)


def _pool_windows(xfull):
    s2 = xfull + pltpu.roll(xfull, 1, 0)
    s4 = s2 + pltpu.roll(s2, 2, 0)
    s8 = s4 + pltpu.roll(s4, 4, 0)
    s16 = s8 + pltpu.roll(s8, 8, 0)
    return s2, s4, s8, s16


def _pool_mix(sums, x, pos, w_ref, scale_ref):
    lane = lax.broadcasted_iota(jnp.int32, (1, POOL_WIDTH), 1) // POOL_GROUP
    win = None
    for gidx in reversed(range(len(POOL_WINDOWS))):
        cnt = jnp.minimum(pos + 1, POOL_WINDOWS[gidx]).astype(F32)
        term = sums[gidx] / cnt
        win = term if win is None else jnp.where(lane == gidx, term, win)
    r = win - x
    return _bdot(r, w_ref[...]) * scale_ref[...]


def _pool_prompt_kernel(u_ref, st_ref, w_ref, scale_ref, y_ref, tail, *, tb, pos0):
    i = pl.program_id(1)
    halo = 2 * SUBLANES

    @pl.when(i == 0)
    def _():
        tail[...] = st_ref[0]

    x = u_ref[0]
    xfull = jnp.concatenate([tail[...], x], axis=0)
    tail[...] = x[tb - halo:, :]
    sums = [s[halo:] for s in _pool_windows(xfull)]
    pos = pos0 + i * tb + lax.broadcasted_iota(jnp.int32, (tb, 1), 0)
    y_ref[0] = _pool_mix(sums, x, pos, w_ref, scale_ref)


def _pool_prompt(p3, st, wbd, scale, tb, pos0):
    nb, t, _ = p3.shape
    kern = functools.partial(_pool_prompt_kernel, tb=tb, pos0=pos0)
    const = lambda a: pl.BlockSpec(a.shape, lambda b, i: (0,) * a.ndim)
    return pl.pallas_call(
        kern,
        grid=(nb, t // tb),
        in_specs=[pl.BlockSpec((1, tb, POOL_WIDTH), lambda b, i: (b, i, COL_POOL // POOL_WIDTH)),
                  pl.BlockSpec((1, 2 * SUBLANES, POOL_WIDTH), lambda b, i: (b, 0, 0)),
                  const(wbd), const(scale)],
        out_specs=pl.BlockSpec((1, tb, POOL_WIDTH), lambda b, i: (b, i, 0)),
        out_shape=jax.ShapeDtypeStruct((nb, t, POOL_WIDTH), F32),
        scratch_shapes=[pltpu.VMEM((2 * SUBLANES, POOL_WIDTH), F32)],
        compiler_params=_cparams(2),
        name="pool_prompt",
    )(p3, st, wbd, scale)


def _pool_sample_kernel(x_ref, w_ref, scale_ref, y_ref, *, group, first, pos0):
    x = x_ref[...]
    rows = x.shape[0]
    sums = _pool_windows(x)
    pos = pos0 + (lax.broadcasted_iota(jnp.int32, (rows, 1), 0) % group) - first
    y_ref[...] = _pool_mix(sums, x, jnp.maximum(pos, 0), w_ref, scale_ref)


def _pool_sample(ext, wbd, scale, group, first, pos0):
    kern = functools.partial(_pool_sample_kernel, group=group, first=first, pos0=pos0)
    const = lambda a: pl.BlockSpec(a.shape, lambda i: (0,) * a.ndim)
    return pl.pallas_call(
        kern,
        grid=(1,),
        in_specs=[const(ext), const(wbd), const(scale)],
        out_specs=const(ext),
        out_shape=jax.ShapeDtypeStruct(ext.shape, F32),
        compiler_params=_cparams(1),
        name="pool_sample",
    )(ext, wbd, scale)


def _block_diag(blocks):
    g, r, c = blocks.shape
    eye = jnp.eye(g, dtype=blocks.dtype)
    return (eye[:, None, :, None] * blocks[:, :, None, :]).reshape(g * r, g * c)


def kernel(x_prompt, x_sample, state_delta, state_conv, state_ssm_re, state_ssm_im, state_pool, norm_mix_pre, norm_mix_post, norm_ffn_pre, norm_ffn_post, w_in, conv_w, dn_a_log, dn_dt_bias, dn_out_norm, ssm_a_re, ssm_a_im, ssm_log_dt, ssm_b_re, ssm_b_im, ssm_c_re, ssm_c_im, ssm_d, ssm_glu_w, ssm_glu_b, pool_w, pool_scale, w_out, ffn_w_gate, ffn_w_up, ffn_w_down):
    depth = w_in.shape[0]
    bp, tp, _ = x_prompt.shape
    bs, ts, _ = x_sample.shape

    w_in_r = jnp.concatenate([w_in[:, :, :QKV_WIDTH], w_in[:, :, _OFF_G:], w_in[:, :, _OFF_A:_OFF_G],
                              jnp.zeros((depth, D_MODEL, LANES - 2 * DN_HEADS), F32)], axis=2).astype(BF16)
    w_out_b = w_out.astype(BF16)
    wg_b, wu_b, wd_b = ffn_w_gate.astype(BF16), ffn_w_up.astype(BF16), ffn_w_down.astype(BF16)
    row = lambda a: a.reshape(depth, 1, -1)
    nmp, nmo, nfp, nfo = row(norm_mix_pre), row(norm_mix_post), row(norm_ffn_pre), row(norm_ffn_post)
    alog = jnp.pad(dn_a_log, ((0, 0), (0, LANES - DN_HEADS))).reshape(depth, 1, LANES)
    dtb = jnp.pad(dn_dt_bias, ((0, 0), (0, LANES - DN_HEADS))).reshape(depth, 1, LANES)
    onorm = row(dn_out_norm)
    dt_full = jnp.repeat(ssm_log_dt, SSM_STATE, axis=1).reshape(depth, 1, SSM_NS)
    b_t = lambda b: jnp.transpose(b, (0, 3, 1, 2)).reshape(depth, SSM_GROUP, SSM_NS)
    lam_re, lam_im, bmat = _ssm_prep(ssm_a_re.reshape(depth, 1, SSM_NS), ssm_a_im.reshape(depth, 1, SSM_NS),
                                     dt_full, b_t(ssm_b_re), b_t(ssm_b_im))
    c_bd = lambda cc: jax.vmap(_block_diag)(jnp.transpose(cc, (0, 1, 3, 2))).astype(BF16)
    cre, cim = c_bd(ssm_c_re), c_bd(ssm_c_im)
    dskip, glu_b = row(ssm_d), row(ssm_glu_b)
    glu_w = ssm_glu_w.astype(BF16)
    pool_bd = jax.vmap(_block_diag)(pool_w).astype(BF16)
    pscale = row(pool_scale)

    xp = x_prompt.reshape(bp * tp, D_MODEL)
    xs = x_sample.reshape(bs * ts, D_MODEL)
    zero_conv = jnp.zeros((bp, SUBLANES, QKV_WIDTH), F32)
    zero_delta = jnp.zeros((bp, DN_HEADS, DN_HEAD_DIM, DN_HEAD_DIM), F32)
    zero_h = jnp.zeros((bp, 2 * SSM_NS), F32)
    zero_pool = jnp.zeros((bp, 2 * SUBLANES, POOL_WIDTH), F32)
    pad_rows = SAMPLE_CHUNK - ts - (DN_CONV - 1)
    pool_group = 24
    pool_first = 1 + POOL_BUF

    outs_p, outs_s = [], []
    for l in range(depth):
        proj = _in_proj(xp, nmp[l], w_in_r[l], 512)
        p3 = proj.reshape(bp, tp, PROJ_WIDTH)
        o_dn, delta_new = _delta_prompt(p3, zero_conv, zero_delta, conv_w[l], alog[l], dtb[l], onorm[l], 256)
        u_tm = jnp.transpose(p3[:, :, COL_SSM:COL_POOL], (1, 0, 2)).reshape(tp * bp, SSM_WIDTH)
        y_tm, h_fin = _ssm(u_tm, zero_h, bmat[l], lam_re[l], lam_im[l], cre[l], cim[l], dskip[l], glu_w[l],
                           glu_b[l], bp, 128, False)
        o_ssm = jnp.transpose(y_tm.reshape(tp, bp, SSM_WIDTH), (1, 0, 2)).reshape(bp * tp, SSM_WIDTH)
        o_pool = _pool_prompt(p3, zero_pool, pool_bd[l], pscale[l], 512, 0)
        xp = _post_mix(o_dn.reshape(bp * tp, DN_WIDTH), o_ssm, o_pool.reshape(bp * tp, POOL_WIDTH), xp,
                       w_out_b[l], nmo[l], nfp[l], wg_b[l], wu_b[l], wd_b[l], nfo[l], 256)
        outs_p.append((delta_new, p3[:, tp - (DN_CONV - 1):, :QKV_WIDTH],
                       h_fin[:, :SSM_NS].reshape(bp, SSM_GROUPS, SSM_STATE),
                       h_fin[:, SSM_NS:].reshape(bp, SSM_GROUPS, SSM_STATE),
                       p3[:, tp - POOL_BUF:, COL_POOL:COL_AB]))

        proj = _in_proj(xs, nmp[l], w_in_r[l], 256)
        s3 = proj.reshape(bs, ts, PROJ_WIDTH)
        head = jnp.concatenate([jnp.zeros((bs, pad_rows, QKV_WIDTH), F32), state_conv[l]], axis=1)
        head = jnp.pad(head, ((0, 0), (0, 0), (0, PROJ_WIDTH - QKV_WIDTH)))
        ext = jnp.concatenate([head, s3], axis=1).reshape(bs * SAMPLE_CHUNK, PROJ_WIDTH)
        o_ext, delta_new = _delta_sample(ext, state_delta[l], conv_w[l], alog[l], dtb[l], onorm[l], 16,
                                         SAMPLE_CHUNK - ts)
        o_dn = o_ext.reshape(bs, SAMPLE_CHUNK, DN_WIDTH)[:, SAMPLE_CHUNK - ts:].reshape(bs * ts, DN_WIDTH)
        u_s = s3[:, :, COL_SSM:COL_POOL]
        u_tm = jnp.transpose(u_s, (1, 0, 2)).reshape(ts * bs, SSM_WIDTH)
        h0 = jnp.concatenate([state_ssm_re[l].reshape(bs, SSM_NS), state_ssm_im[l].reshape(bs, SSM_NS)], axis=1)
        y_tm, h_fin = _ssm(u_tm, h0, bmat[l], lam_re[l], lam_im[l], cre[l], cim[l], dskip[l], glu_w[l],
                           glu_b[l], bs, ts, True)
        o_ssm = jnp.transpose(y_tm.reshape(ts, bs, SSM_WIDTH), (1, 0, 2)).reshape(bs * ts, SSM_WIDTH)
        pool_u = s3[:, :, COL_POOL:COL_AB]
        pext = jnp.concatenate([jnp.zeros((bs, 1, POOL_WIDTH), F32), state_pool[l], pool_u,
                                jnp.zeros((bs, pool_group - pool_first - ts, POOL_WIDTH), F32)], axis=1)
        y_ext = _pool_sample(pext.reshape(bs * pool_group, POOL_WIDTH), pool_bd[l], pscale[l], pool_group,
                             pool_first, PAST_LEN)
        o_pool = y_ext.reshape(bs, pool_group, POOL_WIDTH)[:, pool_first:pool_first + ts].reshape(bs * ts, POOL_WIDTH)
        xs = _post_mix(o_dn, o_ssm, o_pool, xs, w_out_b[l], nmo[l], nfp[l], wg_b[l], wu_b[l], wd_b[l], nfo[l], 256)
        outs_s.append((delta_new, s3[:, ts - (DN_CONV - 1):, :QKV_WIDTH],
                       h_fin[:, :SSM_NS].reshape(bs, SSM_GROUPS, SSM_STATE),
                       h_fin[:, SSM_NS:].reshape(bs, SSM_GROUPS, SSM_STATE),
                       jnp.concatenate([state_pool[l][:, ts:], pool_u], axis=1)))

    stack = lambda outs, k: jnp.stack([o[k] for o in outs])
    return (xp.reshape(bp, tp, D_MODEL), xs.reshape(bs, ts, D_MODEL),
            stack(outs_p, 0), stack(outs_p, 1), stack(outs_p, 2), stack(outs_p, 3), stack(outs_p, 4),
            stack(outs_s, 0), stack(outs_s, 1), stack(outs_s, 2), stack(outs_s, 3), stack(outs_s, 4))
```

```python
import functools
import math

import jax
import jax.numpy as jnp
from jax import lax
from jax.experimental import pallas as pl
from jax.experimental.pallas import tpu as pltpu

F32 = jnp.float32
BF16 = jnp.bfloat16
HIGHEST = lax.Precision.HIGHEST

D_MODEL = 1024
DN_HEADS = 4
DN_HEAD_DIM = 128
DN_WIDTH = DN_HEADS * DN_HEAD_DIM
DN_CONV = 4
QKV_WIDTH = 3 * DN_WIDTH
SSM_WIDTH = 256
SSM_GROUP = 16
SSM_GROUPS = 16
SSM_STATE = 64
SSM_NS = SSM_GROUPS * SSM_STATE
POOL_WIDTH = 256
POOL_WINDOWS = (2, 4, 8, 16)
POOL_GROUP = 64
POOL_BUF = 15
D_FF = 2816
EPS = 1e-6
PAST_LEN = 16384

COL_GATE = QKV_WIDTH
COL_POOL = COL_GATE + DN_WIDTH
COL_AB = COL_POOL + POOL_WIDTH
LANES = 128
SUBLANES = 8
PROJ_WIDTH = COL_AB + LANES

_OFF_A = QKV_WIDTH
_OFF_G = _OFF_A + 2 * DN_HEADS
_REST_WIDTH = DN_WIDTH + SSM_WIDTH + POOL_WIDTH

VMEM_LIMIT_BYTES = 56 * 1024 * 1024

DELTA_CHUNK = 64
SAMPLE_CHUNK = 8
DELTA_SUB = 16


def _cparams(n_axes):
    return pltpu.CompilerParams(dimension_semantics=("arbitrary",) * n_axes,
                                vmem_limit_bytes=VMEM_LIMIT_BYTES)


def _bdot(a, b):
    return jnp.dot(a.astype(BF16), b.astype(BF16), preferred_element_type=F32)


def _bdot_nt(a, b):
    return lax.dot_general(a.astype(BF16), b.astype(BF16), (((1,), (1,)), ((), ())),
                           preferred_element_type=F32)


def _bdot_tn(a, b):
    return lax.dot_general(a.astype(BF16), b.astype(BF16), (((0,), (0,)), ((), ())),
                           preferred_element_type=F32)


def _hdot(a, b):
    return jnp.dot(a, b, precision=HIGHEST, preferred_element_type=F32)


def _rms(x, w):
    return x * lax.rsqrt(jnp.mean(x * x, axis=-1, keepdims=True) + EPS) * w


def _silu(x):
    return x * jax.nn.sigmoid(x)


def _in_proj_kernel(x_ref, nw_ref, wqkv_ref, wrest_ref, wab_ref, o_ref, ssm_ref):
    h = _rms(x_ref[...], nw_ref[...]).astype(BF16)
    o_ref[:, :QKV_WIDTH] = _mm(h, wqkv_ref[...])
    rest = _mm(h, wrest_ref[...])
    o_ref[:, COL_GATE:COL_POOL] = rest[:, :DN_WIDTH]
    ssm_ref[...] = rest[:, DN_WIDTH:DN_WIDTH + SSM_WIDTH]
    o_ref[:, COL_POOL:COL_AB] = rest[:, DN_WIDTH + SSM_WIDTH:]
    o_ref[:, COL_AB:] = _mm(h, wab_ref[...])


def _in_proj(x, nw, wqkv, wrest, wab, tm, seq_len):
    m = x.shape[0]
    if seq_len is None:
        ssm_shape, ssm_map = (m, SSM_WIDTH), lambda i: (i, 0)
    else:
        nt = seq_len // tm
        ssm_shape, ssm_map = (seq_len, (m // seq_len) * SSM_WIDTH), lambda i: (i % nt, i // nt)
    const = lambda a: pl.BlockSpec(a.shape, lambda i: (0, 0), pipeline_mode=pl.Buffered(1))
    return pl.pallas_call(
        _in_proj_kernel,
        grid=(m // tm,),
        in_specs=[pl.BlockSpec((tm, D_MODEL), lambda i: (i, 0)),
                  pl.BlockSpec((1, D_MODEL), lambda i: (0, 0)),
                  const(wqkv), const(wrest), const(wab)],
        out_specs=[pl.BlockSpec((tm, PROJ_WIDTH), lambda i: (i, 0)),
                   pl.BlockSpec((tm, SSM_WIDTH), ssm_map)],
        out_shape=[jax.ShapeDtypeStruct((m, PROJ_WIDTH), F32), jax.ShapeDtypeStruct(ssm_shape, F32)],
        compiler_params=_cparams(1),
        name="in_proj",
    )(x, nw, wqkv, wrest, wab)


def _post_mix_kernel(odn_ref, ossm_ref, opool_ref, x_ref, wo_ref, nmp_ref, nfp_ref, wg_ref, wu_ref, wd_ref,
                     nfo_ref, o_ref):
    mix = (_bdot(odn_ref[...], wo_ref[0:DN_WIDTH, :])
           + _bdot(ossm_ref[...], wo_ref[DN_WIDTH:DN_WIDTH + SSM_WIDTH, :])
           + _bdot(opool_ref[...], wo_ref[DN_WIDTH + SSM_WIDTH:, :]))
    x1 = x_ref[...] + _rms(mix, nmp_ref[...])
    h = _rms(x1, nfp_ref[...]).astype(BF16)
    g = jnp.dot(h, wg_ref[...], preferred_element_type=F32)
    u = jnp.dot(h, wu_ref[...], preferred_element_type=F32)
    f = _bdot(_silu(g) * u, wd_ref[...])
    o_ref[...] = x1 + _rms(f, nfo_ref[...])


def _post_mix(odn, ossm, opool, x, wo, nmp, nfp, wg, wu, wd, nfo, tm, seq_len):
    m = x.shape[0]
    row = lambda w: pl.BlockSpec((tm, w), lambda i: (i, 0))
    full = lambda a: pl.BlockSpec(a.shape, lambda i: (0,) * a.ndim, pipeline_mode=pl.Buffered(1))
    if seq_len is None:
        ssm_spec = row(SSM_WIDTH)
    else:
        nt = seq_len // tm
        ssm_spec = pl.BlockSpec((tm, SSM_WIDTH), lambda i: (i % nt, i // nt))
    return pl.pallas_call(
        _post_mix_kernel,
        grid=(m // tm,),
        in_specs=[row(DN_WIDTH), ssm_spec, row(POOL_WIDTH), row(D_MODEL),
                  full(wo), full(nmp), full(nfp), full(wg), full(wu), full(wd), full(nfo)],
        out_specs=row(D_MODEL),
        out_shape=jax.ShapeDtypeStruct((m, D_MODEL), F32),
        compiler_params=_cparams(1),
        name="post_mix",
    )(odn, ossm, opool, x, wo, nmp, nfp, wg, wu, wd, nfo)


def _mm(a, b):
    return jnp.dot(a, b, preferred_element_type=F32)


def _split2(x):
    hi = x.astype(BF16)
    return hi, (x - hi.astype(F32)).astype(BF16)


def _group_cumsum(x, group, reverse=False):
    rows = x.shape[0]
    pos = lax.broadcasted_iota(jnp.int32, x.shape, 0) % group
    d = 1
    while d < group:
        if reverse:
            x = x + jnp.where(pos + d < group, pltpu.roll(x, rows - d, 0), 0.0)
        else:
            x = x + jnp.where(pos >= d, pltpu.roll(x, d, 0), 0.0)
        d *= 2
    return x


def _chunk_masks(rows, blk):
    r = lax.broadcasted_iota(jnp.int32, (rows, rows), 0)
    s = lax.broadcasted_iota(jnp.int32, (rows, rows), 1)
    d = r - s
    if blk < rows:
        d = jnp.where(r // blk == s // blk, d, -1)
    sub = None
    if blk > DELTA_SUB:
        sub = r // DELTA_SUB == s // DELTA_SUB
    return d >= 0, d > 0, (r == s).astype(F32), sub


def _neumann(a, eye, index):
    t = [eye - x for x in a]
    p = a
    n = 2
    while n < index:
        p = [_bdot(x, x) for x in p]
        t = [x + _bdot(x, y) for x, y in zip(t, p)]
        n *= 2
    return t


def _unit_lower_inverse(a, eye, sub, index):
    if sub is None:
        return _neumann(a, eye, index)
    assert index <= 4 * DELTA_SUB
    d = [jnp.where(sub, x, 0.0) for x in a]
    td = _neumann(d, eye, DELTA_SUB)
    n = [_bdot(t, x - y) for t, x, y in zip(td, a, d)]
    n2 = [_bdot(x, x) for x in n]
    m = [_bdot(eye - x, eye + y) for x, y in zip(n, n2)]
    return [_bdot(x, t) for x, t in zip(m, td)]


def _delta_local(probs, masks, index):
    incl, strict, eye, sub = masks
    rows = probs[0][0].shape[0]
    kb = [k * beta for (_, k, _, beta, _, _, _) in probs]
    kq = [_bdot_nt(jnp.concatenate([x, q], axis=0), k) for x, (q, k, _, _, _, _, _) in zip(kb, probs)]
    decay = [jnp.where(incl, jnp.exp(jnp.where(incl, gam - grow, 0.0)), 0.0) for (_, _, _, _, gam, _, grow) in probs]
    a = [jnp.where(strict, x[:rows] * d, 0.0) for x, d in zip(kq, decay)]
    qk = [x[rows:] * d for x, d in zip(kq, decay)]
    egam = [jnp.exp(p[4]) for p in probs]
    rhs = [jnp.concatenate([p[2] * p[3], x * e], axis=1) for p, x, e in zip(probs, kb, egam)]
    t = _unit_lower_inverse(a, eye, sub, index)
    sol = [_bdot(x, r) for x, r in zip(t, rhs)]
    a_sp = [_split2(x) for x in a]
    s_sp = [_split2(x) for x in sol]
    asol = [_mm(ah, sh) + _mm(ah, sl) + _mm(al, sh) for (ah, al), (sh, sl) in zip(a_sp, s_sp)]
    resid = [r - s - x for r, s, x in zip(rhs, sol, asol)]
    sol = [s + _bdot(x, r) for s, x, r in zip(sol, t, resid)]
    return [(s[:, :DN_HEAD_DIM], s[:, DN_HEAD_DIM:], x, p[0] * e, p[1] * jnp.exp(p[5]))
            for s, x, p, e in zip(sol, qk, probs, egam)]


def _delta_front(y, ab, alog, dtb):
    y = _silu(y)
    qs, ks = [], []
    for h in range(DN_HEADS):
        qh = y[:, h * DN_HEAD_DIM:(h + 1) * DN_HEAD_DIM]
        kh = y[:, DN_WIDTH + h * DN_HEAD_DIM:DN_WIDTH + (h + 1) * DN_HEAD_DIM]
        qs.append(qh * lax.rsqrt(jnp.sum(qh * qh, axis=-1, keepdims=True) + EPS) * (DN_HEAD_DIM ** -0.5))
        ks.append(kh * lax.rsqrt(jnp.sum(kh * kh, axis=-1, keepdims=True) + EPS))
    v = y[:, 2 * DN_WIDTH:]
    z = ab + dtb
    softplus = jnp.maximum(z, 0.0) + jnp.log1p(jnp.exp(-jnp.abs(z)))
    logg = -jnp.exp(alog) * softplus
    beta = jax.nn.sigmoid(ab)
    return qs, ks, v, logg, beta


def _delta_out(o, gate, onorm):
    return o * lax.rsqrt(jnp.mean(o * o, axis=-1, keepdims=True) + EPS) * onorm * _silu(gate)


def _conv4(xfull, w, rows, off):
    z = xfull * w[0:1]
    z = xfull * w[1:2] + pltpu.roll(z, 1, 0)
    z = xfull * w[2:3] + pltpu.roll(z, 1, 0)
    return (xfull * w[3:4] + pltpu.roll(z, 1, 0))[off:off + rows]


def _delta_prompt_kernel(qkv_ref, ab_ref, gate_ref, cst_ref, s0_ref, cw_ref, alog_ref, dtb_ref, onorm_ref,
                         o_ref, sfin_ref, tail, s_scr, *, nseq, tb, c):
    i = pl.program_id(1)

    @pl.when(i == 0)
    def _():
        s_scr[...] = s0_ref[...]
        tail[...] = cst_ref[...]

    masks = _chunk_masks(c, c)
    nchunk = tb // c
    heads = range(DN_HEADS)
    seqs = range(nseq)
    hsl = [slice(h * DN_HEAD_DIM, (h + 1) * DN_HEAD_DIM) for h in heads]
    probs, gams = [], []
    for r in seqs:
        x = qkv_ref[r]
        xfull = jnp.concatenate([tail[r], x], axis=0)
        y = _conv4(xfull, cw_ref[...], tb, SUBLANES)
        tail[r] = x[tb - SUBLANES:, :]
        qs, ks, v, logg, beta = _delta_front(y, ab_ref[r], alog_ref[...], dtb_ref[...])
        gam = _group_cumsum(logg, c)
        gexc = _group_cumsum(logg, c, reverse=True) - logg
        gam_t = gam.T
        gams.append(gam)
        for j in range(nchunk):
            rs = slice(j * c, (j + 1) * c)
            for h in heads:
                probs.append((qs[h][rs], ks[h][rs], v[rs, hsl[h]], beta[rs, DN_HEADS + h:DN_HEADS + h + 1],
                              gam[rs, h:h + 1], gexc[rs, h:h + 1], gam_t[h:h + 1, rs]))
    local = _delta_local(probs, masks, c)
    chains = [(r, h) for r in seqs for h in heads]
    s = {ch: s_scr[ch[0], ch[1]] for ch in chains}
    outs = {ch: [] for ch in chains}
    for j in range(nchunk):
        loc = {(r, h): local[(r * nchunk + j) * DN_HEADS + h] for r, h in chains}
        ks_ = {ch: _bdot(jnp.concatenate([loc[ch][1], loc[ch][3]], axis=0), s[ch]) for ch in chains}
        u = {ch: loc[ch][0] - ks_[ch][:c] for ch in chains}
        for ch in chains:
            outs[ch].append(ks_[ch][c:] + _bdot(loc[ch][2], u[ch]))
        last = (j + 1) * c - 1
        s = {(r, h): s[r, h] * jnp.exp(gams[r][last:last + 1, h:h + 1]) + _bdot_tn(loc[r, h][4], u[r, h])
             for r, h in chains}
    for r, h in chains:
        s_scr[r, h] = s[r, h]
        o = _delta_out(jnp.concatenate(outs[r, h], axis=0), gate_ref[r, :, hsl[h]], onorm_ref[...])
        o_ref[r, :, hsl[h]] = o.astype(o_ref.dtype)
    sfin_ref[...] = s_scr[...]


def _delta_prompt(p3, cst, s0, cw, alog, dtb, onorm, nseq, tb):
    nb, t, _ = p3.shape
    kern = functools.partial(_delta_prompt_kernel, nseq=nseq, tb=tb, c=DELTA_CHUNK)
    const = lambda a: pl.BlockSpec(a.shape, lambda b, i: (0,) * a.ndim)
    sspec = pl.BlockSpec((nseq, DN_HEADS, DN_HEAD_DIM, DN_HEAD_DIM), lambda b, i: (b, 0, 0, 0))
    return pl.pallas_call(
        kern,
        grid=(nb // nseq, t // tb),
        in_specs=[pl.BlockSpec((nseq, tb, QKV_WIDTH), lambda b, i: (b, i, 0)),
                  pl.BlockSpec((nseq, tb, LANES), lambda b, i: (b, i, COL_AB // LANES)),
                  pl.BlockSpec((nseq, tb, DN_WIDTH), lambda b, i: (b, i, COL_GATE // DN_WIDTH)),
                  pl.BlockSpec((nseq, SUBLANES, QKV_WIDTH), lambda b, i: (b, 0, 0)),
                  sspec, const(cw), const(alog), const(dtb), const(onorm)],
        out_specs=[pl.BlockSpec((nseq, tb, DN_WIDTH), lambda b, i: (b, i, 0)), sspec],
        out_shape=[jax.ShapeDtypeStruct((nb, t, DN_WIDTH), BF16),
                   jax.ShapeDtypeStruct((nb, DN_HEADS, DN_HEAD_DIM, DN_HEAD_DIM), F32)],
        scratch_shapes=[pltpu.VMEM((nseq, SUBLANES, QKV_WIDTH), F32),
                        pltpu.VMEM((nseq, DN_HEADS, DN_HEAD_DIM, DN_HEAD_DIM), F32)],
        compiler_params=_cparams(2),
        name="delta_prompt",
    )(p3, p3, p3, cst, s0, cw, alog, dtb, onorm)


def _delta_sample_kernel(qkv_ref, ab_ref, gate_ref, s0_ref, cw_ref, alog_ref, dtb_ref, onorm_ref,
                         o_ref, snew_ref, *, nseq, c, first):
    rows = nseq * c
    x = qkv_ref[...]
    y = _conv4(x, cw_ref[...], rows, 0)
    qs, ks, v, logg, beta = _delta_front(y, ab_ref[...], alog_ref[...], dtb_ref[...])
    valid = (lax.broadcasted_iota(jnp.int32, (rows, LANES), 0) % c) >= first
    logg = jnp.where(valid, logg, 0.0)
    beta = jnp.where(valid, beta, 0.0)
    gam = _group_cumsum(logg, c)
    gexc = _group_cumsum(logg, c, reverse=True) - logg
    gam_t = gam.T
    masks = _chunk_masks(rows, c)
    gate = gate_ref[...]
    heads = range(DN_HEADS)
    hsl = [slice(h * DN_HEAD_DIM, (h + 1) * DN_HEAD_DIM) for h in heads]
    local = _delta_local([(qs[h], ks[h], v[:, hsl[h]], beta[:, DN_HEADS + h:DN_HEADS + h + 1], gam[:, h:h + 1],
                           gexc[:, h:h + 1], gam_t[h:h + 1, :]) for h in heads], masks, c)
    pairs = [(b, h) for b in range(nseq) for h in heads]
    rsl = [slice(b * c, (b + 1) * c) for b in range(nseq)]
    ks_ = {(b, h): _bdot(jnp.concatenate([local[h][1][rsl[b]], local[h][3][rsl[b]]], axis=0), s0_ref[b, h])
           for b, h in pairs}
    u = {(b, h): local[h][0][rsl[b]] - ks_[b, h][:c] for b, h in pairs}
    for b, h in pairs:
        last = (b + 1) * c - 1
        snew_ref[b, h] = (s0_ref[b, h] * jnp.exp(gam[last:last + 1, h:h + 1])
                          + _bdot_tn(local[h][4][rsl[b]], u[b, h]))
    for h in heads:
        o = (jnp.concatenate([ks_[b, h][c:] for b in range(nseq)], axis=0)
             + _bdot(local[h][2], jnp.concatenate([u[b, h] for b in range(nseq)], axis=0)))
        o_ref[:, hsl[h]] = _delta_out(o, gate[:, hsl[h]], onorm_ref[...]).astype(o_ref.dtype)


def _delta_sample(ext, s0, cw, alog, dtb, onorm, nseq, first):
    c = SAMPLE_CHUNK
    nb = s0.shape[0]
    rows = nseq * c
    kern = functools.partial(_delta_sample_kernel, nseq=nseq, c=c, first=first)
    const = lambda a: pl.BlockSpec(a.shape, lambda i: (0,) * a.ndim)
    sspec = pl.BlockSpec((nseq, DN_HEADS, DN_HEAD_DIM, DN_HEAD_DIM), lambda i: (i, 0, 0, 0))
    return pl.pallas_call(
        kern,
        grid=(nb // nseq,),
        in_specs=[pl.BlockSpec((rows, QKV_WIDTH), lambda i: (i, 0)),
                  pl.BlockSpec((rows, LANES), lambda i: (i, COL_AB // LANES)),
                  pl.BlockSpec((rows, DN_WIDTH), lambda i: (i, COL_GATE // DN_WIDTH)),
                  sspec, const(cw), const(alog), const(dtb), const(onorm)],
        out_specs=[pl.BlockSpec((rows, DN_WIDTH), lambda i: (i, 0)), sspec],
        out_shape=[jax.ShapeDtypeStruct((nb * c, DN_WIDTH), BF16),
                   jax.ShapeDtypeStruct(s0.shape, F32)],
        compiler_params=_cparams(1),
        name="delta_sample",
    )(ext, ext, ext, s0, cw, alog, dtb, onorm)


def _ssm_prep_kernel(are_ref, aim_ref, dt_ref, bre_ref, bim_ref, lre_ref, lim_ref, bmat_ref):
    ar, ai, dt = are_ref[0], aim_ref[0], jnp.exp(dt_ref[0])
    mag = jnp.exp(ar * dt)
    lr = mag * jnp.cos(ai * dt)
    li = mag * jnp.sin(ai * dt)
    lre_ref[0] = lr
    lim_ref[0] = li
    den = ar * ar + ai * ai
    fr = ((lr - 1.0) * ar + li * ai) / den
    fi = (li * ar - (lr - 1.0) * ai) / den
    br, bi = bre_ref[0], bim_ref[0]
    bbr = fr * br - fi * bi
    bbi = fr * bi + fi * br
    lane_group = lax.broadcasted_iota(jnp.int32, (SSM_GROUP, SSM_NS), 1) // SSM_STATE
    for g in range(SSM_GROUPS):
        m = lane_group == g
        bmat_ref[0, g * SSM_GROUP:(g + 1) * SSM_GROUP, 0:SSM_NS] = jnp.where(m, bbr, 0.0)
        bmat_ref[0, g * SSM_GROUP:(g + 1) * SSM_GROUP, SSM_NS:] = jnp.where(m, bbi, 0.0)


def _ssm_prep(are, aim, dt, bre, bim):
    depth = are.shape[0]
    vec = pl.BlockSpec((1, 1, SSM_NS), lambda l: (l, 0, 0))
    mat = pl.BlockSpec((1, SSM_GROUP, SSM_NS), lambda l: (l, 0, 0))
    return pl.pallas_call(
        _ssm_prep_kernel,
        grid=(depth,),
        in_specs=[vec, vec, vec, mat, mat],
        out_specs=[vec, vec, pl.BlockSpec((1, SSM_WIDTH, 2 * SSM_NS), lambda l: (l, 0, 0))],
        out_shape=[jax.ShapeDtypeStruct((depth, 1, SSM_NS), F32), jax.ShapeDtypeStruct((depth, 1, SSM_NS), F32),
                   jax.ShapeDtypeStruct((depth, SSM_WIDTH, 2 * SSM_NS), F32)],
        compiler_params=_cparams(1),
        name="ssm_prep",
    )(are, aim, dt, bre, bim)


def _gelu_tanh(x):
    return 0.5 * x * (1.0 + jnp.tanh(math.sqrt(2.0 / math.pi) * (x + 0.044715 * (x * x * x))))


def _ssm_kernel(u_ref, h0_ref, bmat_ref, lre_ref, lim_ref, cre_ref, cim_ref, d_ref, gw_ref, gb_ref,
                y_ref, hl_ref, buf, h_scr, *, nb, nt, exact_in):
    i = pl.program_id(0)

    @pl.when(i == 0)
    def _():
        h_scr[...] = h0_ref[...]

    u = u_ref[...]
    buf[...] = _hdot(u, bmat_ref[...]) if exact_in else _bdot(u, bmat_ref[...])
    lr = jnp.broadcast_to(lre_ref[...], (SUBLANES, SSM_NS))
    li = jnp.broadcast_to(lim_ref[...], (SUBLANES, SSM_NS))
    for g in range(nb // SUBLANES):
        def step(t, h, g=g):
            r0 = pl.multiple_of(t * nb + g * SUBLANES, SUBLANES)
            bu = buf[pl.ds(r0, SUBLANES), :]
            hre, him = h[:, :SSM_NS], h[:, SSM_NS:]
            h = jnp.concatenate([lr * hre - li * him + bu[:, :SSM_NS],
                                 lr * him + li * hre + bu[:, SSM_NS:]], axis=1)
            buf[pl.ds(r0, SUBLANES), :] = h
            return h

        gs = slice(g * SUBLANES, (g + 1) * SUBLANES)
        h_fin = lax.fori_loop(0, nt, step, h_scr[gs, :])
        h_scr[gs, :] = h_fin
    hl_ref[...] = h_scr[...]
    y = _bdot(buf[:, :SSM_NS], cre_ref[...]) - _bdot(buf[:, SSM_NS:], cim_ref[...]) + d_ref[...] * u
    y = _gelu_tanh(y)
    y_ref[...] = (y * jax.nn.sigmoid(_bdot(y, gw_ref[...]) + gb_ref[...])).astype(y_ref.dtype)


def _ssm(u_tm, h0, bmat, lre, lim, cre, cim, dskip, gw, gb, nb, nt, exact_in):
    rows = nt * nb
    m = u_tm.shape[0]
    kern = functools.partial(_ssm_kernel, nb=nb, nt=nt, exact_in=exact_in)
    const = lambda a: pl.BlockSpec(a.shape, lambda i: (0,) * a.ndim)
    return pl.pallas_call(
        kern,
        grid=(m // rows,),
        in_specs=[pl.BlockSpec((rows, SSM_WIDTH), lambda i: (i, 0)), const(h0), const(bmat), const(lre),
                  const(lim), const(cre), const(cim), const(dskip), const(gw), const(gb)],
        out_specs=[pl.BlockSpec((rows, SSM_WIDTH), lambda i: (i, 0)), const(h0)],
        out_shape=[jax.ShapeDtypeStruct((m, SSM_WIDTH), BF16), jax.ShapeDtypeStruct(h0.shape, F32)],
        scratch_shapes=[pltpu.VMEM((rows, 2 * SSM_NS), F32), pltpu.VMEM(h0.shape, F32)],
        compiler_params=_cparams(1),
        name="ssm",
    )(u_tm, h0, bmat, lre, lim, cre, cim, dskip, gw, gb)


def _pool_windows(xfull):
    s2 = xfull + pltpu.roll(xfull, 1, 0)
    s4 = s2 + pltpu.roll(s2, 2, 0)
    s8 = s4 + pltpu.roll(s4, 4, 0)
    s16 = s8 + pltpu.roll(s8, 8, 0)
    return s2, s4, s8, s16


def _pool_mix(sums, x, pos, w_ref, scale_ref):
    lane = lax.broadcasted_iota(jnp.int32, (1, POOL_WIDTH), 1) // POOL_GROUP
    win = None
    for gidx in reversed(range(len(POOL_WINDOWS))):
        cnt = jnp.minimum(pos + 1, POOL_WINDOWS[gidx]).astype(F32)
        term = sums[gidx] / cnt
        win = term if win is None else jnp.where(lane == gidx, term, win)
    r = win - x
    return _bdot(r, w_ref[...]) * scale_ref[...]


def _pool_prompt_kernel(u_ref, st_ref, w_ref, scale_ref, y_ref, tail, *, tb, pos0):
    i = pl.program_id(1)
    halo = 2 * SUBLANES

    @pl.when(i == 0)
    def _():
        tail[...] = st_ref[0]

    x = u_ref[0]
    xfull = jnp.concatenate([tail[...], x], axis=0)
    tail[...] = x[tb - halo:, :]
    sums = [s[halo:] for s in _pool_windows(xfull)]
    pos = pos0 + i * tb + lax.broadcasted_iota(jnp.int32, (tb, 1), 0)
    y_ref[0] = _pool_mix(sums, x, pos, w_ref, scale_ref).astype(y_ref.dtype)


def _pool_prompt(p3, st, wbd, scale, tb, pos0):
    nb, t, _ = p3.shape
    kern = functools.partial(_pool_prompt_kernel, tb=tb, pos0=pos0)
    const = lambda a: pl.BlockSpec(a.shape, lambda b, i: (0,) * a.ndim)
    return pl.pallas_call(
        kern,
        grid=(nb, t // tb),
        in_specs=[pl.BlockSpec((1, tb, POOL_WIDTH), lambda b, i: (b, i, COL_POOL // POOL_WIDTH)),
                  pl.BlockSpec((1, 2 * SUBLANES, POOL_WIDTH), lambda b, i: (b, 0, 0)),
                  const(wbd), const(scale)],
        out_specs=pl.BlockSpec((1, tb, POOL_WIDTH), lambda b, i: (b, i, 0)),
        out_shape=jax.ShapeDtypeStruct((nb, t, POOL_WIDTH), BF16),
        scratch_shapes=[pltpu.VMEM((2 * SUBLANES, POOL_WIDTH), F32)],
        compiler_params=_cparams(2),
        name="pool_prompt",
    )(p3, st, wbd, scale)


def _pool_sample_kernel(x_ref, w_ref, scale_ref, y_ref, *, group, first, pos0):
    x = x_ref[...]
    rows = x.shape[0]
    sums = _pool_windows(x)
    pos = pos0 + (lax.broadcasted_iota(jnp.int32, (rows, 1), 0) % group) - first
    y_ref[...] = _pool_mix(sums, x, jnp.maximum(pos, 0), w_ref, scale_ref).astype(y_ref.dtype)


def _pool_sample(ext, wbd, scale, group, first, pos0):
    kern = functools.partial(_pool_sample_kernel, group=group, first=first, pos0=pos0)
    const = lambda a: pl.BlockSpec(a.shape, lambda i: (0,) * a.ndim)
    return pl.pallas_call(
        kern,
        grid=(1,),
        in_specs=[const(ext), const(wbd), const(scale)],
        out_specs=const(ext),
        out_shape=jax.ShapeDtypeStruct(ext.shape, BF16),
        compiler_params=_cparams(1),
        name="pool_sample",
    )(ext, wbd, scale)


def _block_diag(blocks):
    g, r, c = blocks.shape
    eye = jnp.eye(g, dtype=blocks.dtype)
    return (eye[:, None, :, None] * blocks[:, :, None, :]).reshape(g * r, g * c)


def kernel(x_prompt, x_sample, state_delta, state_conv, state_ssm_re, state_ssm_im, state_pool, norm_mix_pre, norm_mix_post, norm_ffn_pre, norm_ffn_post, w_in, conv_w, dn_a_log, dn_dt_bias, dn_out_norm, ssm_a_re, ssm_a_im, ssm_log_dt, ssm_b_re, ssm_b_im, ssm_c_re, ssm_c_im, ssm_d, ssm_glu_w, ssm_glu_b, pool_w, pool_scale, w_out, ffn_w_gate, ffn_w_up, ffn_w_down):
    depth = w_in.shape[0]
    bp, tp, _ = x_prompt.shape
    bs, ts, _ = x_sample.shape

    w_qkv = w_in[:, :, :QKV_WIDTH].astype(BF16)
    w_rest = w_in[:, :, _OFF_G:].astype(BF16)
    w_ab = jnp.pad(w_in[:, :, _OFF_A:_OFF_G], ((0, 0), (0, 0), (0, LANES - 2 * DN_HEADS))).astype(BF16)
    w_out_b = w_out.astype(BF16)
    wg_b, wu_b, wd_b = ffn_w_gate.astype(BF16), ffn_w_up.astype(BF16), ffn_w_down.astype(BF16)
    row = lambda a: a.reshape(depth, 1, -1)
    nmp, nmo, nfp, nfo = row(norm_mix_pre), row(norm_mix_post), row(norm_ffn_pre), row(norm_ffn_post)
    alog = jnp.pad(dn_a_log, ((0, 0), (0, LANES - DN_HEADS))).reshape(depth, 1, LANES)
    dtb = jnp.pad(dn_dt_bias, ((0, 0), (0, LANES - DN_HEADS))).reshape(depth, 1, LANES)
    onorm = row(dn_out_norm)
    dt_full = jnp.repeat(ssm_log_dt, SSM_STATE, axis=1).reshape(depth, 1, SSM_NS)
    b_t = lambda b: jnp.transpose(b, (0, 3, 1, 2)).reshape(depth, SSM_GROUP, SSM_NS)
    lam_re, lam_im, bmat = _ssm_prep(ssm_a_re.reshape(depth, 1, SSM_NS), ssm_a_im.reshape(depth, 1, SSM_NS),
                                     dt_full, b_t(ssm_b_re), b_t(ssm_b_im))
    c_bd = lambda cc: jax.vmap(_block_diag)(jnp.transpose(cc, (0, 1, 3, 2))).astype(BF16)
    cre, cim = c_bd(ssm_c_re), c_bd(ssm_c_im)
    dskip, glu_b = row(ssm_d), row(ssm_glu_b)
    glu_w = ssm_glu_w.astype(BF16)
    pool_bd = jax.vmap(_block_diag)(pool_w).astype(BF16)
    pscale = row(pool_scale)

    xp = x_prompt.reshape(bp * tp, D_MODEL)
    xs = x_sample.reshape(bs * ts, D_MODEL)
    zero_conv = jnp.zeros((bp, SUBLANES, QKV_WIDTH), F32)
    zero_delta = jnp.zeros((bp, DN_HEADS, DN_HEAD_DIM, DN_HEAD_DIM), F32)
    zero_h = jnp.zeros((bp, 2 * SSM_NS), F32)
    zero_pool = jnp.zeros((bp, 2 * SUBLANES, POOL_WIDTH), F32)
    pad_rows = SAMPLE_CHUNK - ts - (DN_CONV - 1)
    pool_group = 24
    pool_first = 1 + POOL_BUF

    outs_p, outs_s = [], []
    for l in range(depth):
        proj, u_tm = _in_proj(xp, nmp[l], w_qkv[l], w_rest[l], w_ab[l], 512, tp)
        p3 = proj.reshape(bp, tp, PROJ_WIDTH)
        o_dn, delta_new = _delta_prompt(p3, zero_conv, zero_delta, conv_w[l], alog[l], dtb[l], onorm[l], 2, 256)
        y_tm, h_fin = _ssm(u_tm.reshape(tp * bp, SSM_WIDTH), zero_h, bmat[l], lam_re[l], lam_im[l], cre[l], cim[l],
                           dskip[l], glu_w[l], glu_b[l], bp, 128, False)
        o_pool = _pool_prompt(p3, zero_pool, pool_bd[l], pscale[l], 512, 0)
        xp = _post_mix(o_dn.reshape(bp * tp, DN_WIDTH), y_tm.reshape(tp, bp * SSM_WIDTH),
                       o_pool.reshape(bp * tp, POOL_WIDTH), xp,
                       w_out_b[l], nmo[l], nfp[l], wg_b[l], wu_b[l], wd_b[l], nfo[l], 256, tp)
        outs_p.append((delta_new, p3[:, tp - (DN_CONV - 1):, :QKV_WIDTH],
                       h_fin[:, :SSM_NS].reshape(bp, SSM_GROUPS, SSM_STATE),
                       h_fin[:, SSM_NS:].reshape(bp, SSM_GROUPS, SSM_STATE),
                       p3[:, tp - POOL_BUF:, COL_POOL:COL_AB]))

        proj, u_s = _in_proj(xs, nmp[l], w_qkv[l], w_rest[l], w_ab[l], 256, None)
        s3 = proj.reshape(bs, ts, PROJ_WIDTH)
        head = jnp.concatenate([jnp.zeros((bs, pad_rows, QKV_WIDTH), F32), state_conv[l]], axis=1)
        head = jnp.pad(head, ((0, 0), (0, 0), (0, PROJ_WIDTH - QKV_WIDTH)))
        ext = jnp.concatenate([head, s3], axis=1).reshape(bs * SAMPLE_CHUNK, PROJ_WIDTH)
        o_ext, delta_new = _delta_sample(ext, state_delta[l], conv_w[l], alog[l], dtb[l], onorm[l], 16,
                                         SAMPLE_CHUNK - ts)
        o_dn = o_ext.reshape(bs, SAMPLE_CHUNK, DN_WIDTH)[:, SAMPLE_CHUNK - ts:].reshape(bs * ts, DN_WIDTH)
        u_tm = jnp.transpose(u_s.reshape(bs, ts, SSM_WIDTH), (1, 0, 2)).reshape(ts * bs, SSM_WIDTH)
        h0 = jnp.concatenate([state_ssm_re[l].reshape(bs, SSM_NS), state_ssm_im[l].reshape(bs, SSM_NS)], axis=1)
        y_tm, h_fin = _ssm(u_tm, h0, bmat[l], lam_re[l], lam_im[l], cre[l], cim[l], dskip[l], glu_w[l],
                           glu_b[l], bs, ts, True)
        o_ssm = jnp.transpose(y_tm.reshape(ts, bs, SSM_WIDTH), (1, 0, 2)).reshape(bs * ts, SSM_WIDTH)
        pool_u = s3[:, :, COL_POOL:COL_AB]
        pext = jnp.concatenate([jnp.zeros((bs, 1, POOL_WIDTH), F32), state_pool[l], pool_u,
                                jnp.zeros((bs, pool_group - pool_first - ts, POOL_WIDTH), F32)], axis=1)
        y_ext = _pool_sample(pext.reshape(bs * pool_group, POOL_WIDTH), pool_bd[l], pscale[l], pool_group,
                             pool_first, PAST_LEN)
        o_pool = y_ext.reshape(bs, pool_group, POOL_WIDTH)[:, pool_first:pool_first + ts].reshape(bs * ts, POOL_WIDTH)
        xs = _post_mix(o_dn, o_ssm, o_pool, xs, w_out_b[l], nmo[l], nfp[l], wg_b[l], wu_b[l], wd_b[l], nfo[l],
                       256, None)
        outs_s.append((delta_new, s3[:, ts - (DN_CONV - 1):, :QKV_WIDTH],
                       h_fin[:, :SSM_NS].reshape(bs, SSM_GROUPS, SSM_STATE),
                       h_fin[:, SSM_NS:].reshape(bs, SSM_GROUPS, SSM_STATE),
                       jnp.concatenate([state_pool[l][:, ts:], pool_u], axis=1)))

    stack = lambda outs, k: jnp.stack([o[k] for o in outs])
    return (xp.reshape(bp, tp, D_MODEL), xs.reshape(bs, ts, D_MODEL),
            stack(outs_p, 0), stack(outs_p, 1), stack(outs_p, 2), stack(outs_p, 3), stack(outs_p, 4),
            stack(outs_s, 0), stack(outs_s, 1), stack(outs_s, 2), stack(outs_s, 3), stack(outs_s, 4))
```

```python
import functools
import math

import jax
import jax.numpy as jnp
from jax import lax
from jax.experimental import pallas as pl
from jax.experimental.pallas import tpu as pltpu

F32 = jnp.float32
BF16 = jnp.bfloat16
HIGHEST = lax.Precision.HIGHEST

D_MODEL = 1024
DN_HEADS = 4
DN_HEAD_DIM = 128
DN_WIDTH = DN_HEADS * DN_HEAD_DIM
DN_CONV = 4
QKV_WIDTH = 3 * DN_WIDTH
SSM_WIDTH = 256
SSM_GROUP = 16
SSM_GROUPS = 16
SSM_STATE = 64
SSM_NS = SSM_GROUPS * SSM_STATE
POOL_WIDTH = 256
POOL_WINDOWS = (2, 4, 8, 16)
POOL_GROUP = 64
POOL_BUF = 15
D_FF = 2816
EPS = 1e-6
PAST_LEN = 16384

LANES = 128
SUBLANES = 8

COL_GATE = QKV_WIDTH
COL_SSM = COL_GATE + DN_WIDTH
COL_POOL = COL_SSM + SSM_WIDTH
COL_AB = COL_POOL + POOL_WIDTH
PROJ_WIDTH = COL_AB + LANES

_OFF_A = QKV_WIDTH
_OFF_G = _OFF_A + 2 * DN_HEADS

VMEM_LIMIT_BYTES = 56 * 1024 * 1024

DELTA_CHUNK = 64
SAMPLE_CHUNK = 8
DELTA_SUB = 16


def _cparams(n_axes):
    return pltpu.CompilerParams(dimension_semantics=("arbitrary",) * n_axes,
                                vmem_limit_bytes=VMEM_LIMIT_BYTES)


def _layer(a, l, single=False):
    zeros = (0,) * (a.ndim - 1)
    mode = dict(pipeline_mode=pl.Buffered(1)) if single else {}
    return pl.BlockSpec((None,) + a.shape[1:], lambda *_: (l,) + zeros, **mode)


def _mm(a, b):
    return jnp.dot(a, b, preferred_element_type=F32)


def _bdot(a, b):
    return jnp.dot(a.astype(BF16), b.astype(BF16), preferred_element_type=F32)


def _bdot_nt(a, b):
    return lax.dot_general(a.astype(BF16), b.astype(BF16), (((1,), (1,)), ((), ())),
                           preferred_element_type=F32)


def _bdot_tn(a, b):
    return lax.dot_general(a.astype(BF16), b.astype(BF16), (((0,), (0,)), ((), ())),
                           preferred_element_type=F32)


def _hdot(a, b):
    return jnp.dot(a, b, precision=HIGHEST, preferred_element_type=F32)


def _rms(x, w):
    return x * lax.rsqrt(jnp.mean(x * x, axis=-1, keepdims=True) + EPS) * w


def _silu(x):
    return x * jax.nn.sigmoid(x)


def _in_proj_kernel(x_ref, nw_ref, wqkv_ref, wrest_ref, wab_ref, o_ref):
    h = _rms(x_ref[...], nw_ref[...]).astype(BF16)
    o_ref[:, :QKV_WIDTH] = _mm(h, wqkv_ref[...])
    o_ref[:, COL_GATE:COL_AB] = _mm(h, wrest_ref[...])
    o_ref[:, COL_AB:] = _mm(h, wab_ref[...])


def _in_proj(x, nw, wqkv, wrest, wab, l, tm):
    m = x.shape[0]
    return pl.pallas_call(
        _in_proj_kernel,
        grid=(m // tm,),
        in_specs=[pl.BlockSpec((tm, D_MODEL), lambda i: (i, 0)), _layer(nw, l),
                  _layer(wqkv, l, True), _layer(wrest, l, True), _layer(wab, l, True)],
        out_specs=pl.BlockSpec((tm, PROJ_WIDTH), lambda i: (i, 0)),
        out_shape=jax.ShapeDtypeStruct((m, PROJ_WIDTH), F32),
        compiler_params=_cparams(1),
        name="in_proj",
    )(x, nw, wqkv, wrest, wab)


def _post_mix_kernel(odn_ref, ossm_ref, opool_ref, x_ref, wo_ref, nmp_ref, nfp_ref, wg_ref, wu_ref, wd_ref,
                     nfo_ref, o_ref):
    mix = (_bdot(odn_ref[...], wo_ref[0:DN_WIDTH, :])
           + _bdot(ossm_ref[...], wo_ref[DN_WIDTH:DN_WIDTH + SSM_WIDTH, :])
           + _bdot(opool_ref[...], wo_ref[DN_WIDTH + SSM_WIDTH:, :]))
    x1 = x_ref[...] + _rms(mix, nmp_ref[...])
    h = _rms(x1, nfp_ref[...]).astype(BF16)
    g = _mm(h, wg_ref[...])
    u = _mm(h, wu_ref[...])
    f = _bdot(_silu(g) * u, wd_ref[...])
    o_ref[...] = x1 + _rms(f, nfo_ref[...])


def _post_mix(odn, ossm, opool, x, wo, nmp, nfp, wg, wu, wd, nfo, l, tm):
    m = x.shape[0]
    row = lambda w: pl.BlockSpec((tm, w), lambda i: (i, 0))
    return pl.pallas_call(
        _post_mix_kernel,
        grid=(m // tm,),
        in_specs=[row(DN_WIDTH), row(SSM_WIDTH), row(POOL_WIDTH), row(D_MODEL),
                  _layer(wo, l, True), _layer(nmp, l), _layer(nfp, l), _layer(wg, l, True), _layer(wu, l, True),
                  _layer(wd, l, True), _layer(nfo, l)],
        out_specs=row(D_MODEL),
        out_shape=jax.ShapeDtypeStruct((m, D_MODEL), F32),
        compiler_params=_cparams(1),
        name="post_mix",
    )(odn, ossm, opool, x, wo, nmp, nfp, wg, wu, wd, nfo)


def _split2(x):
    hi = x.astype(BF16)
    return hi, (x - hi.astype(F32)).astype(BF16)


def _group_cumsum(x, group, reverse=False):
    rows = x.shape[0]
    pos = lax.broadcasted_iota(jnp.int32, x.shape, 0) % group
    d = 1
    while d < group:
        if reverse:
            x = x + jnp.where(pos + d < group, pltpu.roll(x, rows - d, 0), 0.0)
        else:
            x = x + jnp.where(pos >= d, pltpu.roll(x, d, 0), 0.0)
        d *= 2
    return x


def _chunk_masks(rows, blk):
    r = lax.broadcasted_iota(jnp.int32, (rows, rows), 0)
    s = lax.broadcasted_iota(jnp.int32, (rows, rows), 1)
    d = r - s
    if blk < rows:
        d = jnp.where(r // blk == s // blk, d, -1)
    sub = None
    if blk > DELTA_SUB:
        sub = r // DELTA_SUB == s // DELTA_SUB
    return d >= 0, d > 0, (r == s).astype(F32), sub


def _neumann(a, eye, index):
    t = [eye - x for x in a]
    p = a
    n = 2
    while n < index:
        p = [_bdot(x, x) for x in p]
        t = [x + _bdot(x, y) for x, y in zip(t, p)]
        n *= 2
    return t


def _unit_lower_inverse(a, eye, sub, index):
    if sub is None:
        return _neumann(a, eye, index)
    assert index <= 4 * DELTA_SUB
    d = [jnp.where(sub, x, 0.0) for x in a]
    td = _neumann(d, eye, DELTA_SUB)
    n = [_bdot(t, x - y) for t, x, y in zip(td, a, d)]
    n2 = [_bdot(x, x) for x in n]
    m = [_bdot(eye - x, eye + y) for x, y in zip(n, n2)]
    return [_bdot(x, t) for x, t in zip(m, td)]


def _delta_local(probs, masks, index):
    incl, strict, eye, sub = masks
    rows = probs[0][0].shape[0]
    kb = [k * beta for (_, k, _, beta, _, _, _) in probs]
    kq = [_bdot_nt(jnp.concatenate([x, q], axis=0), k) for x, (q, k, _, _, _, _, _) in zip(kb, probs)]
    decay = [jnp.where(incl, jnp.exp(jnp.where(incl, gam - grow, 0.0)), 0.0) for (_, _, _, _, gam, _, grow) in probs]
    a = [jnp.where(strict, x[:rows] * d, 0.0) for x, d in zip(kq, decay)]
    qk = [x[rows:] * d for x, d in zip(kq, decay)]
    egam = [jnp.exp(p[4]) for p in probs]
    rhs = [jnp.concatenate([p[2] * p[3], x * e], axis=1) for p, x, e in zip(probs, kb, egam)]
    t = _unit_lower_inverse(a, eye, sub, index)
    sol = [_bdot(x, r) for x, r in zip(t, rhs)]
    a_sp = [_split2(x) for x in a]
    s_sp = [_split2(x) for x in sol]
    asol = [_mm(ah, sh) + _mm(ah, sl) + _mm(al, sh) for (ah, al), (sh, sl) in zip(a_sp, s_sp)]
    resid = [r - s - x for r, s, x in zip(rhs, sol, asol)]
    sol = [s + _bdot(x, r) for s, x, r in zip(sol, t, resid)]
    return [(s[:, :DN_HEAD_DIM], s[:, DN_HEAD_DIM:], x, p[0] * e, p[1] * jnp.exp(p[5]))
            for s, x, p, e in zip(sol, qk, probs, egam)]


def _delta_front(y, ab, alog, dtb):
    y = _silu(y)
    qs, ks = [], []
    for h in range(DN_HEADS):
        qh = y[:, h * DN_HEAD_DIM:(h + 1) * DN_HEAD_DIM]
        kh = y[:, DN_WIDTH + h * DN_HEAD_DIM:DN_WIDTH + (h + 1) * DN_HEAD_DIM]
        qs.append(qh * lax.rsqrt(jnp.sum(qh * qh, axis=-1, keepdims=True) + EPS) * (DN_HEAD_DIM ** -0.5))
        ks.append(kh * lax.rsqrt(jnp.sum(kh * kh, axis=-1, keepdims=True) + EPS))
    v = y[:, 2 * DN_WIDTH:]
    z = ab + dtb
    softplus = jnp.maximum(z, 0.0) + jnp.log1p(jnp.exp(-jnp.abs(z)))
    logg = -jnp.exp(alog) * softplus
    beta = jax.nn.sigmoid(ab)
    return qs, ks, v, logg, beta


def _delta_out(o, gate, onorm):
    return o * lax.rsqrt(jnp.mean(o * o, axis=-1, keepdims=True) + EPS) * onorm * _silu(gate)


def _conv4(xfull, w, rows, off):
    z = xfull * w[0:1]
    z = xfull * w[1:2] + pltpu.roll(z, 1, 0)
    z = xfull * w[2:3] + pltpu.roll(z, 1, 0)
    return (xfull * w[3:4] + pltpu.roll(z, 1, 0))[off:off + rows]


def _delta_prompt_kernel(qkv_ref, ab_ref, gate_ref, cst_ref, s0_ref, cw_ref, alog_ref, dtb_ref, onorm_ref,
                         o_ref, sfin_ref, tail, s_scr, *, nseq, tb, c):
    i = pl.program_id(1)

    @pl.when(i == 0)
    def _():
        s_scr[...] = s0_ref[...]
        tail[...] = cst_ref[...]

    masks = _chunk_masks(c, c)
    nchunk = tb // c
    heads = range(DN_HEADS)
    seqs = range(nseq)
    hsl = [slice(h * DN_HEAD_DIM, (h + 1) * DN_HEAD_DIM) for h in heads]
    probs, gams = [], []
    for r in seqs:
        x = qkv_ref[r]
        xfull = jnp.concatenate([tail[r], x], axis=0)
        y = _conv4(xfull, cw_ref[...], tb, SUBLANES)
        tail[r] = x[tb - SUBLANES:, :]
        qs, ks, v, logg, beta = _delta_front(y, ab_ref[r], alog_ref[...], dtb_ref[...])
        gam = _group_cumsum(logg, c)
        gexc = _group_cumsum(logg, c, reverse=True) - logg
        gam_t = gam.T
        gams.append(gam)
        for j in range(nchunk):
            rs = slice(j * c, (j + 1) * c)
            for h in heads:
                probs.append((qs[h][rs], ks[h][rs], v[rs, hsl[h]], beta[rs, DN_HEADS + h:DN_HEADS + h + 1],
                              gam[rs, h:h + 1], gexc[rs, h:h + 1], gam_t[h:h + 1, rs]))
    local = _delta_local(probs, masks, c)
    chains = [(r, h) for r in seqs for h in heads]
    s = {ch: s_scr[ch[0], ch[1]] for ch in chains}
    outs = {ch: [] for ch in chains}
    for j in range(nchunk):
        loc = {(r, h): local[(r * nchunk + j) * DN_HEADS + h] for r, h in chains}
        ks_ = {ch: _bdot(jnp.concatenate([loc[ch][1], loc[ch][3]], axis=0), s[ch]) for ch in chains}
        u = {ch: loc[ch][0] - ks_[ch][:c] for ch in chains}
        for ch in chains:
            outs[ch].append(ks_[ch][c:] + _bdot(loc[ch][2], u[ch]))
        last = (j + 1) * c - 1
        s = {(r, h): s[r, h] * jnp.exp(gams[r][last:last + 1, h:h + 1]) + _bdot_tn(loc[r, h][4], u[r, h])
             for r, h in chains}
    for r, h in chains:
        s_scr[r, h] = s[r, h]
        o = _delta_out(jnp.concatenate(outs[r, h], axis=0), gate_ref[r, :, hsl[h]], onorm_ref[...])
        o_ref[r, :, hsl[h]] = o.astype(o_ref.dtype)
    sfin_ref[...] = s_scr[...]


def _delta_prompt(p3, cst, s0, cw, alog, dtb, onorm, l, nseq, tb):
    nb, t, _ = p3.shape
    kern = functools.partial(_delta_prompt_kernel, nseq=nseq, tb=tb, c=DELTA_CHUNK)
    sspec = pl.BlockSpec((nseq, DN_HEADS, DN_HEAD_DIM, DN_HEAD_DIM), lambda b, i: (b, 0, 0, 0))
    return pl.pallas_call(
        kern,
        grid=(nb // nseq, t // tb),
        in_specs=[pl.BlockSpec((nseq, tb, QKV_WIDTH), lambda b, i: (b, i, 0)),
                  pl.BlockSpec((nseq, tb, LANES), lambda b, i: (b, i, COL_AB // LANES)),
                  pl.BlockSpec((nseq, tb, DN_WIDTH), lambda b, i: (b, i, COL_GATE // DN_WIDTH)),
                  pl.BlockSpec((nseq, SUBLANES, QKV_WIDTH), lambda b, i: (b, 0, 0)),
                  sspec, _layer(cw, l), _layer(alog, l), _layer(dtb, l), _layer(onorm, l)],
        out_specs=[pl.BlockSpec((nseq, tb, DN_WIDTH), lambda b, i: (b, i, 0)), sspec],
        out_shape=[jax.ShapeDtypeStruct((nb, t, DN_WIDTH), BF16),
                   jax.ShapeDtypeStruct((nb, DN_HEADS, DN_HEAD_DIM, DN_HEAD_DIM), F32)],
        scratch_shapes=[pltpu.VMEM((nseq, SUBLANES, QKV_WIDTH), F32),
                        pltpu.VMEM((nseq, DN_HEADS, DN_HEAD_DIM, DN_HEAD_DIM), F32)],
        compiler_params=_cparams(2),
        name="delta_prompt",
    )(p3, p3, p3, cst, s0, cw, alog, dtb, onorm)


def _delta_sample_kernel(qkv_ref, ab_ref, gate_ref, s0_ref, cw_ref, alog_ref, dtb_ref, onorm_ref, *rest,
                         nseq, c, first):
    o_ref, snew_ref = rest[-2:]
    rows = nseq * c
    x = qkv_ref[...]
    y = _conv4(x, cw_ref[...], rows, 0)
    qs, ks, v, logg, beta = _delta_front(y, ab_ref[...], alog_ref[...], dtb_ref[...])
    valid = (lax.broadcasted_iota(jnp.int32, (rows, LANES), 0) % c) >= first
    logg = jnp.where(valid, logg, 0.0)
    beta = jnp.where(valid, beta, 0.0)
    gam = _group_cumsum(logg, c)
    gexc = _group_cumsum(logg, c, reverse=True) - logg
    gam_t = gam.T
    masks = _chunk_masks(rows, c)
    gate = gate_ref[...]
    heads = range(DN_HEADS)
    hsl = [slice(h * DN_HEAD_DIM, (h + 1) * DN_HEAD_DIM) for h in heads]
    local = _delta_local([(qs[h], ks[h], v[:, hsl[h]], beta[:, DN_HEADS + h:DN_HEADS + h + 1], gam[:, h:h + 1],
                           gexc[:, h:h + 1], gam_t[h:h + 1, :]) for h in heads], masks, c)
    pairs = [(b, h) for b in range(nseq) for h in heads]
    rsl = [slice(b * c, (b + 1) * c) for b in range(nseq)]
    ks_ = {(b, h): _bdot(jnp.concatenate([local[h][1][rsl[b]], local[h][3][rsl[b]]], axis=0), s0_ref[b, h])
           for b, h in pairs}
    u = {(b, h): local[h][0][rsl[b]] - ks_[b, h][:c] for b, h in pairs}
    for b, h in pairs:
        last = (b + 1) * c - 1
        snew_ref[b, h] = (s0_ref[b, h] * jnp.exp(gam[last:last + 1, h:h + 1])
                          + _bdot_tn(local[h][4][rsl[b]], u[b, h]))
    for h in heads:
        o = (jnp.concatenate([ks_[b, h][c:] for b in range(nseq)], axis=0)
             + _bdot(local[h][2], jnp.concatenate([u[b, h] for b in range(nseq)], axis=0)))
        o_ref[:, hsl[h]] = _delta_out(o, gate[:, hsl[h]], onorm_ref[...]).astype(o_ref.dtype)


def _delta_sample(ext, s0_all, new_all, cw, alog, dtb, onorm, l, nseq, first):
    c = SAMPLE_CHUNK
    nb = s0_all.shape[1]
    rows = nseq * c
    kern = functools.partial(_delta_sample_kernel, nseq=nseq, c=c, first=first)
    sspec = pl.BlockSpec((None, nseq, DN_HEADS, DN_HEAD_DIM, DN_HEAD_DIM), lambda i: (l, i, 0, 0, 0))
    in_specs = [pl.BlockSpec((rows, QKV_WIDTH), lambda i: (i, 0)),
                pl.BlockSpec((rows, LANES), lambda i: (i, COL_AB // LANES)),
                pl.BlockSpec((rows, DN_WIDTH), lambda i: (i, COL_GATE // DN_WIDTH)),
                sspec, _layer(cw, l), _layer(alog, l), _layer(dtb, l), _layer(onorm, l)]
    args = [ext, ext, ext, s0_all, cw, alog, dtb, onorm]
    aliases = {}
    if new_all is not None:
        in_specs.append(pl.BlockSpec(memory_space=pl.ANY))
        args.append(new_all)
        aliases = {len(args) - 1: 1}
    return pl.pallas_call(
        kern,
        grid=(nb // nseq,),
        in_specs=in_specs,
        out_specs=[pl.BlockSpec((rows, DN_WIDTH), lambda i: (i, 0)), sspec],
        out_shape=[jax.ShapeDtypeStruct((nb * c, DN_WIDTH), BF16),
                   jax.ShapeDtypeStruct(s0_all.shape, F32)],
        input_output_aliases=aliases,
        compiler_params=_cparams(1),
        name="delta_sample",
    )(*args)


def _ssm_prep_kernel(are_ref, aim_ref, dt_ref, bre_ref, bim_ref, lre_ref, lim_ref, bmat_ref):
    ar, ai, dt = are_ref[0], aim_ref[0], jnp.exp(dt_ref[0])
    mag = jnp.exp(ar * dt)
    lr = mag * jnp.cos(ai * dt)
    li = mag * jnp.sin(ai * dt)
    lre_ref[0] = lr
    lim_ref[0] = li
    den = ar * ar + ai * ai
    fr = ((lr - 1.0) * ar + li * ai) / den
    fi = (li * ar - (lr - 1.0) * ai) / den
    br, bi = bre_ref[0], bim_ref[0]
    bbr = fr * br - fi * bi
    bbi = fr * bi + fi * br
    lane_group = lax.broadcasted_iota(jnp.int32, (SSM_GROUP, SSM_NS), 1) // SSM_STATE
    for g in range(SSM_GROUPS):
        m = lane_group == g
        bmat_ref[0, g * SSM_GROUP:(g + 1) * SSM_GROUP, 0:SSM_NS] = jnp.where(m, bbr, 0.0)
        bmat_ref[0, g * SSM_GROUP:(g + 1) * SSM_GROUP, SSM_NS:] = jnp.where(m, bbi, 0.0)


def _ssm_prep(are, aim, dt, bre, bim):
    depth = are.shape[0]
    vec = pl.BlockSpec((1, 1, SSM_NS), lambda l: (l, 0, 0))
    mat = pl.BlockSpec((1, SSM_GROUP, SSM_NS), lambda l: (l, 0, 0))
    return pl.pallas_call(
        _ssm_prep_kernel,
        grid=(depth,),
        in_specs=[vec, vec, vec, mat, mat],
        out_specs=[vec, vec, pl.BlockSpec((1, SSM_WIDTH, 2 * SSM_NS), lambda l: (l, 0, 0))],
        out_shape=[jax.ShapeDtypeStruct((depth, 1, SSM_NS), F32), jax.ShapeDtypeStruct((depth, 1, SSM_NS), F32),
                   jax.ShapeDtypeStruct((depth, SSM_WIDTH, 2 * SSM_NS), F32)],
        compiler_params=_cparams(1),
        name="ssm_prep",
    )(are, aim, dt, bre, bim)


def _gelu_tanh(x):
    return 0.5 * x * (1.0 + jnp.tanh(math.sqrt(2.0 / math.pi) * (x + 0.044715 * (x * x * x))))


def _ssm_kernel(ulo_ref, uhi_ref, h0_ref, bmat_ref, lre_ref, lim_ref, cre_ref, cim_ref, d_ref, gw_ref, gb_ref,
                y_ref, hl_ref, buf, h_scr, tm_scr, bm_scr, *, nb, nt, exact_in):
    i = pl.program_id(0)
    halves = range(SSM_WIDTH // LANES)
    lanes = [slice(s * LANES, (s + 1) * LANES) for s in halves]

    @pl.when(i == 0)
    def _():
        h_scr[...] = h0_ref[...]

    for s, u_ref in enumerate((ulo_ref, uhi_ref)):
        if nb <= nt:
            for b in range(nb):
                tm_scr[s, pl.ds(b, nt, stride=nb), :] = u_ref[b]
        else:
            for t in range(nt):
                tm_scr[s, t * nb:(t + 1) * nb, :] = u_ref[pl.ds(t, nb, stride=nt), :]
    u = jnp.concatenate([tm_scr[s] for s in halves], axis=1)
    buf[...] = _hdot(u, bmat_ref[...]) if exact_in else _bdot(u, bmat_ref[...])
    lr = jnp.broadcast_to(lre_ref[...], (SUBLANES, SSM_NS))
    li = jnp.broadcast_to(lim_ref[...], (SUBLANES, SSM_NS))
    for g in range(nb // SUBLANES):
        def step(t, h, g=g):
            r0 = pl.multiple_of(t * nb + g * SUBLANES, SUBLANES)
            bu = buf[pl.ds(r0, SUBLANES), :]
            hre, him = h[:, :SSM_NS], h[:, SSM_NS:]
            h = jnp.concatenate([lr * hre - li * him + bu[:, :SSM_NS],
                                 lr * him + li * hre + bu[:, SSM_NS:]], axis=1)
            buf[pl.ds(r0, SUBLANES), :] = h
            return h

        gs = slice(g * SUBLANES, (g + 1) * SUBLANES)
        h_fin = lax.fori_loop(0, nt, step, h_scr[gs, :])
        h_scr[gs, :] = h_fin
    hl_ref[...] = h_scr[...]
    y = _bdot(buf[:, :SSM_NS], cre_ref[...]) - _bdot(buf[:, SSM_NS:], cim_ref[...]) + d_ref[...] * u
    y = _gelu_tanh(y)
    y = y * jax.nn.sigmoid(_bdot(y, gw_ref[...]) + gb_ref[...])
    for s in halves:
        tm_scr[s] = y[:, lanes[s]]
        if nb <= nt:
            for b in range(nb):
                y_ref[b, :, lanes[s]] = tm_scr[s, pl.ds(b, nt, stride=nb), :].astype(y_ref.dtype)
        else:
            for t in range(nt):
                bm_scr[s, pl.ds(t, nb, stride=nt), :] = tm_scr[s, t * nb:(t + 1) * nb, :]
            y_ref[:, lanes[s]] = bm_scr[s].astype(y_ref.dtype)


def _ssm(proj, h0, bmat, lre, lim, cre, cim, dskip, gw, gb, l, nt):
    nb = h0.shape[0]
    rows = nt * nb
    lo = COL_SSM // LANES
    if proj.ndim == 3:
        t = proj.shape[1]
        grid = (t // nt,)
        u_spec = lambda c: pl.BlockSpec((nb, nt, LANES), lambda i: (0, i, c))
        y_spec = pl.BlockSpec((nb, nt, SSM_WIDTH), lambda i: (0, i, 0))
        y_shape = (nb, t, SSM_WIDTH)
    else:
        grid = (1,)
        u_spec = lambda c: pl.BlockSpec((rows, LANES), lambda i: (0, c))
        y_spec = pl.BlockSpec((rows, SSM_WIDTH), lambda i: (0, 0))
        y_shape = (rows, SSM_WIDTH)
    kern = functools.partial(_ssm_kernel, nb=nb, nt=nt, exact_in=proj.ndim == 2)
    const = lambda a: pl.BlockSpec(a.shape, lambda i: (0,) * a.ndim)
    return pl.pallas_call(
        kern,
        grid=grid,
        in_specs=[u_spec(lo), u_spec(lo + 1), const(h0), _layer(bmat, l), _layer(lre, l), _layer(lim, l),
                  _layer(cre, l), _layer(cim, l), _layer(dskip, l), _layer(gw, l), _layer(gb, l)],
        out_specs=[y_spec, const(h0)],
        out_shape=[jax.ShapeDtypeStruct(y_shape, BF16), jax.ShapeDtypeStruct(h0.shape, F32)],
        scratch_shapes=[pltpu.VMEM((rows, 2 * SSM_NS), F32), pltpu.VMEM(h0.shape, F32),
                        pltpu.VMEM((SSM_WIDTH // LANES, rows, LANES), F32),
                        pltpu.VMEM((SSM_WIDTH // LANES, rows, LANES), F32)],
        compiler_params=_cparams(1),
        name="ssm",
    )(proj, proj, h0, bmat, lre, lim, cre, cim, dskip, gw, gb)


def _pool_windows(xfull):
    s2 = xfull + pltpu.roll(xfull, 1, 0)
    s4 = s2 + pltpu.roll(s2, 2, 0)
    s8 = s4 + pltpu.roll(s4, 4, 0)
    s16 = s8 + pltpu.roll(s8, 8, 0)
    return s2, s4, s8, s16


def _pool_mix(sums, x, pos, w_ref, scale_ref):
    lane = lax.broadcasted_iota(jnp.int32, (1, POOL_WIDTH), 1) // POOL_GROUP
    win = None
    for gidx in reversed(range(len(POOL_WINDOWS))):
        cnt = jnp.minimum(pos + 1, POOL_WINDOWS[gidx]).astype(F32)
        term = sums[gidx] / cnt
        win = term if win is None else jnp.where(lane == gidx, term, win)
    r = win - x
    return _bdot(r, w_ref[...]) * scale_ref[...]


def _pool_prompt_kernel(u_ref, st_ref, w_ref, scale_ref, y_ref, tail, *, tb, pos0):
    i = pl.program_id(1)
    halo = 2 * SUBLANES

    @pl.when(i == 0)
    def _():
        tail[...] = st_ref[0]

    x = u_ref[0]
    xfull = jnp.concatenate([tail[...], x], axis=0)
    tail[...] = x[tb - halo:, :]
    sums = [s[halo:] for s in _pool_windows(xfull)]
    pos = pos0 + i * tb + lax.broadcasted_iota(jnp.int32, (tb, 1), 0)
    y_ref[0] = _pool_mix(sums, x, pos, w_ref, scale_ref).astype(y_ref.dtype)


def _pool_prompt(p3, st, wbd, scale, l, tb, pos0):
    nb, t, _ = p3.shape
    kern = functools.partial(_pool_prompt_kernel, tb=tb, pos0=pos0)
    return pl.pallas_call(
        kern,
        grid=(nb, t // tb),
        in_specs=[pl.BlockSpec((1, tb, POOL_WIDTH), lambda b, i: (b, i, COL_POOL // POOL_WIDTH)),
                  pl.BlockSpec((1, 2 * SUBLANES, POOL_WIDTH), lambda b, i: (b, 0, 0)),
                  _layer(wbd, l), _layer(scale, l)],
        out_specs=pl.BlockSpec((1, tb, POOL_WIDTH), lambda b, i: (b, i, 0)),
        out_shape=jax.ShapeDtypeStruct((nb, t, POOL_WIDTH), BF16),
        scratch_shapes=[pltpu.VMEM((2 * SUBLANES, POOL_WIDTH), F32)],
        compiler_params=_cparams(2),
        name="pool_prompt",
    )(p3, st, wbd, scale)


def _pool_sample_kernel(x_ref, w_ref, scale_ref, y_ref, *, group, first, pos0):
    x = x_ref[...]
    rows = x.shape[0]
    sums = _pool_windows(x)
    pos = pos0 + (lax.broadcasted_iota(jnp.int32, (rows, 1), 0) % group) - first
    y_ref[...] = _pool_mix(sums, x, jnp.maximum(pos, 0), w_ref, scale_ref).astype(y_ref.dtype)


def _pool_sample(ext, wbd, scale, l, group, first, pos0):
    kern = functools.partial(_pool_sample_kernel, group=group, first=first, pos0=pos0)
    const = lambda a: pl.BlockSpec(a.shape, lambda i: (0,) * a.ndim)
    return pl.pallas_call(
        kern,
        grid=(1,),
        in_specs=[const(ext), _layer(wbd, l), _layer(scale, l)],
        out_specs=const(ext),
        out_shape=jax.ShapeDtypeStruct(ext.shape, BF16),
        compiler_params=_cparams(1),
        name="pool_sample",
    )(ext, wbd, scale)


def _block_diag(blocks):
    g, r, c = blocks.shape
    eye = jnp.eye(g, dtype=blocks.dtype)
    return (eye[:, None, :, None] * blocks[:, :, None, :]).reshape(g * r, g * c)


def kernel(x_prompt, x_sample, state_delta, state_conv, state_ssm_re, state_ssm_im, state_pool, norm_mix_pre, norm_mix_post, norm_ffn_pre, norm_ffn_post, w_in, conv_w, dn_a_log, dn_dt_bias, dn_out_norm, ssm_a_re, ssm_a_im, ssm_log_dt, ssm_b_re, ssm_b_im, ssm_c_re, ssm_c_im, ssm_d, ssm_glu_w, ssm_glu_b, pool_w, pool_scale, w_out, ffn_w_gate, ffn_w_up, ffn_w_down):
    depth = w_in.shape[0]
    bp, tp, _ = x_prompt.shape
    bs, ts, _ = x_sample.shape

    w_qkv = w_in[:, :, :QKV_WIDTH].astype(BF16)
    w_rest = w_in[:, :, _OFF_G:].astype(BF16)
    w_ab = jnp.pad(w_in[:, :, _OFF_A:_OFF_G], ((0, 0), (0, 0), (0, LANES - 2 * DN_HEADS))).astype(BF16)
    w_out_b = w_out.astype(BF16)
    wg_b, wu_b, wd_b = ffn_w_gate.astype(BF16), ffn_w_up.astype(BF16), ffn_w_down.astype(BF16)
    row = lambda a: a.reshape(depth, 1, -1)
    nmp, nmo, nfp, nfo = row(norm_mix_pre), row(norm_mix_post), row(norm_ffn_pre), row(norm_ffn_post)
    alog = jnp.pad(dn_a_log, ((0, 0), (0, LANES - DN_HEADS))).reshape(depth, 1, LANES)
    dtb = jnp.pad(dn_dt_bias, ((0, 0), (0, LANES - DN_HEADS))).reshape(depth, 1, LANES)
    onorm = row(dn_out_norm)
    dt_full = jnp.repeat(ssm_log_dt, SSM_STATE, axis=1).reshape(depth, 1, SSM_NS)
    b_t = lambda b: jnp.transpose(b, (0, 3, 1, 2)).reshape(depth, SSM_GROUP, SSM_NS)
    lam_re, lam_im, bmat = _ssm_prep(ssm_a_re.reshape(depth, 1, SSM_NS), ssm_a_im.reshape(depth, 1, SSM_NS),
                                     dt_full, b_t(ssm_b_re), b_t(ssm_b_im))
    c_bd = lambda cc: jax.vmap(_block_diag)(jnp.transpose(cc, (0, 1, 3, 2))).astype(BF16)
    cre, cim = c_bd(ssm_c_re), c_bd(ssm_c_im)
    dskip, glu_b = row(ssm_d), row(ssm_glu_b)
    glu_w = ssm_glu_w.astype(BF16)
    pool_bd = jax.vmap(_block_diag)(pool_w).astype(BF16)
    pscale = row(pool_scale)

    xp = x_prompt.reshape(bp * tp, D_MODEL)
    xs = x_sample.reshape(bs * ts, D_MODEL)
    zero_conv = jnp.zeros((bp, SUBLANES, QKV_WIDTH), F32)
    zero_delta = jnp.zeros((bp, DN_HEADS, DN_HEAD_DIM, DN_HEAD_DIM), F32)
    zero_h = jnp.zeros((bp, 2 * SSM_NS), F32)
    zero_pool = jnp.zeros((bp, 2 * SUBLANES, POOL_WIDTH), F32)
    pad_rows = SAMPLE_CHUNK - ts - (DN_CONV - 1)
    pool_group = 24
    pool_first = 1 + POOL_BUF
    h0_s = jnp.concatenate([state_ssm_re.reshape(depth, bs, SSM_NS), state_ssm_im.reshape(depth, bs, SSM_NS)],
                           axis=2)

    outs_p, outs_s = [], []
    delta_s = None
    for l in range(depth):
        proj = _in_proj(xp, nmp, w_qkv, w_rest, w_ab, l, 512)
        p3 = proj.reshape(bp, tp, PROJ_WIDTH)
        o_dn, delta_new = _delta_prompt(p3, zero_conv, zero_delta, conv_w, alog, dtb, onorm, l, 2, 256)
        o_ssm, h_fin = _ssm(p3, zero_h, bmat, lam_re, lam_im, cre, cim, dskip, glu_w, glu_b, l, 128)
        o_pool = _pool_prompt(p3, zero_pool, pool_bd, pscale, l, 512, 0)
        xp = _post_mix(o_dn.reshape(bp * tp, DN_WIDTH), o_ssm.reshape(bp * tp, SSM_WIDTH),
                       o_pool.reshape(bp * tp, POOL_WIDTH), xp,
                       w_out_b, nmo, nfp, wg_b, wu_b, wd_b, nfo, l, 256)
        outs_p.append((delta_new, p3[:, tp - (DN_CONV - 1):, :QKV_WIDTH],
                       h_fin[:, :SSM_NS].reshape(bp, SSM_GROUPS, SSM_STATE),
                       h_fin[:, SSM_NS:].reshape(bp, SSM_GROUPS, SSM_STATE),
                       p3[:, tp - POOL_BUF:, COL_POOL:COL_AB]))

        proj = _in_proj(xs, nmp, w_qkv, w_rest, w_ab, l, 256)
        s3 = proj.reshape(bs, ts, PROJ_WIDTH)
        head = jnp.concatenate([jnp.zeros((bs, pad_rows, QKV_WIDTH), F32), state_conv[l]], axis=1)
        head = jnp.pad(head, ((0, 0), (0, 0), (0, PROJ_WIDTH - QKV_WIDTH)))
        ext = jnp.concatenate([head, s3], axis=1).reshape(bs * SAMPLE_CHUNK, PROJ_WIDTH)
        o_ext, delta_s = _delta_sample(ext, state_delta, delta_s, conv_w, alog, dtb, onorm, l, 16,
                                       SAMPLE_CHUNK - ts)
        o_dn = o_ext.reshape(bs, SAMPLE_CHUNK, DN_WIDTH)[:, SAMPLE_CHUNK - ts:].reshape(bs * ts, DN_WIDTH)
        o_ssm, h_fin = _ssm(proj, h0_s[l], bmat, lam_re, lam_im, cre, cim, dskip, glu_w, glu_b, l, ts)
        pool_u = s3[:, :, COL_POOL:COL_AB]
        pext = jnp.concatenate([jnp.zeros((bs, 1, POOL_WIDTH), F32), state_pool[l], pool_u,
                                jnp.zeros((bs, pool_group - pool_first - ts, POOL_WIDTH), F32)], axis=1)
        y_ext = _pool_sample(pext.reshape(bs * pool_group, POOL_WIDTH), pool_bd, pscale, l, pool_group,
                             pool_first, PAST_LEN)
        o_pool = y_ext.reshape(bs, pool_group, POOL_WIDTH)[:, pool_first:pool_first + ts].reshape(bs * ts, POOL_WIDTH)
        xs = _post_mix(o_dn, o_ssm, o_pool, xs, w_out_b, nmo, nfp, wg_b, wu_b, wd_b, nfo, l, 256)
        outs_s.append((s3[:, ts - (DN_CONV - 1):, :QKV_WIDTH],
                       h_fin[:, :SSM_NS].reshape(bs, SSM_GROUPS, SSM_STATE),
                       h_fin[:, SSM_NS:].reshape(bs, SSM_GROUPS, SSM_STATE),
                       jnp.concatenate([state_pool[l][:, ts:], pool_u], axis=1)))

    stack = lambda outs, k: jnp.stack([o[k] for o in outs])
    return (xp.reshape(bp, tp, D_MODEL), xs.reshape(bs, ts, D_MODEL),
            stack(outs_p, 0), stack(outs_p, 1), stack(outs_p, 2), stack(outs_p, 3), stack(outs_p, 4),
            delta_s, stack(outs_s, 0), stack(outs_s, 1), stack(outs_s, 2), stack(outs_s, 3))
```

```python
import functools
import math

import jax
import jax.numpy as jnp
from jax import lax
from jax.experimental import pallas as pl
from jax.experimental.pallas import tpu as pltpu

F32 = jnp.float32
BF16 = jnp.bfloat16
HIGHEST = lax.Precision.HIGHEST

D_MODEL = 1024
DN_HEADS = 4
DN_HEAD_DIM = 128
DN_WIDTH = DN_HEADS * DN_HEAD_DIM
DN_CONV = 4
QKV_WIDTH = 3 * DN_WIDTH
SSM_WIDTH = 256
SSM_GROUP = 16
SSM_GROUPS = 16
SSM_STATE = 64
SSM_NS = SSM_GROUPS * SSM_STATE
POOL_WIDTH = 256
POOL_WINDOWS = (2, 4, 8, 16)
POOL_GROUP = 64
POOL_BUF = 15
D_FF = 2816
EPS = 1e-6
PAST_LEN = 16384

LANES = 128
SUBLANES = 8

COL_GATE = QKV_WIDTH
COL_SSM = COL_GATE + DN_WIDTH
COL_POOL = COL_SSM + SSM_WIDTH
COL_AB = COL_POOL + POOL_WIDTH
PROJ_WIDTH = COL_AB + LANES
REST_GATE = 0
REST_SSM = COL_SSM - COL_GATE
REST_POOL = COL_POOL - COL_GATE
REST_AB = COL_AB - COL_GATE
REST_WIDTH = PROJ_WIDTH - COL_GATE

_OFF_A = QKV_WIDTH
_OFF_G = _OFF_A + 2 * DN_HEADS

VMEM_LIMIT_BYTES = 56 * 1024 * 1024

DELTA_CHUNK = 64
SAMPLE_CHUNK = 8
DELTA_SUB = 16
POST_MIX_PARTS = 2


def _cparams(n_axes):
    return pltpu.CompilerParams(dimension_semantics=("arbitrary",) * n_axes,
                                vmem_limit_bytes=VMEM_LIMIT_BYTES)


def _layer(a, l, single=False):
    zeros = (0,) * (a.ndim - 1)
    mode = dict(pipeline_mode=pl.Buffered(1)) if single else {}
    return pl.BlockSpec((None,) + a.shape[1:], lambda *_: (l,) + zeros, **mode)


def _mm(a, b):
    return jnp.dot(a, b, preferred_element_type=F32)


def _bdot(a, b):
    return jnp.dot(a.astype(BF16), b.astype(BF16), preferred_element_type=F32)


def _bdot_nt(a, b):
    return lax.dot_general(a.astype(BF16), b.astype(BF16), (((1,), (1,)), ((), ())),
                           preferred_element_type=F32)


def _bdot_tn(a, b):
    return lax.dot_general(a.astype(BF16), b.astype(BF16), (((0,), (0,)), ((), ())),
                           preferred_element_type=F32)


def _hdot(a, b):
    return jnp.dot(a, b, precision=HIGHEST, preferred_element_type=F32)


def _rms(x, w):
    return x * lax.rsqrt(jnp.mean(x * x, axis=-1, keepdims=True) + EPS) * w


def _silu(x):
    return x * jax.nn.sigmoid(x)


def _in_proj_kernel(x_ref, nw_ref, wqkv_ref, wrest_ref, wab_ref, o_ref):
    h = _rms(x_ref[...], nw_ref[...]).astype(BF16)
    o_ref[:, :QKV_WIDTH] = _mm(h, wqkv_ref[...])
    o_ref[:, COL_GATE:COL_AB] = _mm(h, wrest_ref[...])
    o_ref[:, COL_AB:] = _mm(h, wab_ref[...])


def _in_proj(x, nw, wqkv, wrest, wab, l, tm):
    m = x.shape[0]
    return pl.pallas_call(
        _in_proj_kernel,
        grid=(m // tm,),
        in_specs=[pl.BlockSpec((tm, D_MODEL), lambda i: (i, 0)), _layer(nw, l),
                  _layer(wqkv, l, True), _layer(wrest, l, True), _layer(wab, l, True)],
        out_specs=pl.BlockSpec((tm, PROJ_WIDTH), lambda i: (i, 0)),
        out_shape=jax.ShapeDtypeStruct((m, PROJ_WIDTH), F32),
        compiler_params=_cparams(1),
        name="in_proj",
    )(x, nw, wqkv, wrest, wab)


def _in_proj_conv_kernel(x_ref, nw_ref, wqkv_ref, wrest_ref, wab_ref, cst_ref, cw_ref,
                         y_ref, o_ref, craw_ref, tail, *, tm, nt, nc):
    i = pl.program_id(0)

    @pl.when(i % nt == 0)
    def _():
        tail[...] = cst_ref[0]

    h = _rms(x_ref[...], nw_ref[...]).astype(BF16)
    for cc in range(QKV_WIDTH // nc):
        cs = slice(cc * nc, (cc + 1) * nc)
        r = _mm(h, wqkv_ref[:, cs])
        xfull = jnp.concatenate([tail[:, cs], r], axis=0)
        tail[:, cs] = r[tm - SUBLANES:, :]
        y = _silu(_conv4(xfull, cw_ref[:, cs], tm, SUBLANES))
        if cc * nc < 2 * DN_WIDTH:
            scale = DN_HEAD_DIM ** -0.5 if cc * nc < DN_WIDTH else 1.0
            parts = []
            for j in range(nc // DN_HEAD_DIM):
                yh = y[:, j * DN_HEAD_DIM:(j + 1) * DN_HEAD_DIM]
                parts.append(yh * lax.rsqrt(jnp.sum(yh * yh, axis=-1, keepdims=True) + EPS) * scale)
            y = jnp.concatenate(parts, axis=1)
        y_ref[:, cs] = y
    o_ref[:, :REST_AB] = _mm(h, wrest_ref[...])
    o_ref[:, REST_AB:] = _mm(h, wab_ref[...])
    craw_ref[0] = tail[...]


def _in_proj_conv(x, nw, wqkv, wrest, wab, cst, cw, l, tm, seq_len):
    m = x.shape[0]
    nt = seq_len // tm
    kern = functools.partial(_in_proj_conv_kernel, tm=tm, nt=nt, nc=2 * DN_HEAD_DIM)
    tail_spec = pl.BlockSpec((1, SUBLANES, QKV_WIDTH), lambda i: (i // nt, 0, 0))
    return pl.pallas_call(
        kern,
        grid=(m // tm,),
        in_specs=[pl.BlockSpec((tm, D_MODEL), lambda i: (i, 0)), _layer(nw, l),
                  _layer(wqkv, l, True), _layer(wrest, l, True), _layer(wab, l, True), tail_spec, _layer(cw, l)],
        out_specs=[pl.BlockSpec((tm, QKV_WIDTH), lambda i: (i, 0)),
                   pl.BlockSpec((tm, REST_WIDTH), lambda i: (i, 0)), tail_spec],
        out_shape=[jax.ShapeDtypeStruct((m, QKV_WIDTH), F32), jax.ShapeDtypeStruct((m, REST_WIDTH), F32),
                   jax.ShapeDtypeStruct((m // seq_len, SUBLANES, QKV_WIDTH), F32)],
        scratch_shapes=[pltpu.VMEM((SUBLANES, QKV_WIDTH), F32)],
        compiler_params=_cparams(1),
        name="in_proj_conv",
    )(x, nw, wqkv, wrest, wab, cst, cw)


def _post_mix_kernel(odn_ref, ossm_ref, opool_ref, x_ref, wo_ref, nmp_ref, nfp_ref, wg_ref, wu_ref, wd_ref,
                     nfo_ref, o_ref):
    tm = x_ref.shape[0]
    parts = [slice(p * tm // POST_MIX_PARTS, (p + 1) * tm // POST_MIX_PARTS) for p in range(POST_MIX_PARTS)]
    mix = [(_bdot(odn_ref[rs, :], wo_ref[0:DN_WIDTH, :])
            + _bdot(ossm_ref[rs, :], wo_ref[DN_WIDTH:DN_WIDTH + SSM_WIDTH, :])
            + _bdot(opool_ref[rs, :], wo_ref[DN_WIDTH + SSM_WIDTH:, :])) for rs in parts]
    x1 = [x_ref[rs, :] + _rms(m, nmp_ref[...]) for rs, m in zip(parts, mix)]
    h = [_rms(x, nfp_ref[...]).astype(BF16) for x in x1]
    g = [_mm(y, wg_ref[...]) for y in h]
    u = [_mm(y, wu_ref[...]) for y in h]
    f = [_bdot(_silu(a) * b, wd_ref[...]) for a, b in zip(g, u)]
    for rs, x, y in zip(parts, x1, f):
        o_ref[rs, :] = x + _rms(y, nfo_ref[...])


def _post_mix(odn, ossm, opool, x, wo, nmp, nfp, wg, wu, wd, nfo, l, tm):
    m = x.shape[0]
    row = lambda w: pl.BlockSpec((tm, w), lambda i: (i, 0))
    return pl.pallas_call(
        _post_mix_kernel,
        grid=(m // tm,),
        in_specs=[row(DN_WIDTH), row(SSM_WIDTH), row(POOL_WIDTH), row(D_MODEL),
                  _layer(wo, l, True), _layer(nmp, l), _layer(nfp, l), _layer(wg, l, True), _layer(wu, l, True),
                  _layer(wd, l, True), _layer(nfo, l)],
        out_specs=row(D_MODEL),
        out_shape=jax.ShapeDtypeStruct((m, D_MODEL), F32),
        compiler_params=_cparams(1),
        name="post_mix",
    )(odn, ossm, opool, x, wo, nmp, nfp, wg, wu, wd, nfo)


def _split2(x):
    hi = x.astype(BF16)
    return hi, (x - hi.astype(F32)).astype(BF16)


def _group_cumsum(x, group, reverse=False):
    rows = x.shape[0]
    pos = lax.broadcasted_iota(jnp.int32, x.shape, 0) % group
    d = 1
    while d < group:
        if reverse:
            x = x + jnp.where(pos + d < group, pltpu.roll(x, rows - d, 0), 0.0)
        else:
            x = x + jnp.where(pos >= d, pltpu.roll(x, d, 0), 0.0)
        d *= 2
    return x


def _chunk_masks(rows, blk):
    r = lax.broadcasted_iota(jnp.int32, (rows, rows), 0)
    s = lax.broadcasted_iota(jnp.int32, (rows, rows), 1)
    d = r - s
    if blk < rows:
        d = jnp.where(r // blk == s // blk, d, -1)
    sub = None
    if blk > DELTA_SUB:
        sub = r // DELTA_SUB == s // DELTA_SUB
    return d >= 0, d > 0, (r == s).astype(F32), sub


def _neumann(a, eye, index):
    t = [eye - x for x in a]
    p = a
    n = 2
    while n < index:
        p = [_bdot(x, x) for x in p]
        t = [x + _bdot(x, y) for x, y in zip(t, p)]
        n *= 2
    return t


def _unit_lower_inverse(a, eye, sub, index):
    if sub is None:
        return _neumann(a, eye, index)
    assert index <= 4 * DELTA_SUB
    d = [jnp.where(sub, x, 0.0) for x in a]
    td = _neumann(d, eye, DELTA_SUB)
    n = [_bdot(t, x - y) for t, x, y in zip(td, a, d)]
    n2 = [_bdot(x, x) for x in n]
    m = [_bdot(eye - x, eye + y) for x, y in zip(n, n2)]
    return [_bdot(x, t) for x, t in zip(m, td)]


def _delta_local(probs, masks, index):
    incl, strict, eye, sub = masks
    rows = probs[0][0].shape[0]
    kb = [k * beta for (_, k, _, beta, _, _, _) in probs]
    kq = [_bdot_nt(jnp.concatenate([x, q], axis=0), k) for x, (q, k, _, _, _, _, _) in zip(kb, probs)]
    decay = [jnp.where(incl, jnp.exp(jnp.where(incl, gam - grow, 0.0)), 0.0) for (_, _, _, _, gam, _, grow) in probs]
    a = [jnp.where(strict, x[:rows] * d, 0.0) for x, d in zip(kq, decay)]
    qk = [x[rows:] * d for x, d in zip(kq, decay)]
    egam = [jnp.exp(p[4]) for p in probs]
    rhs = [jnp.concatenate([p[2] * p[3], x * e], axis=1) for p, x, e in zip(probs, kb, egam)]
    t = _unit_lower_inverse(a, eye, sub, index)
    sol = [_bdot(x, r) for x, r in zip(t, rhs)]
    a_sp = [_split2(x) for x in a]
    s_sp = [_split2(x) for x in sol]
    asol = [_mm(ah, sh) + _mm(ah, sl) + _mm(al, sh) for (ah, al), (sh, sl) in zip(a_sp, s_sp)]
    resid = [r - s - x for r, s, x in zip(rhs, sol, asol)]
    sol = [s + _bdot(x, r) for s, x, r in zip(sol, t, resid)]
    return [(s[:, :DN_HEAD_DIM], s[:, DN_HEAD_DIM:], x, p[0] * e, p[1] * jnp.exp(p[5]))
            for s, x, p, e in zip(sol, qk, probs, egam)]


def _delta_front(y, ab, alog, dtb):
    y = _silu(y)
    qs, ks = [], []
    for h in range(DN_HEADS):
        qh = y[:, h * DN_HEAD_DIM:(h + 1) * DN_HEAD_DIM]
        kh = y[:, DN_WIDTH + h * DN_HEAD_DIM:DN_WIDTH + (h + 1) * DN_HEAD_DIM]
        qs.append(qh * lax.rsqrt(jnp.sum(qh * qh, axis=-1, keepdims=True) + EPS) * (DN_HEAD_DIM ** -0.5))
        ks.append(kh * lax.rsqrt(jnp.sum(kh * kh, axis=-1, keepdims=True) + EPS))
    v = y[:, 2 * DN_WIDTH:]
    logg, beta = _delta_gates(ab, alog, dtb)
    return qs, ks, v, logg, beta


def _delta_gates(ab, alog, dtb):
    z = ab + dtb
    softplus = jnp.maximum(z, 0.0) + jnp.log1p(jnp.exp(-jnp.abs(z)))
    return -jnp.exp(alog) * softplus, jax.nn.sigmoid(ab)


def _delta_out(o, gate, onorm):
    return o * lax.rsqrt(jnp.mean(o * o, axis=-1, keepdims=True) + EPS) * onorm * _silu(gate)


def _conv4(xfull, w, rows, off):
    z = xfull * w[0:1]
    z = xfull * w[1:2] + pltpu.roll(z, 1, 0)
    z = xfull * w[2:3] + pltpu.roll(z, 1, 0)
    return (xfull * w[3:4] + pltpu.roll(z, 1, 0))[off:off + rows]


def _delta_prompt_kernel(qkv_ref, ab_ref, gate_ref, s0_ref, alog_ref, dtb_ref, onorm_ref,
                         o_ref, sfin_ref, s_scr, *, nseq, tb, c):
    i = pl.program_id(1)

    @pl.when(i == 0)
    def _():
        s_scr[...] = s0_ref[...]

    masks = _chunk_masks(c, c)
    nchunk = tb // c
    heads = range(DN_HEADS)
    seqs = range(nseq)
    hsl = [slice(h * DN_HEAD_DIM, (h + 1) * DN_HEAD_DIM) for h in heads]
    probs, gams = [], []
    for r in seqs:
        qs = [qkv_ref[r, :, hsl[h]] for h in heads]
        ks = [qkv_ref[r, :, DN_WIDTH + h * DN_HEAD_DIM:DN_WIDTH + (h + 1) * DN_HEAD_DIM] for h in heads]
        v = qkv_ref[r, :, 2 * DN_WIDTH:]
        logg, beta = _delta_gates(ab_ref[r], alog_ref[...], dtb_ref[...])
        gam = _group_cumsum(logg, c)
        gexc = _group_cumsum(logg, c, reverse=True) - logg
        gam_t = gam.T
        gams.append(gam)
        for j in range(nchunk):
            rs = slice(j * c, (j + 1) * c)
            for h in heads:
                probs.append((qs[h][rs], ks[h][rs], v[rs, hsl[h]], beta[rs, DN_HEADS + h:DN_HEADS + h + 1],
                              gam[rs, h:h + 1], gexc[rs, h:h + 1], gam_t[h:h + 1, rs]))
    local = _delta_local(probs, masks, c)
    chains = [(r, h) for r in seqs for h in heads]
    s = {ch: s_scr[ch[0], ch[1]] for ch in chains}
    outs = {ch: [] for ch in chains}
    for j in range(nchunk):
        loc = {(r, h): local[(r * nchunk + j) * DN_HEADS + h] for r, h in chains}
        ks_ = {ch: _bdot(jnp.concatenate([loc[ch][1], loc[ch][3]], axis=0), s[ch]) for ch in chains}
        u = {ch: loc[ch][0] - ks_[ch][:c] for ch in chains}
        for ch in chains:
            outs[ch].append(ks_[ch][c:] + _bdot(loc[ch][2], u[ch]))
        last = (j + 1) * c - 1
        s = {(r, h): s[r, h] * jnp.exp(gams[r][last:last + 1, h:h + 1]) + _bdot_tn(loc[r, h][4], u[r, h])
             for r, h in chains}
    for r, h in chains:
        s_scr[r, h] = s[r, h]
        o = _delta_out(jnp.concatenate(outs[r, h], axis=0), gate_ref[r, :, hsl[h]], onorm_ref[...])
        o_ref[r, :, hsl[h]] = o.astype(o_ref.dtype)
    sfin_ref[...] = s_scr[...]


def _delta_prompt(y3, rest3, s0, alog, dtb, onorm, l, nseq, tb):
    nb, t, _ = y3.shape
    kern = functools.partial(_delta_prompt_kernel, nseq=nseq, tb=tb, c=DELTA_CHUNK)
    sspec = pl.BlockSpec((nseq, DN_HEADS, DN_HEAD_DIM, DN_HEAD_DIM), lambda b, i: (b, 0, 0, 0))
    return pl.pallas_call(
        kern,
        grid=(nb // nseq, t // tb),
        in_specs=[pl.BlockSpec((nseq, tb, QKV_WIDTH), lambda b, i: (b, i, 0)),
                  pl.BlockSpec((nseq, tb, LANES), lambda b, i: (b, i, REST_AB // LANES)),
                  pl.BlockSpec((nseq, tb, DN_WIDTH), lambda b, i: (b, i, REST_GATE // DN_WIDTH)),
                  sspec, _layer(alog, l), _layer(dtb, l), _layer(onorm, l)],
        out_specs=[pl.BlockSpec((nseq, tb, DN_WIDTH), lambda b, i: (b, i, 0)), sspec],
        out_shape=[jax.ShapeDtypeStruct((nb, t, DN_WIDTH), BF16),
                   jax.ShapeDtypeStruct((nb, DN_HEADS, DN_HEAD_DIM, DN_HEAD_DIM), F32)],
        scratch_shapes=[pltpu.VMEM((nseq, DN_HEADS, DN_HEAD_DIM, DN_HEAD_DIM), F32)],
        compiler_params=_cparams(2),
        name="delta_prompt",
    )(y3, rest3, rest3, s0, alog, dtb, onorm)


def _delta_sample_kernel(qkv_ref, ab_ref, gate_ref, s0_ref, cw_ref, alog_ref, dtb_ref, onorm_ref, *rest,
                         nseq, c, first):
    o_ref, snew_ref = rest[-2:]
    rows = nseq * c
    x = qkv_ref[...]
    y = _conv4(x, cw_ref[...], rows, 0)
    qs, ks, v, logg, beta = _delta_front(y, ab_ref[...], alog_ref[...], dtb_ref[...])
    valid = (lax.broadcasted_iota(jnp.int32, (rows, LANES), 0) % c) >= first
    logg = jnp.where(valid, logg, 0.0)
    beta = jnp.where(valid, beta, 0.0)
    gam = _group_cumsum(logg, c)
    gexc = _group_cumsum(logg, c, reverse=True) - logg
    gam_t = gam.T
    masks = _chunk_masks(rows, c)
    gate = gate_ref[...]
    heads = range(DN_HEADS)
    hsl = [slice(h * DN_HEAD_DIM, (h + 1) * DN_HEAD_DIM) for h in heads]
    local = _delta_local([(qs[h], ks[h], v[:, hsl[h]], beta[:, DN_HEADS + h:DN_HEADS + h + 1], gam[:, h:h + 1],
                           gexc[:, h:h + 1], gam_t[h:h + 1, :]) for h in heads], masks, c)
    pairs = [(b, h) for b in range(nseq) for h in heads]
    rsl = [slice(b * c, (b + 1) * c) for b in range(nseq)]
    ks_ = {(b, h): _bdot(jnp.concatenate([local[h][1][rsl[b]], local[h][3][rsl[b]]], axis=0), s0_ref[b, h])
           for b, h in pairs}
    u = {(b, h): local[h][0][rsl[b]] - ks_[b, h][:c] for b, h in pairs}
    for b, h in pairs:
        last = (b + 1) * c - 1
        snew_ref[b, h] = (s0_ref[b, h] * jnp.exp(gam[last:last + 1, h:h + 1])
                          + _bdot_tn(local[h][4][rsl[b]], u[b, h]))
    for h in heads:
        o = (jnp.concatenate([ks_[b, h][c:] for b in range(nseq)], axis=0)
             + _bdot(local[h][2], jnp.concatenate([u[b, h] for b in range(nseq)], axis=0)))
        o_ref[:, hsl[h]] = _delta_out(o, gate[:, hsl[h]], onorm_ref[...]).astype(o_ref.dtype)


def _delta_sample(ext, s0_all, new_all, cw, alog, dtb, onorm, l, nseq, first):
    c = SAMPLE_CHUNK
    nb = s0_all.shape[1]
    rows = nseq * c
    kern = functools.partial(_delta_sample_kernel, nseq=nseq, c=c, first=first)
    sspec = pl.BlockSpec((None, nseq, DN_HEADS, DN_HEAD_DIM, DN_HEAD_DIM), lambda i: (l, i, 0, 0, 0))
    in_specs = [pl.BlockSpec((rows, QKV_WIDTH), lambda i: (i, 0)),
                pl.BlockSpec((rows, LANES), lambda i: (i, COL_AB // LANES)),
                pl.BlockSpec((rows, DN_WIDTH), lambda i: (i, COL_GATE // DN_WIDTH)),
                sspec, _layer(cw, l), _layer(alog, l), _layer(dtb, l), _layer(onorm, l)]
    args = [ext, ext, ext, s0_all, cw, alog, dtb, onorm]
    aliases = {}
    if new_all is not None:
        in_specs.append(pl.BlockSpec(memory_space=pl.ANY))
        args.append(new_all)
        aliases = {len(args) - 1: 1}
    return pl.pallas_call(
        kern,
        grid=(nb // nseq,),
        in_specs=in_specs,
        out_specs=[pl.BlockSpec((rows, DN_WIDTH), lambda i: (i, 0)), sspec],
        out_shape=[jax.ShapeDtypeStruct((nb * c, DN_WIDTH), BF16),
                   jax.ShapeDtypeStruct(s0_all.shape, F32)],
        input_output_aliases=aliases,
        compiler_params=_cparams(1),
        name="delta_sample",
    )(*args)


def _ssm_prep_kernel(are_ref, aim_ref, dt_ref, bre_ref, bim_ref, lre_ref, lim_ref, bmat_ref):
    ar, ai, dt = are_ref[0], aim_ref[0], jnp.exp(dt_ref[0])
    mag = jnp.exp(ar * dt)
    lr = mag * jnp.cos(ai * dt)
    li = mag * jnp.sin(ai * dt)
    lre_ref[0] = lr
    lim_ref[0] = li
    den = ar * ar + ai * ai
    fr = ((lr - 1.0) * ar + li * ai) / den
    fi = (li * ar - (lr - 1.0) * ai) / den
    br, bi = bre_ref[0], bim_ref[0]
    bbr = fr * br - fi * bi
    bbi = fr * bi + fi * br
    lane_group = lax.broadcasted_iota(jnp.int32, (SSM_GROUP, SSM_NS), 1) // SSM_STATE
    for g in range(SSM_GROUPS):
        m = lane_group == g
        bmat_ref[0, g * SSM_GROUP:(g + 1) * SSM_GROUP, 0:SSM_NS] = jnp.where(m, bbr, 0.0)
        bmat_ref[0, g * SSM_GROUP:(g + 1) * SSM_GROUP, SSM_NS:] = jnp.where(m, bbi, 0.0)


def _ssm_prep(are, aim, dt, bre, bim):
    depth = are.shape[0]
    vec = pl.BlockSpec((1, 1, SSM_NS), lambda l: (l, 0, 0))
    mat = pl.BlockSpec((1, SSM_GROUP, SSM_NS), lambda l: (l, 0, 0))
    return pl.pallas_call(
        _ssm_prep_kernel,
        grid=(depth,),
        in_specs=[vec, vec, vec, mat, mat],
        out_specs=[vec, vec, pl.BlockSpec((1, SSM_WIDTH, 2 * SSM_NS), lambda l: (l, 0, 0))],
        out_shape=[jax.ShapeDtypeStruct((depth, 1, SSM_NS), F32), jax.ShapeDtypeStruct((depth, 1, SSM_NS), F32),
                   jax.ShapeDtypeStruct((depth, SSM_WIDTH, 2 * SSM_NS), F32)],
        compiler_params=_cparams(1),
        name="ssm_prep",
    )(are, aim, dt, bre, bim)


def _gelu_tanh(x):
    return 0.5 * x * (1.0 + jnp.tanh(math.sqrt(2.0 / math.pi) * (x + 0.044715 * (x * x * x))))


def _ssm_kernel(ulo_ref, uhi_ref, h0_ref, bmat_ref, lre_ref, lim_ref, cre_ref, cim_ref, d_ref, gw_ref, gb_ref,
                y_ref, hl_ref, buf, h_scr, tm_scr, bm_scr, *, nb, nt, exact_in):
    i = pl.program_id(0)
    halves = range(SSM_WIDTH // LANES)
    lanes = [slice(s * LANES, (s + 1) * LANES) for s in halves]

    @pl.when(i == 0)
    def _():
        h_scr[...] = h0_ref[...]

    for s, u_ref in enumerate((ulo_ref, uhi_ref)):
        if nb <= nt:
            for b in range(nb):
                tm_scr[s, pl.ds(b, nt, stride=nb), :] = u_ref[b]
        else:
            for t in range(nt):
                tm_scr[s, t * nb:(t + 1) * nb, :] = u_ref[pl.ds(t, nb, stride=nt), :]
    u = jnp.concatenate([tm_scr[s] for s in halves], axis=1)
    buf[...] = _hdot(u, bmat_ref[...]) if exact_in else _bdot(u, bmat_ref[...])
    lr = jnp.broadcast_to(lre_ref[...], (SUBLANES, SSM_NS))
    li = jnp.broadcast_to(lim_ref[...], (SUBLANES, SSM_NS))
    for g in range(nb // SUBLANES):
        def step(t, h, g=g):
            r0 = pl.multiple_of(t * nb + g * SUBLANES, SUBLANES)
            bu = buf[pl.ds(r0, SUBLANES), :]
            hre, him = h[:, :SSM_NS], h[:, SSM_NS:]
            h = jnp.concatenate([lr * hre - li * him + bu[:, :SSM_NS],
                                 lr * him + li * hre + bu[:, SSM_NS:]], axis=1)
            buf[pl.ds(r0, SUBLANES), :] = h
            return h

        gs = slice(g * SUBLANES, (g + 1) * SUBLANES)
        h_fin = lax.fori_loop(0, nt, step, h_scr[gs, :])
        h_scr[gs, :] = h_fin
    hl_ref[...] = h_scr[...]
    y = _bdot(buf[:, :SSM_NS], cre_ref[...]) - _bdot(buf[:, SSM_NS:], cim_ref[...]) + d_ref[...] * u
    y = _gelu_tanh(y)
    y = y * jax.nn.sigmoid(_bdot(y, gw_ref[...]) + gb_ref[...])
    for s in halves:
        tm_scr[s] = y[:, lanes[s]]
        if nb <= nt:
            for b in range(nb):
                y_ref[b, :, lanes[s]] = tm_scr[s, pl.ds(b, nt, stride=nb), :].astype(y_ref.dtype)
        else:
            for t in range(nt):
                bm_scr[s, pl.ds(t, nb, stride=nt), :] = tm_scr[s, t * nb:(t + 1) * nb, :]
            y_ref[:, lanes[s]] = bm_scr[s].astype(y_ref.dtype)


def _ssm(proj, col, h0, bmat, lre, lim, cre, cim, dskip, gw, gb, l, nt):
    nb = h0.shape[0]
    rows = nt * nb
    lo = col // LANES
    if proj.ndim == 3:
        t = proj.shape[1]
        grid = (t // nt,)
        u_spec = lambda c: pl.BlockSpec((nb, nt, LANES), lambda i: (0, i, c))
        y_spec = pl.BlockSpec((nb, nt, SSM_WIDTH), lambda i: (0, i, 0))
        y_shape = (nb, t, SSM_WIDTH)
    else:
        grid = (1,)
        u_spec = lambda c: pl.BlockSpec((rows, LANES), lambda i: (0, c))
        y_spec = pl.BlockSpec((rows, SSM_WIDTH), lambda i: (0, 0))
        y_shape = (rows, SSM_WIDTH)
    kern = functools.partial(_ssm_kernel, nb=nb, nt=nt, exact_in=proj.ndim == 2)
    const = lambda a: pl.BlockSpec(a.shape, lambda i: (0,) * a.ndim)
    return pl.pallas_call(
        kern,
        grid=grid,
        in_specs=[u_spec(lo), u_spec(lo + 1), const(h0), _layer(bmat, l), _layer(lre, l), _layer(lim, l),
                  _layer(cre, l), _layer(cim, l), _layer(dskip, l), _layer(gw, l), _layer(gb, l)],
        out_specs=[y_spec, const(h0)],
        out_shape=[jax.ShapeDtypeStruct(y_shape, BF16), jax.ShapeDtypeStruct(h0.shape, F32)],
        scratch_shapes=[pltpu.VMEM((rows, 2 * SSM_NS), F32), pltpu.VMEM(h0.shape, F32),
                        pltpu.VMEM((SSM_WIDTH // LANES, rows, LANES), F32),
                        pltpu.VMEM((SSM_WIDTH // LANES, rows, LANES), F32)],
        compiler_params=_cparams(1),
        name="ssm",
    )(proj, proj, h0, bmat, lre, lim, cre, cim, dskip, gw, gb)


def _pool_windows(xfull):
    s2 = xfull + pltpu.roll(xfull, 1, 0)
    s4 = s2 + pltpu.roll(s2, 2, 0)
    s8 = s4 + pltpu.roll(s4, 4, 0)
    s16 = s8 + pltpu.roll(s8, 8, 0)
    return s2, s4, s8, s16


def _pool_mix(sums, x, pos, w_ref, scale_ref):
    lane = lax.broadcasted_iota(jnp.int32, (1, POOL_WIDTH), 1) // POOL_GROUP
    win = None
    for gidx in reversed(range(len(POOL_WINDOWS))):
        cnt = jnp.minimum(pos + 1, POOL_WINDOWS[gidx]).astype(F32)
        term = sums[gidx] / cnt
        win = term if win is None else jnp.where(lane == gidx, term, win)
    r = win - x
    return _bdot(r, w_ref[...]) * scale_ref[...]


def _pool_prompt_kernel(u_ref, st_ref, w_ref, scale_ref, y_ref, tail, *, tb, pos0):
    i = pl.program_id(1)
    halo = 2 * SUBLANES

    @pl.when(i == 0)
    def _():
        tail[...] = st_ref[0]

    x = u_ref[0]
    xfull = jnp.concatenate([tail[...], x], axis=0)
    tail[...] = x[tb - halo:, :]
    sums = [s[halo:] for s in _pool_windows(xfull)]
    pos = pos0 + i * tb + lax.broadcasted_iota(jnp.int32, (tb, 1), 0)
    y_ref[0] = _pool_mix(sums, x, pos, w_ref, scale_ref).astype(y_ref.dtype)


def _pool_prompt(p3, col, st, wbd, scale, l, tb, pos0):
    nb, t, _ = p3.shape
    kern = functools.partial(_pool_prompt_kernel, tb=tb, pos0=pos0)
    return pl.pallas_call(
        kern,
        grid=(nb, t // tb),
        in_specs=[pl.BlockSpec((1, tb, POOL_WIDTH), lambda b, i: (b, i, col // POOL_WIDTH)),
                  pl.BlockSpec((1, 2 * SUBLANES, POOL_WIDTH), lambda b, i: (b, 0, 0)),
                  _layer(wbd, l), _layer(scale, l)],
        out_specs=pl.BlockSpec((1, tb, POOL_WIDTH), lambda b, i: (b, i, 0)),
        out_shape=jax.ShapeDtypeStruct((nb, t, POOL_WIDTH), BF16),
        scratch_shapes=[pltpu.VMEM((2 * SUBLANES, POOL_WIDTH), F32)],
        compiler_params=_cparams(2),
        name="pool_prompt",
    )(p3, st, wbd, scale)


def _pool_sample_kernel(x_ref, w_ref, scale_ref, y_ref, *, group, first, pos0):
    x = x_ref[...]
    rows = x.shape[0]
    sums = _pool_windows(x)
    pos = pos0 + (lax.broadcasted_iota(jnp.int32, (rows, 1), 0) % group) - first
    y_ref[...] = _pool_mix(sums, x, jnp.maximum(pos, 0), w_ref, scale_ref).astype(y_ref.dtype)


def _pool_sample(ext, wbd, scale, l, group, first, pos0):
    kern = functools.partial(_pool_sample_kernel, group=group, first=first, pos0=pos0)
    const = lambda a: pl.BlockSpec(a.shape, lambda i: (0,) * a.ndim)
    return pl.pallas_call(
        kern,
        grid=(1,),
        in_specs=[const(ext), _layer(wbd, l), _layer(scale, l)],
        out_specs=const(ext),
        out_shape=jax.ShapeDtypeStruct(ext.shape, BF16),
        compiler_params=_cparams(1),
        name="pool_sample",
    )(ext, wbd, scale)


def _block_diag(blocks):
    g, r, c = blocks.shape
    eye = jnp.eye(g, dtype=blocks.dtype)
    return (eye[:, None, :, None] * blocks[:, :, None, :]).reshape(g * r, g * c)


def kernel(x_prompt, x_sample, state_delta, state_conv, state_ssm_re, state_ssm_im, state_pool, norm_mix_pre, norm_mix_post, norm_ffn_pre, norm_ffn_post, w_in, conv_w, dn_a_log, dn_dt_bias, dn_out_norm, ssm_a_re, ssm_a_im, ssm_log_dt, ssm_b_re, ssm_b_im, ssm_c_re, ssm_c_im, ssm_d, ssm_glu_w, ssm_glu_b, pool_w, pool_scale, w_out, ffn_w_gate, ffn_w_up, ffn_w_down):
    depth = w_in.shape[0]
    bp, tp, _ = x_prompt.shape
    bs, ts, _ = x_sample.shape

    w_qkv = w_in[:, :, :QKV_WIDTH].astype(BF16)
    w_rest = w_in[:, :, _OFF_G:].astype(BF16)
    w_ab = jnp.pad(w_in[:, :, _OFF_A:_OFF_G], ((0, 0), (0, 0), (0, LANES - 2 * DN_HEADS))).astype(BF16)
    w_out_b = w_out.astype(BF16)
    wg_b, wu_b, wd_b = ffn_w_gate.astype(BF16), ffn_w_up.astype(BF16), ffn_w_down.astype(BF16)
    row = lambda a: a.reshape(depth, 1, -1)
    nmp, nmo, nfp, nfo = row(norm_mix_pre), row(norm_mix_post), row(norm_ffn_pre), row(norm_ffn_post)
    alog = jnp.pad(dn_a_log, ((0, 0), (0, LANES - DN_HEADS))).reshape(depth, 1, LANES)
    dtb = jnp.pad(dn_dt_bias, ((0, 0), (0, LANES - DN_HEADS))).reshape(depth, 1, LANES)
    onorm = row(dn_out_norm)
    dt_full = jnp.repeat(ssm_log_dt, SSM_STATE, axis=1).reshape(depth, 1, SSM_NS)
    b_t = lambda b: jnp.transpose(b, (0, 3, 1, 2)).reshape(depth, SSM_GROUP, SSM_NS)
    lam_re, lam_im, bmat = _ssm_prep(ssm_a_re.reshape(depth, 1, SSM_NS), ssm_a_im.reshape(depth, 1, SSM_NS),
                                     dt_full, b_t(ssm_b_re), b_t(ssm_b_im))
    c_bd = lambda cc: jax.vmap(_block_diag)(jnp.transpose(cc, (0, 1, 3, 2))).astype(BF16)
    cre, cim = c_bd(ssm_c_re), c_bd(ssm_c_im)
    dskip, glu_b = row(ssm_d), row(ssm_glu_b)
    glu_w = ssm_glu_w.astype(BF16)
    pool_bd = jax.vmap(_block_diag)(pool_w).astype(BF16)
    pscale = row(pool_scale)

    xp = x_prompt.reshape(bp * tp, D_MODEL)
    xs = x_sample.reshape(bs * ts, D_MODEL)
    zero_conv = jnp.zeros((bp, SUBLANES, QKV_WIDTH), F32)
    zero_delta = jnp.zeros((bp, DN_HEADS, DN_HEAD_DIM, DN_HEAD_DIM), F32)
    zero_h = jnp.zeros((bp, 2 * SSM_NS), F32)
    zero_pool = jnp.zeros((bp, 2 * SUBLANES, POOL_WIDTH), F32)
    pad_rows = SAMPLE_CHUNK - ts - (DN_CONV - 1)
    pool_group = 24
    pool_first = 1 + POOL_BUF
    h0_s = jnp.concatenate([state_ssm_re.reshape(depth, bs, SSM_NS), state_ssm_im.reshape(depth, bs, SSM_NS)],
                           axis=2)

    outs_p, outs_s = [], []
    delta_s = None
    for l in range(depth):
        y, rest, craw = _in_proj_conv(xp, nmp, w_qkv, w_rest, w_ab, zero_conv, conv_w, l, 512, tp)
        y3 = y.reshape(bp, tp, QKV_WIDTH)
        r3 = rest.reshape(bp, tp, REST_WIDTH)
        o_dn, delta_new = _delta_prompt(y3, r3, zero_delta, alog, dtb, onorm, l, 2, 256)
        o_ssm, h_fin = _ssm(r3, REST_SSM, zero_h, bmat, lam_re, lam_im, cre, cim, dskip, glu_w, glu_b, l, 128)
        o_pool = _pool_prompt(r3, REST_POOL, zero_pool, pool_bd, pscale, l, 512, 0)
        xp = _post_mix(o_dn.reshape(bp * tp, DN_WIDTH), o_ssm.reshape(bp * tp, SSM_WIDTH),
                       o_pool.reshape(bp * tp, POOL_WIDTH), xp,
                       w_out_b, nmo, nfp, wg_b, wu_b, wd_b, nfo, l, 512)
        outs_p.append((delta_new, craw[:, SUBLANES - (DN_CONV - 1):, :],
                       h_fin[:, :SSM_NS].reshape(bp, SSM_GROUPS, SSM_STATE),
                       h_fin[:, SSM_NS:].reshape(bp, SSM_GROUPS, SSM_STATE),
                       r3[:, tp - POOL_BUF:, REST_POOL:REST_AB]))

        proj = _in_proj(xs, nmp, w_qkv, w_rest, w_ab, l, 256)
        s3 = proj.reshape(bs, ts, PROJ_WIDTH)
        head = jnp.concatenate([jnp.zeros((bs, pad_rows, QKV_WIDTH), F32), state_conv[l]], axis=1)
        head = jnp.pad(head, ((0, 0), (0, 0), (0, PROJ_WIDTH - QKV_WIDTH)))
        ext = jnp.concatenate([head, s3], axis=1).reshape(bs * SAMPLE_CHUNK, PROJ_WIDTH)
        o_ext, delta_s = _delta_sample(ext, state_delta, delta_s, conv_w, alog, dtb, onorm, l, 16,
                                       SAMPLE_CHUNK - ts)
        o_dn = o_ext.reshape(bs, SAMPLE_CHUNK, DN_WIDTH)[:, SAMPLE_CHUNK - ts:].reshape(bs * ts, DN_WIDTH)
        o_ssm, h_fin = _ssm(proj, COL_SSM, h0_s[l], bmat, lam_re, lam_im, cre, cim, dskip, glu_w, glu_b, l, ts)
        pool_u = s3[:, :, COL_POOL:COL_AB]
        pext = jnp.concatenate([jnp.zeros((bs, 1, POOL_WIDTH), F32), state_pool[l], pool_u,
                                jnp.zeros((bs, pool_group - pool_first - ts, POOL_WIDTH), F32)], axis=1)
        y_ext = _pool_sample(pext.reshape(bs * pool_group, POOL_WIDTH), pool_bd, pscale, l, pool_group,
                             pool_first, PAST_LEN)
        o_pool = y_ext.reshape(bs, pool_group, POOL_WIDTH)[:, pool_first:pool_first + ts].reshape(bs * ts, POOL_WIDTH)
        xs = _post_mix(o_dn, o_ssm, o_pool, xs, w_out_b, nmo, nfp, wg_b, wu_b, wd_b, nfo, l, 256)
        outs_s.append((s3[:, ts - (DN_CONV - 1):, :QKV_WIDTH],
                       h_fin[:, :SSM_NS].reshape(bs, SSM_GROUPS, SSM_STATE),
                       h_fin[:, SSM_NS:].reshape(bs, SSM_GROUPS, SSM_STATE),
                       jnp.concatenate([state_pool[l][:, ts:], pool_u], axis=1)))

    stack = lambda outs, k: jnp.stack([o[k] for o in outs])
    return (xp.reshape(bp, tp, D_MODEL), xs.reshape(bs, ts, D_MODEL),
            stack(outs_p, 0), stack(outs_p, 1), stack(outs_p, 2), stack(outs_p, 3), stack(outs_p, 4),
            delta_s, stack(outs_s, 0), stack(outs_s, 1), stack(outs_s, 2), stack(outs_s, 3))
```

```python
import functools
import math

import jax
import jax.numpy as jnp
from jax import lax
from jax.experimental import pallas as pl
from jax.experimental.pallas import tpu as pltpu

F32 = jnp.float32
BF16 = jnp.bfloat16
HIGHEST = lax.Precision.HIGHEST

D_MODEL = 1024
DN_HEADS = 4
DN_HEAD_DIM = 128
DN_WIDTH = DN_HEADS * DN_HEAD_DIM
DN_CONV = 4
QKV_WIDTH = 3 * DN_WIDTH
SSM_WIDTH = 256
SSM_GROUP = 16
SSM_GROUPS = 16
SSM_STATE = 64
SSM_NS = SSM_GROUPS * SSM_STATE
POOL_WIDTH = 256
POOL_WINDOWS = (2, 4, 8, 16)
POOL_GROUP = 64
POOL_BUF = 15
D_FF = 2816
EPS = 1e-6
PAST_LEN = 16384

LANES = 128
SUBLANES = 8

COL_GATE = QKV_WIDTH
COL_SSM = COL_GATE + DN_WIDTH
COL_POOL = COL_SSM + SSM_WIDTH
COL_AB = COL_POOL + POOL_WIDTH
PROJ_WIDTH = COL_AB + LANES
REST_GATE = 0
REST_SSM = COL_SSM - COL_GATE
REST_POOL = COL_POOL - COL_GATE
REST_AB = COL_AB - COL_GATE
REST_WIDTH = PROJ_WIDTH - COL_GATE

_OFF_A = QKV_WIDTH
_OFF_G = _OFF_A + 2 * DN_HEADS

VMEM_LIMIT_BYTES = 56 * 1024 * 1024

DELTA_CHUNK = 64
SAMPLE_CHUNK = 8
DELTA_SUB = 16
POST_MIX_PARTS = 2
SSM_PARTS = 4


def _cparams(n_axes):
    return pltpu.CompilerParams(dimension_semantics=("arbitrary",) * n_axes,
                                vmem_limit_bytes=VMEM_LIMIT_BYTES)


def _layer(a, l, single=False):
    zeros = (0,) * (a.ndim - 1)
    mode = dict(pipeline_mode=pl.Buffered(1)) if single else {}
    return pl.BlockSpec((None,) + a.shape[1:], lambda *_: (l,) + zeros, **mode)


def _mm(a, b):
    return jnp.dot(a, b, preferred_element_type=F32)


def _bdot(a, b):
    return jnp.dot(a.astype(BF16), b.astype(BF16), preferred_element_type=F32)


def _bdot_nt(a, b):
    return lax.dot_general(a.astype(BF16), b.astype(BF16), (((1,), (1,)), ((), ())),
                           preferred_element_type=F32)


def _bdot_tn(a, b):
    return lax.dot_general(a.astype(BF16), b.astype(BF16), (((0,), (0,)), ((), ())),
                           preferred_element_type=F32)


def _hdot(a, b):
    return jnp.dot(a, b, precision=HIGHEST, preferred_element_type=F32)


def _rms(x, w):
    return x * lax.rsqrt(jnp.mean(x * x, axis=-1, keepdims=True) + EPS) * w


def _silu(x):
    return x * jax.nn.sigmoid(x)


def _in_proj_kernel(x_ref, nw_ref, wqkv_ref, wrest_ref, wab_ref, o_ref):
    h = _rms(x_ref[...], nw_ref[...]).astype(BF16)
    o_ref[:, :QKV_WIDTH] = _mm(h, wqkv_ref[...])
    o_ref[:, COL_GATE:COL_AB] = _mm(h, wrest_ref[...])
    o_ref[:, COL_AB:] = _mm(h, wab_ref[...])


def _in_proj(x, nw, wqkv, wrest, wab, l, tm):
    m = x.shape[0]
    return pl.pallas_call(
        _in_proj_kernel,
        grid=(m // tm,),
        in_specs=[pl.BlockSpec((tm, D_MODEL), lambda i: (i, 0)), _layer(nw, l),
                  _layer(wqkv, l, True), _layer(wrest, l, True), _layer(wab, l, True)],
        out_specs=pl.BlockSpec((tm, PROJ_WIDTH), lambda i: (i, 0)),
        out_shape=jax.ShapeDtypeStruct((m, PROJ_WIDTH), F32),
        compiler_params=_cparams(1),
        name="in_proj",
    )(x, nw, wqkv, wrest, wab)


def _in_proj_conv_kernel(x_ref, nw_ref, wqkv_ref, wrest_ref, wab_ref, cst_ref, cw_ref,
                         y_ref, o_ref, craw_ref, tail, *, tm, nt, nc):
    i = pl.program_id(0)

    @pl.when(i % nt == 0)
    def _():
        tail[...] = cst_ref[0]

    h = _rms(x_ref[...], nw_ref[...]).astype(BF16)
    for cc in range(QKV_WIDTH // nc):
        cs = slice(cc * nc, (cc + 1) * nc)
        r = _mm(h, wqkv_ref[:, cs])
        xfull = jnp.concatenate([tail[:, cs], r], axis=0)
        tail[:, cs] = r[tm - SUBLANES:, :]
        y = _silu(_conv4(xfull, cw_ref[:, cs], tm, SUBLANES))
        if cc * nc < 2 * DN_WIDTH:
            scale = DN_HEAD_DIM ** -0.5 if cc * nc < DN_WIDTH else 1.0
            parts = []
            for j in range(nc // DN_HEAD_DIM):
                yh = y[:, j * DN_HEAD_DIM:(j + 1) * DN_HEAD_DIM]
                parts.append(yh * lax.rsqrt(jnp.sum(yh * yh, axis=-1, keepdims=True) + EPS) * scale)
            y = jnp.concatenate(parts, axis=1)
        y_ref[:, cs] = y
    o_ref[:, :REST_AB] = _mm(h, wrest_ref[...])
    o_ref[:, REST_AB:] = _mm(h, wab_ref[...])
    craw_ref[0] = tail[...]


def _in_proj_conv(x, nw, wqkv, wrest, wab, cst, cw, l, tm, seq_len):
    m = x.shape[0]
    nt = seq_len // tm
    kern = functools.partial(_in_proj_conv_kernel, tm=tm, nt=nt, nc=2 * DN_HEAD_DIM)
    tail_spec = pl.BlockSpec((1, SUBLANES, QKV_WIDTH), lambda i: (i // nt, 0, 0))
    return pl.pallas_call(
        kern,
        grid=(m // tm,),
        in_specs=[pl.BlockSpec((tm, D_MODEL), lambda i: (i, 0)), _layer(nw, l),
                  _layer(wqkv, l, True), _layer(wrest, l, True), _layer(wab, l, True), tail_spec, _layer(cw, l)],
        out_specs=[pl.BlockSpec((tm, QKV_WIDTH), lambda i: (i, 0)),
                   pl.BlockSpec((tm, REST_WIDTH), lambda i: (i, 0)), tail_spec],
        out_shape=[jax.ShapeDtypeStruct((m, QKV_WIDTH), F32), jax.ShapeDtypeStruct((m, REST_WIDTH), F32),
                   jax.ShapeDtypeStruct((m // seq_len, SUBLANES, QKV_WIDTH), F32)],
        scratch_shapes=[pltpu.VMEM((SUBLANES, QKV_WIDTH), F32)],
        compiler_params=_cparams(1),
        name="in_proj_conv",
    )(x, nw, wqkv, wrest, wab, cst, cw)


def _post_mix_kernel(odn_ref, ossm_ref, opool_ref, x_ref, wo_ref, nmp_ref, nfp_ref, wg_ref, wu_ref, wd_ref,
                     nfo_ref, o_ref):
    tm = x_ref.shape[0]
    parts = [slice(p * tm // POST_MIX_PARTS, (p + 1) * tm // POST_MIX_PARTS) for p in range(POST_MIX_PARTS)]
    mix = [(_bdot(odn_ref[rs, :], wo_ref[0:DN_WIDTH, :])
            + _bdot(ossm_ref[rs, :], wo_ref[DN_WIDTH:DN_WIDTH + SSM_WIDTH, :])
            + _bdot(opool_ref[rs, :], wo_ref[DN_WIDTH + SSM_WIDTH:, :])) for rs in parts]
    x1 = [x_ref[rs, :] + _rms(m, nmp_ref[...]) for rs, m in zip(parts, mix)]
    h = [_rms(x, nfp_ref[...]).astype(BF16) for x in x1]
    g = [_mm(y, wg_ref[...]) for y in h]
    u = [_mm(y, wu_ref[...]) for y in h]
    f = [_bdot(_silu(a) * b, wd_ref[...]) for a, b in zip(g, u)]
    for rs, x, y in zip(parts, x1, f):
        o_ref[rs, :] = x + _rms(y, nfo_ref[...])


def _post_mix(odn, ossm, opool, x, wo, nmp, nfp, wg, wu, wd, nfo, l, tm):
    m = x.shape[0]
    row = lambda w: pl.BlockSpec((tm, w), lambda i: (i, 0))
    return pl.pallas_call(
        _post_mix_kernel,
        grid=(m // tm,),
        in_specs=[row(DN_WIDTH), row(SSM_WIDTH), row(POOL_WIDTH), row(D_MODEL),
                  _layer(wo, l, True), _layer(nmp, l), _layer(nfp, l), _layer(wg, l, True), _layer(wu, l, True),
                  _layer(wd, l, True), _layer(nfo, l)],
        out_specs=row(D_MODEL),
        out_shape=jax.ShapeDtypeStruct((m, D_MODEL), F32),
        compiler_params=_cparams(1),
        name="post_mix",
    )(odn, ossm, opool, x, wo, nmp, nfp, wg, wu, wd, nfo)


def _split2(x):
    hi = x.astype(BF16)
    return hi, (x - hi.astype(F32)).astype(BF16)


def _group_cumsum(x, group, reverse=False):
    rows = x.shape[0]
    pos = lax.broadcasted_iota(jnp.int32, x.shape, 0) % group
    d = 1
    while d < group:
        if reverse:
            x = x + jnp.where(pos + d < group, pltpu.roll(x, rows - d, 0), 0.0)
        else:
            x = x + jnp.where(pos >= d, pltpu.roll(x, d, 0), 0.0)
        d *= 2
    return x


def _chunk_masks(rows, blk):
    r = lax.broadcasted_iota(jnp.int32, (rows, rows), 0)
    s = lax.broadcasted_iota(jnp.int32, (rows, rows), 1)
    d = r - s
    if blk < rows:
        d = jnp.where(r // blk == s // blk, d, -1)
    sub = None
    if blk > DELTA_SUB:
        sub = r // DELTA_SUB == s // DELTA_SUB
    return d >= 0, d > 0, (r == s).astype(F32), sub


def _neumann(a, eye, index):
    t = [eye - x for x in a]
    p = a
    n = 2
    while n < index:
        p = [_bdot(x, x) for x in p]
        t = [x + _bdot(x, y) for x, y in zip(t, p)]
        n *= 2
    return t


def _unit_lower_inverse(a, eye, sub, index):
    if sub is None:
        return _neumann(a, eye, index)
    assert index <= 4 * DELTA_SUB
    d = [jnp.where(sub, x, 0.0) for x in a]
    td = _neumann(d, eye, DELTA_SUB)
    n = [_bdot(t, x - y) for t, x, y in zip(td, a, d)]
    n2 = [_bdot(x, x) for x in n]
    m = [_bdot(eye - x, eye + y) for x, y in zip(n, n2)]
    return [_bdot(x, t) for x, t in zip(m, td)]


def _delta_local(probs, masks, index):
    incl, strict, eye, sub = masks
    rows = probs[0][0].shape[0]
    kb = [k * beta for (_, k, _, beta, _, _, _) in probs]
    kq = [_bdot_nt(jnp.concatenate([x, q], axis=0), k) for x, (q, k, _, _, _, _, _) in zip(kb, probs)]
    decay = [jnp.where(incl, jnp.exp(jnp.where(incl, gam - grow, 0.0)), 0.0) for (_, _, _, _, gam, _, grow) in probs]
    a = [jnp.where(strict, x[:rows] * d, 0.0) for x, d in zip(kq, decay)]
    qk = [x[rows:] * d for x, d in zip(kq, decay)]
    egam = [jnp.exp(p[4]) for p in probs]
    rhs = [jnp.concatenate([p[2] * p[3], x * e], axis=1) for p, x, e in zip(probs, kb, egam)]
    t = _unit_lower_inverse(a, eye, sub, index)
    sol = [_bdot(x, r) for x, r in zip(t, rhs)]
    a_sp = [_split2(x) for x in a]
    s_sp = [_split2(x) for x in sol]
    asol = [_mm(ah, sh) + _mm(ah, sl) + _mm(al, sh) for (ah, al), (sh, sl) in zip(a_sp, s_sp)]
    resid = [r - s - x for r, s, x in zip(rhs, sol, asol)]
    sol = [s + _bdot(x, r) for s, x, r in zip(sol, t, resid)]
    return [(s[:, :DN_HEAD_DIM], s[:, DN_HEAD_DIM:], x, p[0] * e, p[1] * jnp.exp(p[5]))
            for s, x, p, e in zip(sol, qk, probs, egam)]


def _delta_front(y, ab, alog, dtb):
    y = _silu(y)
    qs, ks = [], []
    for h in range(DN_HEADS):
        qh = y[:, h * DN_HEAD_DIM:(h + 1) * DN_HEAD_DIM]
        kh = y[:, DN_WIDTH + h * DN_HEAD_DIM:DN_WIDTH + (h + 1) * DN_HEAD_DIM]
        qs.append(qh * lax.rsqrt(jnp.sum(qh * qh, axis=-1, keepdims=True) + EPS) * (DN_HEAD_DIM ** -0.5))
        ks.append(kh * lax.rsqrt(jnp.sum(kh * kh, axis=-1, keepdims=True) + EPS))
    v = y[:, 2 * DN_WIDTH:]
    logg, beta = _delta_gates(ab, alog, dtb)
    return qs, ks, v, logg, beta


def _delta_gates(ab, alog, dtb):
    z = ab + dtb
    softplus = jnp.maximum(z, 0.0) + jnp.log1p(jnp.exp(-jnp.abs(z)))
    return -jnp.exp(alog) * softplus, jax.nn.sigmoid(ab)


def _delta_out(o, gate, onorm):
    return o * lax.rsqrt(jnp.mean(o * o, axis=-1, keepdims=True) + EPS) * onorm * _silu(gate)


def _conv4(xfull, w, rows, off):
    z = xfull * w[0:1]
    z = xfull * w[1:2] + pltpu.roll(z, 1, 0)
    z = xfull * w[2:3] + pltpu.roll(z, 1, 0)
    return (xfull * w[3:4] + pltpu.roll(z, 1, 0))[off:off + rows]


def _delta_prompt_kernel(qkv_ref, ab_ref, gate_ref, s0_ref, alog_ref, dtb_ref, onorm_ref,
                         o_ref, sfin_ref, s_scr, *, nseq, tb, c):
    i = pl.program_id(1)

    @pl.when(i == 0)
    def _():
        s_scr[...] = s0_ref[...]

    masks = _chunk_masks(c, c)
    nchunk = tb // c
    heads = range(DN_HEADS)
    seqs = range(nseq)
    hsl = [slice(h * DN_HEAD_DIM, (h + 1) * DN_HEAD_DIM) for h in heads]
    probs, gams = [], []
    for r in seqs:
        qs = [qkv_ref[r, :, hsl[h]] for h in heads]
        ks = [qkv_ref[r, :, DN_WIDTH + h * DN_HEAD_DIM:DN_WIDTH + (h + 1) * DN_HEAD_DIM] for h in heads]
        v = qkv_ref[r, :, 2 * DN_WIDTH:]
        logg, beta = _delta_gates(ab_ref[r], alog_ref[...], dtb_ref[...])
        gam = _group_cumsum(logg, c)
        gexc = _group_cumsum(logg, c, reverse=True) - logg
        gam_t = gam.T
        gams.append(gam)
        for j in range(nchunk):
            rs = slice(j * c, (j + 1) * c)
            for h in heads:
                probs.append((qs[h][rs], ks[h][rs], v[rs, hsl[h]], beta[rs, DN_HEADS + h:DN_HEADS + h + 1],
                              gam[rs, h:h + 1], gexc[rs, h:h + 1], gam_t[h:h + 1, rs]))
    local = _delta_local(probs, masks, c)
    chains = [(r, h) for r in seqs for h in heads]
    s = {ch: s_scr[ch[0], ch[1]] for ch in chains}
    outs = {ch: [] for ch in chains}
    for j in range(nchunk):
        loc = {(r, h): local[(r * nchunk + j) * DN_HEADS + h] for r, h in chains}
        ks_ = {ch: _bdot(jnp.concatenate([loc[ch][1], loc[ch][3]], axis=0), s[ch]) for ch in chains}
        u = {ch: loc[ch][0] - ks_[ch][:c] for ch in chains}
        for ch in chains:
            outs[ch].append(ks_[ch][c:] + _bdot(loc[ch][2], u[ch]))
        last = (j + 1) * c - 1
        s = {(r, h): s[r, h] * jnp.exp(gams[r][last:last + 1, h:h + 1]) + _bdot_tn(loc[r, h][4], u[r, h])
             for r, h in chains}
    for r, h in chains:
        s_scr[r, h] = s[r, h]
        o = _delta_out(jnp.concatenate(outs[r, h], axis=0), gate_ref[r, :, hsl[h]], onorm_ref[...])
        o_ref[r, :, hsl[h]] = o.astype(o_ref.dtype)
    sfin_ref[...] = s_scr[...]


def _delta_prompt(y3, rest3, s0, alog, dtb, onorm, l, nseq, tb):
    nb, t, _ = y3.shape
    kern = functools.partial(_delta_prompt_kernel, nseq=nseq, tb=tb, c=DELTA_CHUNK)
    sspec = pl.BlockSpec((nseq, DN_HEADS, DN_HEAD_DIM, DN_HEAD_DIM), lambda b, i: (b, 0, 0, 0))
    return pl.pallas_call(
        kern,
        grid=(nb // nseq, t // tb),
        in_specs=[pl.BlockSpec((nseq, tb, QKV_WIDTH), lambda b, i: (b, i, 0)),
                  pl.BlockSpec((nseq, tb, LANES), lambda b, i: (b, i, REST_AB // LANES)),
                  pl.BlockSpec((nseq, tb, DN_WIDTH), lambda b, i: (b, i, REST_GATE // DN_WIDTH)),
                  sspec, _layer(alog, l), _layer(dtb, l), _layer(onorm, l)],
        out_specs=[pl.BlockSpec((nseq, tb, DN_WIDTH), lambda b, i: (b, i, 0)), sspec],
        out_shape=[jax.ShapeDtypeStruct((nb, t, DN_WIDTH), BF16),
                   jax.ShapeDtypeStruct((nb, DN_HEADS, DN_HEAD_DIM, DN_HEAD_DIM), F32)],
        scratch_shapes=[pltpu.VMEM((nseq, DN_HEADS, DN_HEAD_DIM, DN_HEAD_DIM), F32)],
        compiler_params=_cparams(2),
        name="delta_prompt",
    )(y3, rest3, rest3, s0, alog, dtb, onorm)


def _delta_sample_kernel(qkv_ref, ab_ref, gate_ref, s0_ref, cw_ref, alog_ref, dtb_ref, onorm_ref, *rest,
                         nseq, c, first):
    o_ref, snew_ref = rest[-2:]
    rows = nseq * c
    x = qkv_ref[...]
    y = _conv4(x, cw_ref[...], rows, 0)
    qs, ks, v, logg, beta = _delta_front(y, ab_ref[...], alog_ref[...], dtb_ref[...])
    valid = (lax.broadcasted_iota(jnp.int32, (rows, LANES), 0) % c) >= first
    logg = jnp.where(valid, logg, 0.0)
    beta = jnp.where(valid, beta, 0.0)
    gam = _group_cumsum(logg, c)
    gexc = _group_cumsum(logg, c, reverse=True) - logg
    gam_t = gam.T
    masks = _chunk_masks(rows, c)
    gate = gate_ref[...]
    heads = range(DN_HEADS)
    hsl = [slice(h * DN_HEAD_DIM, (h + 1) * DN_HEAD_DIM) for h in heads]
    local = _delta_local([(qs[h], ks[h], v[:, hsl[h]], beta[:, DN_HEADS + h:DN_HEADS + h + 1], gam[:, h:h + 1],
                           gexc[:, h:h + 1], gam_t[h:h + 1, :]) for h in heads], masks, c)
    pairs = [(b, h) for b in range(nseq) for h in heads]
    rsl = [slice(b * c, (b + 1) * c) for b in range(nseq)]
    ks_ = {(b, h): _bdot(jnp.concatenate([local[h][1][rsl[b]], local[h][3][rsl[b]]], axis=0), s0_ref[b, h])
           for b, h in pairs}
    u = {(b, h): local[h][0][rsl[b]] - ks_[b, h][:c] for b, h in pairs}
    for b, h in pairs:
        last = (b + 1) * c - 1
        snew_ref[b, h] = (s0_ref[b, h] * jnp.exp(gam[last:last + 1, h:h + 1])
                          + _bdot_tn(local[h][4][rsl[b]], u[b, h]))
    for h in heads:
        o = (jnp.concatenate([ks_[b, h][c:] for b in range(nseq)], axis=0)
             + _bdot(local[h][2], jnp.concatenate([u[b, h] for b in range(nseq)], axis=0)))
        o_ref[:, hsl[h]] = _delta_out(o, gate[:, hsl[h]], onorm_ref[...]).astype(o_ref.dtype)


def _delta_sample(ext, s0_all, new_all, cw, alog, dtb, onorm, l, nseq, first):
    c = SAMPLE_CHUNK
    nb = s0_all.shape[1]
    rows = nseq * c
    kern = functools.partial(_delta_sample_kernel, nseq=nseq, c=c, first=first)
    sspec = pl.BlockSpec((None, nseq, DN_HEADS, DN_HEAD_DIM, DN_HEAD_DIM), lambda i: (l, i, 0, 0, 0))
    in_specs = [pl.BlockSpec((rows, QKV_WIDTH), lambda i: (i, 0)),
                pl.BlockSpec((rows, LANES), lambda i: (i, COL_AB // LANES)),
                pl.BlockSpec((rows, DN_WIDTH), lambda i: (i, COL_GATE // DN_WIDTH)),
                sspec, _layer(cw, l), _layer(alog, l), _layer(dtb, l), _layer(onorm, l)]
    args = [ext, ext, ext, s0_all, cw, alog, dtb, onorm]
    aliases = {}
    if new_all is not None:
        in_specs.append(pl.BlockSpec(memory_space=pl.ANY))
        args.append(new_all)
        aliases = {len(args) - 1: 1}
    return pl.pallas_call(
        kern,
        grid=(nb // nseq,),
        in_specs=in_specs,
        out_specs=[pl.BlockSpec((rows, DN_WIDTH), lambda i: (i, 0)), sspec],
        out_shape=[jax.ShapeDtypeStruct((nb * c, DN_WIDTH), BF16),
                   jax.ShapeDtypeStruct(s0_all.shape, F32)],
        input_output_aliases=aliases,
        compiler_params=_cparams(1),
        name="delta_sample",
    )(*args)


def _ssm_prep_kernel(are_ref, aim_ref, dt_ref, bre_ref, bim_ref, lre_ref, lim_ref, bmat_ref):
    ar, ai, dt = are_ref[0], aim_ref[0], jnp.exp(dt_ref[0])
    mag = jnp.exp(ar * dt)
    lr = mag * jnp.cos(ai * dt)
    li = mag * jnp.sin(ai * dt)
    lre_ref[0] = lr
    lim_ref[0] = li
    den = ar * ar + ai * ai
    fr = ((lr - 1.0) * ar + li * ai) / den
    fi = (li * ar - (lr - 1.0) * ai) / den
    br, bi = bre_ref[0], bim_ref[0]
    bbr = fr * br - fi * bi
    bbi = fr * bi + fi * br
    lane_group = lax.broadcasted_iota(jnp.int32, (SSM_GROUP, SSM_NS), 1) // SSM_STATE
    for g in range(SSM_GROUPS):
        m = lane_group == g
        bmat_ref[0, g * SSM_GROUP:(g + 1) * SSM_GROUP, 0:SSM_NS] = jnp.where(m, bbr, 0.0)
        bmat_ref[0, g * SSM_GROUP:(g + 1) * SSM_GROUP, SSM_NS:] = jnp.where(m, bbi, 0.0)


def _ssm_prep(are, aim, dt, bre, bim):
    depth = are.shape[0]
    vec = pl.BlockSpec((1, 1, SSM_NS), lambda l: (l, 0, 0))
    mat = pl.BlockSpec((1, SSM_GROUP, SSM_NS), lambda l: (l, 0, 0))
    return pl.pallas_call(
        _ssm_prep_kernel,
        grid=(depth,),
        in_specs=[vec, vec, vec, mat, mat],
        out_specs=[vec, vec, pl.BlockSpec((1, SSM_WIDTH, 2 * SSM_NS), lambda l: (l, 0, 0))],
        out_shape=[jax.ShapeDtypeStruct((depth, 1, SSM_NS), F32), jax.ShapeDtypeStruct((depth, 1, SSM_NS), F32),
                   jax.ShapeDtypeStruct((depth, SSM_WIDTH, 2 * SSM_NS), F32)],
        compiler_params=_cparams(1),
        name="ssm_prep",
    )(are, aim, dt, bre, bim)


def _gelu_tanh(x):
    return 0.5 * x * (1.0 + jnp.tanh(math.sqrt(2.0 / math.pi) * (x + 0.044715 * (x * x * x))))


def _ssm_kernel(ulo_ref, uhi_ref, h0_ref, bmat_ref, lre_ref, lim_ref, cre_ref, cim_ref, d_ref, gw_ref, gb_ref,
                y_ref, hl_ref, buf, h_scr, tm_scr, bm_scr, *, nb, nt, nparts, exact_in):
    i = pl.program_id(0)
    halves = range(SSM_WIDTH // LANES)
    lanes = [slice(s * LANES, (s + 1) * LANES) for s in halves]

    @pl.when(i == 0)
    def _():
        h_scr[...] = h0_ref[...]

    for s, u_ref in enumerate((ulo_ref, uhi_ref)):
        if nb <= nt:
            for b in range(nb):
                tm_scr[s, pl.ds(b, nt, stride=nb), :] = u_ref[b]
        else:
            for t in range(nt):
                tm_scr[s, t * nb:(t + 1) * nb, :] = u_ref[pl.ds(t, nb, stride=nt), :]
    u = jnp.concatenate([tm_scr[s] for s in halves], axis=1)
    lr = jnp.broadcast_to(lre_ref[...], (SUBLANES, SSM_NS))
    li = jnp.broadcast_to(lim_ref[...], (SUBLANES, SSM_NS))
    pt = nt // nparts
    prow = [slice(p * pt * nb, (p + 1) * pt * nb) for p in range(nparts)]
    groups = range(nb // SUBLANES)
    h = [h_scr[g * SUBLANES:(g + 1) * SUBLANES, :] for g in groups]

    bmat = bmat_ref[...] if exact_in else bmat_ref[...].astype(BF16)

    def in_map(p):
        up = u[prow[p]]
        buf[prow[p], :] = _hdot(up, bmat) if exact_in else _bdot(up, bmat)

    def recur(p):
        for t in range(p * pt, (p + 1) * pt):
            for g in groups:
                rs = slice(t * nb + g * SUBLANES, t * nb + (g + 1) * SUBLANES)
                bu = buf[rs, :]
                hre, him = h[g][:, :SSM_NS], h[g][:, SSM_NS:]
                h[g] = jnp.concatenate([lr * hre - li * him + bu[:, :SSM_NS],
                                        lr * him + li * hre + bu[:, SSM_NS:]], axis=1)
                buf[rs, :] = h[g]

    def out_map(p):
        rs = prow[p]
        y = _bdot(buf[rs, :SSM_NS], cre_ref[...]) - _bdot(buf[rs, SSM_NS:], cim_ref[...]) + d_ref[...] * u[rs]
        y = _gelu_tanh(y)
        y = y * jax.nn.sigmoid(_bdot(y, gw_ref[...]) + gb_ref[...])
        for s in halves:
            tm_scr[s, rs, :] = y[:, lanes[s]]

    in_map(0)
    for p in range(nparts):
        if p + 1 < nparts:
            in_map(p + 1)
        recur(p)
        if p >= 1:
            out_map(p - 1)
    out_map(nparts - 1)
    for g in groups:
        h_scr[g * SUBLANES:(g + 1) * SUBLANES, :] = h[g]
    hl_ref[...] = h_scr[...]
    for s in halves:
        if nb <= nt:
            for b in range(nb):
                y_ref[b, :, lanes[s]] = tm_scr[s, pl.ds(b, nt, stride=nb), :].astype(y_ref.dtype)
        else:
            for t in range(nt):
                bm_scr[s, pl.ds(t, nb, stride=nt), :] = tm_scr[s, t * nb:(t + 1) * nb, :]
            y_ref[:, lanes[s]] = bm_scr[s].astype(y_ref.dtype)


def _ssm(proj, col, h0, bmat, lre, lim, cre, cim, dskip, gw, gb, l, nt):
    nb = h0.shape[0]
    rows = nt * nb
    lo = col // LANES
    if proj.ndim == 3:
        t = proj.shape[1]
        grid = (t // nt,)
        u_spec = lambda c: pl.BlockSpec((nb, nt, LANES), lambda i: (0, i, c))
        y_spec = pl.BlockSpec((nb, nt, SSM_WIDTH), lambda i: (0, i, 0))
        y_shape = (nb, t, SSM_WIDTH)
    else:
        grid = (1,)
        u_spec = lambda c: pl.BlockSpec((rows, LANES), lambda i: (0, c))
        y_spec = pl.BlockSpec((rows, SSM_WIDTH), lambda i: (0, 0))
        y_shape = (rows, SSM_WIDTH)
    kern = functools.partial(_ssm_kernel, nb=nb, nt=nt, nparts=SSM_PARTS if proj.ndim == 3 else 1,
                             exact_in=proj.ndim == 2)
    const = lambda a: pl.BlockSpec(a.shape, lambda i: (0,) * a.ndim)
    return pl.pallas_call(
        kern,
        grid=grid,
        in_specs=[u_spec(lo), u_spec(lo + 1), const(h0), _layer(bmat, l), _layer(lre, l), _layer(lim, l),
                  _layer(cre, l), _layer(cim, l), _layer(dskip, l), _layer(gw, l), _layer(gb, l)],
        out_specs=[y_spec, const(h0)],
        out_shape=[jax.ShapeDtypeStruct(y_shape, BF16), jax.ShapeDtypeStruct(h0.shape, F32)],
        scratch_shapes=[pltpu.VMEM((rows, 2 * SSM_NS), F32), pltpu.VMEM(h0.shape, F32),
                        pltpu.VMEM((SSM_WIDTH // LANES, rows, LANES), F32),
                        pltpu.VMEM((SSM_WIDTH // LANES, rows, LANES), F32)],
        compiler_params=_cparams(1),
        name="ssm",
    )(proj, proj, h0, bmat, lre, lim, cre, cim, dskip, gw, gb)


def _pool_windows(xfull):
    s2 = xfull + pltpu.roll(xfull, 1, 0)
    s4 = s2 + pltpu.roll(s2, 2, 0)
    s8 = s4 + pltpu.roll(s4, 4, 0)
    s16 = s8 + pltpu.roll(s8, 8, 0)
    return s2, s4, s8, s16


def _pool_mix(sums, x, pos, w_ref, scale_ref):
    lane = lax.broadcasted_iota(jnp.int32, (1, POOL_WIDTH), 1) // POOL_GROUP
    win = None
    for gidx in reversed(range(len(POOL_WINDOWS))):
        cnt = jnp.minimum(pos + 1, POOL_WINDOWS[gidx]).astype(F32)
        term = sums[gidx] / cnt
        win = term if win is None else jnp.where(lane == gidx, term, win)
    r = win - x
    return _bdot(r, w_ref[...]) * scale_ref[...]


def _pool_prompt_kernel(u_ref, st_ref, w_ref, scale_ref, y_ref, tail, *, tb, pos0):
    i = pl.program_id(1)
    halo = 2 * SUBLANES

    @pl.when(i == 0)
    def _():
        tail[...] = st_ref[0]

    x = u_ref[0]
    xfull = jnp.concatenate([tail[...], x], axis=0)
    tail[...] = x[tb - halo:, :]
    sums = [s[halo:] for s in _pool_windows(xfull)]
    pos = pos0 + i * tb + lax.broadcasted_iota(jnp.int32, (tb, 1), 0)
    y_ref[0] = _pool_mix(sums, x, pos, w_ref, scale_ref).astype(y_ref.dtype)


def _pool_prompt(p3, col, st, wbd, scale, l, tb, pos0):
    nb, t, _ = p3.shape
    kern = functools.partial(_pool_prompt_kernel, tb=tb, pos0=pos0)
    return pl.pallas_call(
        kern,
        grid=(nb, t // tb),
        in_specs=[pl.BlockSpec((1, tb, POOL_WIDTH), lambda b, i: (b, i, col // POOL_WIDTH)),
                  pl.BlockSpec((1, 2 * SUBLANES, POOL_WIDTH), lambda b, i: (b, 0, 0)),
                  _layer(wbd, l), _layer(scale, l)],
        out_specs=pl.BlockSpec((1, tb, POOL_WIDTH), lambda b, i: (b, i, 0)),
        out_shape=jax.ShapeDtypeStruct((nb, t, POOL_WIDTH), BF16),
        scratch_shapes=[pltpu.VMEM((2 * SUBLANES, POOL_WIDTH), F32)],
        compiler_params=_cparams(2),
        name="pool_prompt",
    )(p3, st, wbd, scale)


def _pool_sample_kernel(x_ref, w_ref, scale_ref, y_ref, *, group, first, pos0):
    x = x_ref[...]
    rows = x.shape[0]
    sums = _pool_windows(x)
    pos = pos0 + (lax.broadcasted_iota(jnp.int32, (rows, 1), 0) % group) - first
    y_ref[...] = _pool_mix(sums, x, jnp.maximum(pos, 0), w_ref, scale_ref).astype(y_ref.dtype)


def _pool_sample(ext, wbd, scale, l, group, first, pos0):
    kern = functools.partial(_pool_sample_kernel, group=group, first=first, pos0=pos0)
    const = lambda a: pl.BlockSpec(a.shape, lambda i: (0,) * a.ndim)
    return pl.pallas_call(
        kern,
        grid=(1,),
        in_specs=[const(ext), _layer(wbd, l), _layer(scale, l)],
        out_specs=const(ext),
        out_shape=jax.ShapeDtypeStruct(ext.shape, BF16),
        compiler_params=_cparams(1),
        name="pool_sample",
    )(ext, wbd, scale)


def _block_diag(blocks):
    g, r, c = blocks.shape
    eye = jnp.eye(g, dtype=blocks.dtype)
    return (eye[:, None, :, None] * blocks[:, :, None, :]).reshape(g * r, g * c)


def kernel(x_prompt, x_sample, state_delta, state_conv, state_ssm_re, state_ssm_im, state_pool, norm_mix_pre, norm_mix_post, norm_ffn_pre, norm_ffn_post, w_in, conv_w, dn_a_log, dn_dt_bias, dn_out_norm, ssm_a_re, ssm_a_im, ssm_log_dt, ssm_b_re, ssm_b_im, ssm_c_re, ssm_c_im, ssm_d, ssm_glu_w, ssm_glu_b, pool_w, pool_scale, w_out, ffn_w_gate, ffn_w_up, ffn_w_down):
    depth = w_in.shape[0]
    bp, tp, _ = x_prompt.shape
    bs, ts, _ = x_sample.shape

    w_qkv = w_in[:, :, :QKV_WIDTH].astype(BF16)
    w_rest = w_in[:, :, _OFF_G:].astype(BF16)
    w_ab = jnp.pad(w_in[:, :, _OFF_A:_OFF_G], ((0, 0), (0, 0), (0, LANES - 2 * DN_HEADS))).astype(BF16)
    w_out_b = w_out.astype(BF16)
    wg_b, wu_b, wd_b = ffn_w_gate.astype(BF16), ffn_w_up.astype(BF16), ffn_w_down.astype(BF16)
    row = lambda a: a.reshape(depth, 1, -1)
    nmp, nmo, nfp, nfo = row(norm_mix_pre), row(norm_mix_post), row(norm_ffn_pre), row(norm_ffn_post)
    alog = jnp.pad(dn_a_log, ((0, 0), (0, LANES - DN_HEADS))).reshape(depth, 1, LANES)
    dtb = jnp.pad(dn_dt_bias, ((0, 0), (0, LANES - DN_HEADS))).reshape(depth, 1, LANES)
    onorm = row(dn_out_norm)
    dt_full = jnp.repeat(ssm_log_dt, SSM_STATE, axis=1).reshape(depth, 1, SSM_NS)
    b_t = lambda b: jnp.transpose(b, (0, 3, 1, 2)).reshape(depth, SSM_GROUP, SSM_NS)
    lam_re, lam_im, bmat = _ssm_prep(ssm_a_re.reshape(depth, 1, SSM_NS), ssm_a_im.reshape(depth, 1, SSM_NS),
                                     dt_full, b_t(ssm_b_re), b_t(ssm_b_im))
    c_bd = lambda cc: jax.vmap(_block_diag)(jnp.transpose(cc, (0, 1, 3, 2))).astype(BF16)
    cre, cim = c_bd(ssm_c_re), c_bd(ssm_c_im)
    dskip, glu_b = row(ssm_d), row(ssm_glu_b)
    glu_w = ssm_glu_w.astype(BF16)
    pool_bd = jax.vmap(_block_diag)(pool_w).astype(BF16)
    pscale = row(pool_scale)

    xp = x_prompt.reshape(bp * tp, D_MODEL)
    xs = x_sample.reshape(bs * ts, D_MODEL)
    zero_conv = jnp.zeros((bp, SUBLANES, QKV_WIDTH), F32)
    zero_delta = jnp.zeros((bp, DN_HEADS, DN_HEAD_DIM, DN_HEAD_DIM), F32)
    zero_h = jnp.zeros((bp, 2 * SSM_NS), F32)
    zero_pool = jnp.zeros((bp, 2 * SUBLANES, POOL_WIDTH), F32)
    pad_rows = SAMPLE_CHUNK - ts - (DN_CONV - 1)
    pool_group = 24
    pool_first = 1 + POOL_BUF
    h0_s = jnp.concatenate([state_ssm_re.reshape(depth, bs, SSM_NS), state_ssm_im.reshape(depth, bs, SSM_NS)],
                           axis=2)

    outs_p, outs_s = [], []
    delta_s = jnp.zeros(state_delta.shape, F32)
    for l in range(depth):
        y, rest, craw = _in_proj_conv(xp, nmp, w_qkv, w_rest, w_ab, zero_conv, conv_w, l, 512, tp)
        y3 = y.reshape(bp, tp, QKV_WIDTH)
        r3 = rest.reshape(bp, tp, REST_WIDTH)
        o_dn, delta_new = _delta_prompt(y3, r3, zero_delta, alog, dtb, onorm, l, 2, 256)
        o_ssm, h_fin = _ssm(r3, REST_SSM, zero_h, bmat, lam_re, lam_im, cre, cim, dskip, glu_w, glu_b, l, 128)
        o_pool = _pool_prompt(r3, REST_POOL, zero_pool, pool_bd, pscale, l, 512, 0)
        xp = _post_mix(o_dn.reshape(bp * tp, DN_WIDTH), o_ssm.reshape(bp * tp, SSM_WIDTH),
                       o_pool.reshape(bp * tp, POOL_WIDTH), xp,
                       w_out_b, nmo, nfp, wg_b, wu_b, wd_b, nfo, l, 512)
        outs_p.append((delta_new, craw[:, SUBLANES - (DN_CONV - 1):, :],
                       h_fin[:, :SSM_NS].reshape(bp, SSM_GROUPS, SSM_STATE),
                       h_fin[:, SSM_NS:].reshape(bp, SSM_GROUPS, SSM_STATE),
                       r3[:, tp - POOL_BUF:, REST_POOL:REST_AB]))

        proj = _in_proj(xs, nmp, w_qkv, w_rest, w_ab, l, 256)
        s3 = proj.reshape(bs, ts, PROJ_WIDTH)
        head = jnp.concatenate([jnp.zeros((bs, pad_rows, QKV_WIDTH), F32), state_conv[l]], axis=1)
        head = jnp.pad(head, ((0, 0), (0, 0), (0, PROJ_WIDTH - QKV_WIDTH)))
        ext = jnp.concatenate([head, s3], axis=1).reshape(bs * SAMPLE_CHUNK, PROJ_WIDTH)
        o_ext, delta_s = _delta_sample(ext, state_delta, delta_s, conv_w, alog, dtb, onorm, l, 16,
                                       SAMPLE_CHUNK - ts)
        o_dn = o_ext.reshape(bs, SAMPLE_CHUNK, DN_WIDTH)[:, SAMPLE_CHUNK - ts:].reshape(bs * ts, DN_WIDTH)
        o_ssm, h_fin = _ssm(proj, COL_SSM, h0_s[l], bmat, lam_re, lam_im, cre, cim, dskip, glu_w, glu_b, l, ts)
        pool_u = s3[:, :, COL_POOL:COL_AB]
        pext = jnp.concatenate([jnp.zeros((bs, 1, POOL_WIDTH), F32), state_pool[l], pool_u,
                                jnp.zeros((bs, pool_group - pool_first - ts, POOL_WIDTH), F32)], axis=1)
        y_ext = _pool_sample(pext.reshape(bs * pool_group, POOL_WIDTH), pool_bd, pscale, l, pool_group,
                             pool_first, PAST_LEN)
        o_pool = y_ext.reshape(bs, pool_group, POOL_WIDTH)[:, pool_first:pool_first + ts].reshape(bs * ts, POOL_WIDTH)
        xs = _post_mix(o_dn, o_ssm, o_pool, xs, w_out_b, nmo, nfp, wg_b, wu_b, wd_b, nfo, l, 256)
        outs_s.append((s3[:, ts - (DN_CONV - 1):, :QKV_WIDTH],
                       h_fin[:, :SSM_NS].reshape(bs, SSM_GROUPS, SSM_STATE),
                       h_fin[:, SSM_NS:].reshape(bs, SSM_GROUPS, SSM_STATE),
                       jnp.concatenate([state_pool[l][:, ts:], pool_u], axis=1)))

    stack = lambda outs, k: jnp.stack([o[k] for o in outs])
    return (xp.reshape(bp, tp, D_MODEL), xs.reshape(bs, ts, D_MODEL),
            stack(outs_p, 0), stack(outs_p, 1), stack(outs_p, 2), stack(outs_p, 3), stack(outs_p, 4),
            delta_s, stack(outs_s, 0), stack(outs_s, 1), stack(outs_s, 2), stack(outs_s, 3))
```

```python
import functools
import math

import jax
import jax.numpy as jnp
from jax import lax
from jax.experimental import pallas as pl
from jax.experimental.pallas import tpu as pltpu

F32 = jnp.float32
BF16 = jnp.bfloat16
HIGHEST = lax.Precision.HIGHEST

D_MODEL = 1024
DN_HEADS = 4
DN_HEAD_DIM = 128
DN_WIDTH = DN_HEADS * DN_HEAD_DIM
DN_CONV = 4
QKV_WIDTH = 3 * DN_WIDTH
SSM_WIDTH = 256
SSM_GROUP = 16
SSM_GROUPS = 16
SSM_STATE = 64
SSM_NS = SSM_GROUPS * SSM_STATE
POOL_WIDTH = 256
POOL_WINDOWS = (2, 4, 8, 16)
POOL_GROUP = 64
POOL_BUF = 15
D_FF = 2816
EPS = 1e-6
PAST_LEN = 16384

LANES = 128
SUBLANES = 8

COL_GATE = QKV_WIDTH
COL_SSM = COL_GATE + DN_WIDTH
COL_POOL = COL_SSM + SSM_WIDTH
COL_AB = COL_POOL + POOL_WIDTH
PROJ_WIDTH = COL_AB + LANES
REST_GATE = 0
REST_SSM = COL_SSM - COL_GATE
REST_POOL = COL_POOL - COL_GATE
REST_AB = COL_AB - COL_GATE
REST_WIDTH = PROJ_WIDTH - COL_GATE

_OFF_A = QKV_WIDTH
_OFF_G = _OFF_A + 2 * DN_HEADS

VMEM_LIMIT_BYTES = 56 * 1024 * 1024

DELTA_CHUNK = 64
SAMPLE_CHUNK = 8
DELTA_SUB = 16
POST_MIX_PARTS = 2
SSM_PARTS = 4


def _cparams(n_axes):
    return pltpu.CompilerParams(dimension_semantics=("arbitrary",) * n_axes,
                                vmem_limit_bytes=VMEM_LIMIT_BYTES)


def _layer(a, l, single=False):
    zeros = (0,) * (a.ndim - 1)
    mode = dict(pipeline_mode=pl.Buffered(1)) if single else {}
    return pl.BlockSpec((None,) + a.shape[1:], lambda *_: (l,) + zeros, **mode)


def _mm(a, b):
    return jnp.dot(a, b, preferred_element_type=F32)


def _bdot(a, b):
    return jnp.dot(a.astype(BF16), b.astype(BF16), preferred_element_type=F32)


def _bdot_nt(a, b):
    return lax.dot_general(a.astype(BF16), b.astype(BF16), (((1,), (1,)), ((), ())),
                           preferred_element_type=F32)


def _bdot_tn(a, b):
    return lax.dot_general(a.astype(BF16), b.astype(BF16), (((0,), (0,)), ((), ())),
                           preferred_element_type=F32)


def _hdot(a, b):
    return jnp.dot(a, b, precision=HIGHEST, preferred_element_type=F32)


def _rms(x, w):
    return x * lax.rsqrt(jnp.mean(x * x, axis=-1, keepdims=True) + EPS) * w


def _silu(x):
    return x * jax.nn.sigmoid(x)


def _in_proj_kernel(x_ref, nw_ref, wqkv_ref, wrest_ref, wab_ref, o_ref):
    h = _rms(x_ref[...], nw_ref[...]).astype(BF16)
    o_ref[:, :QKV_WIDTH] = _mm(h, wqkv_ref[...])
    o_ref[:, COL_GATE:COL_AB] = _mm(h, wrest_ref[...])
    o_ref[:, COL_AB:] = _mm(h, wab_ref[...])


def _in_proj(x, nw, wqkv, wrest, wab, l, tm):
    m = x.shape[0]
    return pl.pallas_call(
        _in_proj_kernel,
        grid=(m // tm,),
        in_specs=[pl.BlockSpec((tm, D_MODEL), lambda i: (i, 0)), _layer(nw, l),
                  _layer(wqkv, l, True), _layer(wrest, l, True), _layer(wab, l, True)],
        out_specs=pl.BlockSpec((tm, PROJ_WIDTH), lambda i: (i, 0)),
        out_shape=jax.ShapeDtypeStruct((m, PROJ_WIDTH), F32),
        compiler_params=_cparams(1),
        name="in_proj",
    )(x, nw, wqkv, wrest, wab)


def _in_proj_conv_kernel(x_ref, nw_ref, wqkv_ref, wrest_ref, wab_ref, cst_ref, cw_ref,
                         y_ref, o_ref, craw_ref, tail, *, tm, nt, nc):
    i = pl.program_id(0)

    @pl.when(i % nt == 0)
    def _():
        tail[...] = cst_ref[0]

    h = _rms(x_ref[...], nw_ref[...]).astype(BF16)
    for cc in range(QKV_WIDTH // nc):
        cs = slice(cc * nc, (cc + 1) * nc)
        r = _mm(h, wqkv_ref[:, cs])
        xfull = jnp.concatenate([tail[:, cs], r], axis=0)
        tail[:, cs] = r[tm - SUBLANES:, :]
        y = _silu(_conv4(xfull, cw_ref[:, cs], tm, SUBLANES))
        if cc * nc < 2 * DN_WIDTH:
            scale = DN_HEAD_DIM ** -0.5 if cc * nc < DN_WIDTH else 1.0
            parts = []
            for j in range(nc // DN_HEAD_DIM):
                yh = y[:, j * DN_HEAD_DIM:(j + 1) * DN_HEAD_DIM]
                parts.append(yh * lax.rsqrt(jnp.sum(yh * yh, axis=-1, keepdims=True) + EPS) * scale)
            y = jnp.concatenate(parts, axis=1)
        y_ref[:, cs] = y
    o_ref[:, :REST_AB] = _mm(h, wrest_ref[...])
    o_ref[:, REST_AB:] = _mm(h, wab_ref[...])
    craw_ref[0] = tail[...]


def _in_proj_conv(x, nw, wqkv, wrest, wab, cst, cw, l, tm, seq_len):
    m = x.shape[0]
    nt = seq_len // tm
    kern = functools.partial(_in_proj_conv_kernel, tm=tm, nt=nt, nc=2 * DN_HEAD_DIM)
    tail_spec = pl.BlockSpec((1, SUBLANES, QKV_WIDTH), lambda i: (i // nt, 0, 0))
    return pl.pallas_call(
        kern,
        grid=(m // tm,),
        in_specs=[pl.BlockSpec((tm, D_MODEL), lambda i: (i, 0)), _layer(nw, l),
                  _layer(wqkv, l, True), _layer(wrest, l, True), _layer(wab, l, True), tail_spec, _layer(cw, l)],
        out_specs=[pl.BlockSpec((tm, QKV_WIDTH), lambda i: (i, 0)),
                   pl.BlockSpec((tm, REST_WIDTH), lambda i: (i, 0)), tail_spec],
        out_shape=[jax.ShapeDtypeStruct((m, QKV_WIDTH), F32), jax.ShapeDtypeStruct((m, REST_WIDTH), F32),
                   jax.ShapeDtypeStruct((m // seq_len, SUBLANES, QKV_WIDTH), F32)],
        scratch_shapes=[pltpu.VMEM((SUBLANES, QKV_WIDTH), F32)],
        compiler_params=_cparams(1),
        name="in_proj_conv",
    )(x, nw, wqkv, wrest, wab, cst, cw)


def _post_mix_kernel(odn_ref, ossm_ref, opool_ref, x_ref, wo_ref, nmp_ref, nfp_ref, wg_ref, wu_ref, wd_ref,
                     nfo_ref, o_ref):
    tm = x_ref.shape[0]
    parts = [slice(p * tm // POST_MIX_PARTS, (p + 1) * tm // POST_MIX_PARTS) for p in range(POST_MIX_PARTS)]
    mix = [(_bdot(odn_ref[rs, :], wo_ref[0:DN_WIDTH, :])
            + _bdot(ossm_ref[rs, :], wo_ref[DN_WIDTH:DN_WIDTH + SSM_WIDTH, :])
            + _bdot(opool_ref[rs, :], wo_ref[DN_WIDTH + SSM_WIDTH:, :])) for rs in parts]
    x1 = [x_ref[rs, :] + _rms(m, nmp_ref[...]) for rs, m in zip(parts, mix)]
    h = [_rms(x, nfp_ref[...]).astype(BF16) for x in x1]
    g = [_mm(y, wg_ref[...]) for y in h]
    u = [_mm(y, wu_ref[...]) for y in h]
    f = [_bdot(_silu(a) * b, wd_ref[...]) for a, b in zip(g, u)]
    for rs, x, y in zip(parts, x1, f):
        o_ref[rs, :] = x + _rms(y, nfo_ref[...])


def _post_mix(odn, ossm, opool, x, wo, nmp, nfp, wg, wu, wd, nfo, l, tm):
    m = x.shape[0]
    row = lambda w: pl.BlockSpec((tm, w), lambda i: (i, 0))
    return pl.pallas_call(
        _post_mix_kernel,
        grid=(m // tm,),
        in_specs=[row(DN_WIDTH), row(SSM_WIDTH), row(POOL_WIDTH), row(D_MODEL),
                  _layer(wo, l, True), _layer(nmp, l), _layer(nfp, l), _layer(wg, l, True), _layer(wu, l, True),
                  _layer(wd, l, True), _layer(nfo, l)],
        out_specs=row(D_MODEL),
        out_shape=jax.ShapeDtypeStruct((m, D_MODEL), F32),
        compiler_params=_cparams(1),
        name="post_mix",
    )(odn, ossm, opool, x, wo, nmp, nfp, wg, wu, wd, nfo)


def _split2(x):
    hi = x.astype(BF16)
    return hi, (x - hi.astype(F32)).astype(BF16)


def _group_cumsum(x, group, reverse=False):
    rows = x.shape[0]
    pos = lax.broadcasted_iota(jnp.int32, x.shape, 0) % group
    d = 1
    while d < group:
        if reverse:
            x = x + jnp.where(pos + d < group, pltpu.roll(x, rows - d, 0), 0.0)
        else:
            x = x + jnp.where(pos >= d, pltpu.roll(x, d, 0), 0.0)
        d *= 2
    return x


def _chunk_masks(rows, blk):
    r = lax.broadcasted_iota(jnp.int32, (rows, rows), 0)
    s = lax.broadcasted_iota(jnp.int32, (rows, rows), 1)
    d = r - s
    if blk < rows:
        d = jnp.where(r // blk == s // blk, d, -1)
    sub = None
    if blk > DELTA_SUB:
        sub = r // DELTA_SUB == s // DELTA_SUB
    return d >= 0, d > 0, (r == s).astype(F32), sub


def _run(gen):
    try:
        while True:
            next(gen)
    except StopIteration as stop:
        return stop.value


def _run_interleaved(gens):
    results = [None] * len(gens)
    live = list(range(len(gens)))
    while live:
        for i in list(live):
            try:
                next(gens[i])
            except StopIteration as stop:
                results[i] = stop.value
                live.remove(i)
    return results


def _neumann(a, eye, index):
    t = [eye - x for x in a]
    p = a
    n = 2
    while n < index:
        p = [_bdot(x, x) for x in p]
        yield
        t = [x + _bdot(x, y) for x, y in zip(t, p)]
        yield
        n *= 2
    return t


def _unit_lower_inverse(a, eye, sub, index):
    if sub is None:
        return (yield from _neumann(a, eye, index))
    assert index <= 4 * DELTA_SUB
    d = [jnp.where(sub, x, 0.0) for x in a]
    td = yield from _neumann(d, eye, DELTA_SUB)
    n = [_bdot(t, x - y) for t, x, y in zip(td, a, d)]
    yield
    n2 = [_bdot(x, x) for x in n]
    yield
    m = [_bdot(eye - x, eye + y) for x, y in zip(n, n2)]
    yield
    t = [_bdot(x, t) for x, t in zip(m, td)]
    yield
    return t


def _delta_local(probs, masks, index):
    incl, strict, eye, sub = masks
    rows = probs[0][0].shape[0]
    kb = [k * beta for (_, k, _, beta, _, _, _) in probs]
    kq = [_bdot_nt(jnp.concatenate([x, q], axis=0), k) for x, (q, k, _, _, _, _, _) in zip(kb, probs)]
    yield
    decay = [jnp.where(incl, jnp.exp(jnp.where(incl, gam - grow, 0.0)), 0.0) for (_, _, _, _, gam, _, grow) in probs]
    a = [jnp.where(strict, x[:rows] * d, 0.0) for x, d in zip(kq, decay)]
    qk = [x[rows:] * d for x, d in zip(kq, decay)]
    egam = [jnp.exp(p[4]) for p in probs]
    rhs = [jnp.concatenate([p[2] * p[3], x * e], axis=1) for p, x, e in zip(probs, kb, egam)]
    t = yield from _unit_lower_inverse(a, eye, sub, index)
    sol = [_bdot(x, r) for x, r in zip(t, rhs)]
    yield
    a_sp = [_split2(x) for x in a]
    s_sp = [_split2(x) for x in sol]
    asol = [_mm(ah, sh) + _mm(ah, sl) + _mm(al, sh) for (ah, al), (sh, sl) in zip(a_sp, s_sp)]
    yield
    resid = [r - s - x for r, s, x in zip(rhs, sol, asol)]
    sol = [s + _bdot(x, r) for s, x, r in zip(sol, t, resid)]
    yield
    return [(s[:, :DN_HEAD_DIM], s[:, DN_HEAD_DIM:], x, p[0] * e, p[1] * jnp.exp(p[5]))
            for s, x, p, e in zip(sol, qk, probs, egam)]


def _delta_scan(local, s, glast, nchunk, c):
    chains = range(len(s))
    outs = [[] for _ in chains]
    for j in range(nchunk):
        ks_ = [_bdot(jnp.concatenate([local[n][j][1], local[n][j][3]], axis=0), s[n]) for n in chains]
        yield
        u = [local[n][j][0] - ks_[n][:c] for n in chains]
        for n in chains:
            outs[n].append(ks_[n][c:] + _bdot(local[n][j][2], u[n]))
        yield
        s = [s[n] * jnp.exp(glast[n][j]) + _bdot_tn(local[n][j][4], u[n]) for n in chains]
        yield
    return s, [jnp.concatenate(o, axis=0) for o in outs]


def _delta_front(y, ab, alog, dtb):
    y = _silu(y)
    qs, ks = [], []
    for h in range(DN_HEADS):
        qh = y[:, h * DN_HEAD_DIM:(h + 1) * DN_HEAD_DIM]
        kh = y[:, DN_WIDTH + h * DN_HEAD_DIM:DN_WIDTH + (h + 1) * DN_HEAD_DIM]
        qs.append(qh * lax.rsqrt(jnp.sum(qh * qh, axis=-1, keepdims=True) + EPS) * (DN_HEAD_DIM ** -0.5))
        ks.append(kh * lax.rsqrt(jnp.sum(kh * kh, axis=-1, keepdims=True) + EPS))
    v = y[:, 2 * DN_WIDTH:]
    logg, beta = _delta_gates(ab, alog, dtb)
    return qs, ks, v, logg, beta


def _delta_gates(ab, alog, dtb):
    z = ab + dtb
    softplus = jnp.maximum(z, 0.0) + jnp.log1p(jnp.exp(-jnp.abs(z)))
    return -jnp.exp(alog) * softplus, jax.nn.sigmoid(ab)


def _delta_out(o, gate, onorm):
    return o * lax.rsqrt(jnp.mean(o * o, axis=-1, keepdims=True) + EPS) * onorm * _silu(gate)


def _conv4(xfull, w, rows, off):
    z = xfull * w[0:1]
    z = xfull * w[1:2] + pltpu.roll(z, 1, 0)
    z = xfull * w[2:3] + pltpu.roll(z, 1, 0)
    return (xfull * w[3:4] + pltpu.roll(z, 1, 0))[off:off + rows]


def _delta_prompt_kernel(qkv_ref, ab_ref, gate_ref, s0_ref, alog_ref, dtb_ref, onorm_ref,
                         o_ref, sfin_ref, s_scr, *, nseq, group, tb, c):
    i = pl.program_id(1)

    @pl.when(i == 0)
    def _():
        s_scr[...] = s0_ref[...]

    masks = _chunk_masks(c, c)
    nchunk = tb // c
    heads = range(DN_HEADS)
    hsl = [slice(h * DN_HEAD_DIM, (h + 1) * DN_HEAD_DIM) for h in heads]

    def chunk_problems(chains):
        probs, glast = [], []
        per_seq = {}
        for r, h in chains:
            if r not in per_seq:
                logg, beta = _delta_gates(ab_ref[r], alog_ref[...], dtb_ref[...])
                gam = _group_cumsum(logg, c)
                gexc = _group_cumsum(logg, c, reverse=True) - logg
                per_seq[r] = (beta, gam, gexc, gam.T)
            beta, gam, gexc, gam_t = per_seq[r]
            ksl = slice(DN_WIDTH + h * DN_HEAD_DIM, DN_WIDTH + (h + 1) * DN_HEAD_DIM)
            vsl = slice(2 * DN_WIDTH + h * DN_HEAD_DIM, 2 * DN_WIDTH + (h + 1) * DN_HEAD_DIM)
            for j in range(nchunk):
                rs = slice(j * c, (j + 1) * c)
                probs.append((qkv_ref[r, rs, hsl[h]], qkv_ref[r, rs, ksl], qkv_ref[r, rs, vsl],
                              beta[rs, DN_HEADS + h:DN_HEADS + h + 1], gam[rs, h:h + 1], gexc[rs, h:h + 1],
                              gam_t[h:h + 1, rs]))
            glast.append([gam[(j + 1) * c - 1:(j + 1) * c, h:h + 1] for j in range(nchunk)])
        return probs, glast

    def finish(chains, result):
        s, outs = result
        for n, (r, h) in enumerate(chains):
            s_scr[r, h] = s[n]
            o = _delta_out(outs[n], gate_ref[r, :, hsl[h]], onorm_ref[...])
            o_ref[r, :, hsl[h]] = o.astype(o_ref.dtype)

    groups = [[(r, h) for r in range(g, g + group) for h in heads] for g in range(0, nseq, group)]
    scan, prev = None, None
    for chains in groups:
        probs, glast = chunk_problems(chains)
        local_gen = _delta_local(probs, masks, c)
        if scan is None:
            flat = _run(local_gen)
        else:
            flat, result = _run_interleaved([local_gen, scan])
            finish(prev, result)
        local = [flat[n * nchunk:(n + 1) * nchunk] for n in range(len(chains))]
        scan = _delta_scan(local, [s_scr[r, h] for r, h in chains], glast, nchunk, c)
        prev = chains
    finish(prev, _run(scan))
    sfin_ref[...] = s_scr[...]


def _delta_prompt(y3, rest3, s0, alog, dtb, onorm, l, nseq, group, tb):
    nb, t, _ = y3.shape
    kern = functools.partial(_delta_prompt_kernel, nseq=nseq, group=group, tb=tb, c=DELTA_CHUNK)
    sspec = pl.BlockSpec((nseq, DN_HEADS, DN_HEAD_DIM, DN_HEAD_DIM), lambda b, i: (b, 0, 0, 0))
    return pl.pallas_call(
        kern,
        grid=(nb // nseq, t // tb),
        in_specs=[pl.BlockSpec((nseq, tb, QKV_WIDTH), lambda b, i: (b, i, 0)),
                  pl.BlockSpec((nseq, tb, LANES), lambda b, i: (b, i, REST_AB // LANES)),
                  pl.BlockSpec((nseq, tb, DN_WIDTH), lambda b, i: (b, i, REST_GATE // DN_WIDTH)),
                  sspec, _layer(alog, l), _layer(dtb, l), _layer(onorm, l)],
        out_specs=[pl.BlockSpec((nseq, tb, DN_WIDTH), lambda b, i: (b, i, 0)), sspec],
        out_shape=[jax.ShapeDtypeStruct((nb, t, DN_WIDTH), BF16),
                   jax.ShapeDtypeStruct((nb, DN_HEADS, DN_HEAD_DIM, DN_HEAD_DIM), F32)],
        scratch_shapes=[pltpu.VMEM((nseq, DN_HEADS, DN_HEAD_DIM, DN_HEAD_DIM), F32)],
        compiler_params=_cparams(2),
        name="delta_prompt",
    )(y3, rest3, rest3, s0, alog, dtb, onorm)


def _delta_sample_kernel(qkv_ref, ab_ref, gate_ref, s0_ref, cw_ref, alog_ref, dtb_ref, onorm_ref, *rest,
                         nseq, c, first):
    o_ref, snew_ref = rest[-2:]
    rows = nseq * c
    x = qkv_ref[...]
    y = _conv4(x, cw_ref[...], rows, 0)
    qs, ks, v, logg, beta = _delta_front(y, ab_ref[...], alog_ref[...], dtb_ref[...])
    valid = (lax.broadcasted_iota(jnp.int32, (rows, LANES), 0) % c) >= first
    logg = jnp.where(valid, logg, 0.0)
    beta = jnp.where(valid, beta, 0.0)
    gam = _group_cumsum(logg, c)
    gexc = _group_cumsum(logg, c, reverse=True) - logg
    gam_t = gam.T
    masks = _chunk_masks(rows, c)
    gate = gate_ref[...]
    heads = range(DN_HEADS)
    hsl = [slice(h * DN_HEAD_DIM, (h + 1) * DN_HEAD_DIM) for h in heads]
    local = _run(_delta_local([(qs[h], ks[h], v[:, hsl[h]], beta[:, DN_HEADS + h:DN_HEADS + h + 1],
                                gam[:, h:h + 1], gexc[:, h:h + 1], gam_t[h:h + 1, :]) for h in heads], masks, c))
    pairs = [(b, h) for b in range(nseq) for h in heads]
    rsl = [slice(b * c, (b + 1) * c) for b in range(nseq)]
    ks_ = {(b, h): _bdot(jnp.concatenate([local[h][1][rsl[b]], local[h][3][rsl[b]]], axis=0), s0_ref[b, h])
           for b, h in pairs}
    u = {(b, h): local[h][0][rsl[b]] - ks_[b, h][:c] for b, h in pairs}
    for b, h in pairs:
        last = (b + 1) * c - 1
        snew_ref[b, h] = (s0_ref[b, h] * jnp.exp(gam[last:last + 1, h:h + 1])
                          + _bdot_tn(local[h][4][rsl[b]], u[b, h]))
    for h in heads:
        o = (jnp.concatenate([ks_[b, h][c:] for b in range(nseq)], axis=0)
             + _bdot(local[h][2], jnp.concatenate([u[b, h] for b in range(nseq)], axis=0)))
        o_ref[:, hsl[h]] = _delta_out(o, gate[:, hsl[h]], onorm_ref[...]).astype(o_ref.dtype)


def _delta_sample(ext, s0_all, new_all, cw, alog, dtb, onorm, l, nseq, first):
    c = SAMPLE_CHUNK
    nb = s0_all.shape[1]
    rows = nseq * c
    kern = functools.partial(_delta_sample_kernel, nseq=nseq, c=c, first=first)
    sspec = pl.BlockSpec((None, nseq, DN_HEADS, DN_HEAD_DIM, DN_HEAD_DIM), lambda i: (l, i, 0, 0, 0))
    in_specs = [pl.BlockSpec((rows, QKV_WIDTH), lambda i: (i, 0)),
                pl.BlockSpec((rows, LANES), lambda i: (i, COL_AB // LANES)),
                pl.BlockSpec((rows, DN_WIDTH), lambda i: (i, COL_GATE // DN_WIDTH)),
                sspec, _layer(cw, l), _layer(alog, l), _layer(dtb, l), _layer(onorm, l)]
    args = [ext, ext, ext, s0_all, cw, alog, dtb, onorm]
    aliases = {}
    if new_all is not None:
        in_specs.append(pl.BlockSpec(memory_space=pl.ANY))
        args.append(new_all)
        aliases = {len(args) - 1: 1}
    return pl.pallas_call(
        kern,
        grid=(nb // nseq,),
        in_specs=in_specs,
        out_specs=[pl.BlockSpec((rows, DN_WIDTH), lambda i: (i, 0)), sspec],
        out_shape=[jax.ShapeDtypeStruct((nb * c, DN_WIDTH), BF16),
                   jax.ShapeDtypeStruct(s0_all.shape, F32)],
        input_output_aliases=aliases,
        compiler_params=_cparams(1),
        name="delta_sample",
    )(*args)


def _ssm_prep_kernel(are_ref, aim_ref, dt_ref, bre_ref, bim_ref, lre_ref, lim_ref, bmat_ref):
    ar, ai, dt = are_ref[0], aim_ref[0], jnp.exp(dt_ref[0])
    mag = jnp.exp(ar * dt)
    lr = mag * jnp.cos(ai * dt)
    li = mag * jnp.sin(ai * dt)
    lre_ref[0] = lr
    lim_ref[0] = li
    den = ar * ar + ai * ai
    fr = ((lr - 1.0) * ar + li * ai) / den
    fi = (li * ar - (lr - 1.0) * ai) / den
    br, bi = bre_ref[0], bim_ref[0]
    bbr = fr * br - fi * bi
    bbi = fr * bi + fi * br
    lane_group = lax.broadcasted_iota(jnp.int32, (SSM_GROUP, SSM_NS), 1) // SSM_STATE
    for g in range(SSM_GROUPS):
        m = lane_group == g
        bmat_ref[0, g * SSM_GROUP:(g + 1) * SSM_GROUP, 0:SSM_NS] = jnp.where(m, bbr, 0.0)
        bmat_ref[0, g * SSM_GROUP:(g + 1) * SSM_GROUP, SSM_NS:] = jnp.where(m, bbi, 0.0)


def _ssm_prep(are, aim, dt, bre, bim):
    depth = are.shape[0]
    vec = pl.BlockSpec((1, 1, SSM_NS), lambda l: (l, 0, 0))
    mat = pl.BlockSpec((1, SSM_GROUP, SSM_NS), lambda l: (l, 0, 0))
    return pl.pallas_call(
        _ssm_prep_kernel,
        grid=(depth,),
        in_specs=[vec, vec, vec, mat, mat],
        out_specs=[vec, vec, pl.BlockSpec((1, SSM_WIDTH, 2 * SSM_NS), lambda l: (l, 0, 0))],
        out_shape=[jax.ShapeDtypeStruct((depth, 1, SSM_NS), F32), jax.ShapeDtypeStruct((depth, 1, SSM_NS), F32),
                   jax.ShapeDtypeStruct((depth, SSM_WIDTH, 2 * SSM_NS), F32)],
        compiler_params=_cparams(1),
        name="ssm_prep",
    )(are, aim, dt, bre, bim)


def _gelu_tanh(x):
    return 0.5 * x * (1.0 + jnp.tanh(math.sqrt(2.0 / math.pi) * (x + 0.044715 * (x * x * x))))


def _ssm_kernel(ulo_ref, uhi_ref, h0_ref, bmat_ref, lre_ref, lim_ref, cre_ref, cim_ref, d_ref, gw_ref, gb_ref,
                y_ref, hl_ref, buf, h_scr, tm_scr, bm_scr, *, nb, nt, nparts, exact_in):
    i = pl.program_id(0)
    halves = range(SSM_WIDTH // LANES)
    lanes = [slice(s * LANES, (s + 1) * LANES) for s in halves]

    @pl.when(i == 0)
    def _():
        h_scr[...] = h0_ref[...]

    for s, u_ref in enumerate((ulo_ref, uhi_ref)):
        if nb <= nt:
            for b in range(nb):
                tm_scr[s, pl.ds(b, nt, stride=nb), :] = u_ref[b]
        else:
            for t in range(nt):
                tm_scr[s, t * nb:(t + 1) * nb, :] = u_ref[pl.ds(t, nb, stride=nt), :]
    u = jnp.concatenate([tm_scr[s] for s in halves], axis=1)
    lr = jnp.broadcast_to(lre_ref[...], (SUBLANES, SSM_NS))
    li = jnp.broadcast_to(lim_ref[...], (SUBLANES, SSM_NS))
    pt = nt // nparts
    prow = [slice(p * pt * nb, (p + 1) * pt * nb) for p in range(nparts)]
    groups = range(nb // SUBLANES)
    h = [h_scr[g * SUBLANES:(g + 1) * SUBLANES, :] for g in groups]

    bmat = bmat_ref[...] if exact_in else bmat_ref[...].astype(BF16)

    def in_map(p):
        up = u[prow[p]]
        buf[prow[p], :] = _hdot(up, bmat) if exact_in else _bdot(up, bmat)

    def recur(p):
        for t in range(p * pt, (p + 1) * pt):
            for g in groups:
                rs = slice(t * nb + g * SUBLANES, t * nb + (g + 1) * SUBLANES)
                bu = buf[rs, :]
                hre, him = h[g][:, :SSM_NS], h[g][:, SSM_NS:]
                h[g] = jnp.concatenate([lr * hre - li * him + bu[:, :SSM_NS],
                                        lr * him + li * hre + bu[:, SSM_NS:]], axis=1)
                buf[rs, :] = h[g]

    def out_map(p):
        rs = prow[p]
        y = _bdot(buf[rs, :SSM_NS], cre_ref[...]) - _bdot(buf[rs, SSM_NS:], cim_ref[...]) + d_ref[...] * u[rs]
        y = _gelu_tanh(y)
        y = y * jax.nn.sigmoid(_bdot(y, gw_ref[...]) + gb_ref[...])
        for s in halves:
            tm_scr[s, rs, :] = y[:, lanes[s]]

    in_map(0)
    for p in range(nparts):
        if p + 1 < nparts:
            in_map(p + 1)
        recur(p)
        if p >= 1:
            out_map(p - 1)
    out_map(nparts - 1)
    for g in groups:
        h_scr[g * SUBLANES:(g + 1) * SUBLANES, :] = h[g]
    hl_ref[...] = h_scr[...]
    for s in halves:
        if nb <= nt:
            for b in range(nb):
                y_ref[b, :, lanes[s]] = tm_scr[s, pl.ds(b, nt, stride=nb), :].astype(y_ref.dtype)
        else:
            for t in range(nt):
                bm_scr[s, pl.ds(t, nb, stride=nt), :] = tm_scr[s, t * nb:(t + 1) * nb, :]
            y_ref[:, lanes[s]] = bm_scr[s].astype(y_ref.dtype)


def _ssm(proj, col, h0, bmat, lre, lim, cre, cim, dskip, gw, gb, l, nt):
    nb = h0.shape[0]
    rows = nt * nb
    lo = col // LANES
    if proj.ndim == 3:
        t = proj.shape[1]
        grid = (t // nt,)
        u_spec = lambda c: pl.BlockSpec((nb, nt, LANES), lambda i: (0, i, c))
        y_spec = pl.BlockSpec((nb, nt, SSM_WIDTH), lambda i: (0, i, 0))
        y_shape = (nb, t, SSM_WIDTH)
    else:
        grid = (1,)
        u_spec = lambda c: pl.BlockSpec((rows, LANES), lambda i: (0, c))
        y_spec = pl.BlockSpec((rows, SSM_WIDTH), lambda i: (0, 0))
        y_shape = (rows, SSM_WIDTH)
    kern = functools.partial(_ssm_kernel, nb=nb, nt=nt, nparts=SSM_PARTS if proj.ndim == 3 else 1,
                             exact_in=proj.ndim == 2)
    const = lambda a: pl.BlockSpec(a.shape, lambda i: (0,) * a.ndim)
    return pl.pallas_call(
        kern,
        grid=grid,
        in_specs=[u_spec(lo), u_spec(lo + 1), const(h0), _layer(bmat, l), _layer(lre, l), _layer(lim, l),
                  _layer(cre, l), _layer(cim, l), _layer(dskip, l), _layer(gw, l), _layer(gb, l)],
        out_specs=[y_spec, const(h0)],
        out_shape=[jax.ShapeDtypeStruct(y_shape, BF16), jax.ShapeDtypeStruct(h0.shape, F32)],
        scratch_shapes=[pltpu.VMEM((rows, 2 * SSM_NS), F32), pltpu.VMEM(h0.shape, F32),
                        pltpu.VMEM((SSM_WIDTH // LANES, rows, LANES), F32),
                        pltpu.VMEM((SSM_WIDTH // LANES, rows, LANES), F32)],
        compiler_params=_cparams(1),
        name="ssm",
    )(proj, proj, h0, bmat, lre, lim, cre, cim, dskip, gw, gb)


def _pool_windows(xfull):
    s2 = xfull + pltpu.roll(xfull, 1, 0)
    s4 = s2 + pltpu.roll(s2, 2, 0)
    s8 = s4 + pltpu.roll(s4, 4, 0)
    s16 = s8 + pltpu.roll(s8, 8, 0)
    return s2, s4, s8, s16


def _pool_mix(sums, x, pos, w_ref, scale_ref):
    lane = lax.broadcasted_iota(jnp.int32, (1, POOL_WIDTH), 1) // POOL_GROUP
    win = None
    for gidx in reversed(range(len(POOL_WINDOWS))):
        cnt = jnp.minimum(pos + 1, POOL_WINDOWS[gidx]).astype(F32)
        term = sums[gidx] / cnt
        win = term if win is None else jnp.where(lane == gidx, term, win)
    r = win - x
    return _bdot(r, w_ref[...]) * scale_ref[...]


def _pool_prompt_kernel(u_ref, st_ref, w_ref, scale_ref, y_ref, tail, *, tb, pos0):
    i = pl.program_id(1)
    halo = 2 * SUBLANES

    @pl.when(i == 0)
    def _():
        tail[...] = st_ref[0]

    x = u_ref[0]
    xfull = jnp.concatenate([tail[...], x], axis=0)
    tail[...] = x[tb - halo:, :]
    sums = [s[halo:] for s in _pool_windows(xfull)]
    pos = pos0 + i * tb + lax.broadcasted_iota(jnp.int32, (tb, 1), 0)
    y_ref[0] = _pool_mix(sums, x, pos, w_ref, scale_ref).astype(y_ref.dtype)


def _pool_prompt(p3, col, st, wbd, scale, l, tb, pos0):
    nb, t, _ = p3.shape
    kern = functools.partial(_pool_prompt_kernel, tb=tb, pos0=pos0)
    return pl.pallas_call(
        kern,
        grid=(nb, t // tb),
        in_specs=[pl.BlockSpec((1, tb, POOL_WIDTH), lambda b, i: (b, i, col // POOL_WIDTH)),
                  pl.BlockSpec((1, 2 * SUBLANES, POOL_WIDTH), lambda b, i: (b, 0, 0)),
                  _layer(wbd, l), _layer(scale, l)],
        out_specs=pl.BlockSpec((1, tb, POOL_WIDTH), lambda b, i: (b, i, 0)),
        out_shape=jax.ShapeDtypeStruct((nb, t, POOL_WIDTH), BF16),
        scratch_shapes=[pltpu.VMEM((2 * SUBLANES, POOL_WIDTH), F32)],
        compiler_params=_cparams(2),
        name="pool_prompt",
    )(p3, st, wbd, scale)


def _pool_sample_kernel(x_ref, w_ref, scale_ref, y_ref, *, group, first, pos0):
    x = x_ref[...]
    rows = x.shape[0]
    sums = _pool_windows(x)
    pos = pos0 + (lax.broadcasted_iota(jnp.int32, (rows, 1), 0) % group) - first
    y_ref[...] = _pool_mix(sums, x, jnp.maximum(pos, 0), w_ref, scale_ref).astype(y_ref.dtype)


def _pool_sample(ext, wbd, scale, l, group, first, pos0):
    kern = functools.partial(_pool_sample_kernel, group=group, first=first, pos0=pos0)
    const = lambda a: pl.BlockSpec(a.shape, lambda i: (0,) * a.ndim)
    return pl.pallas_call(
        kern,
        grid=(1,),
        in_specs=[const(ext), _layer(wbd, l), _layer(scale, l)],
        out_specs=const(ext),
        out_shape=jax.ShapeDtypeStruct(ext.shape, BF16),
        compiler_params=_cparams(1),
        name="pool_sample",
    )(ext, wbd, scale)


def _block_diag(blocks):
    g, r, c = blocks.shape
    eye = jnp.eye(g, dtype=blocks.dtype)
    return (eye[:, None, :, None] * blocks[:, :, None, :]).reshape(g * r, g * c)


def kernel(x_prompt, x_sample, state_delta, state_conv, state_ssm_re, state_ssm_im, state_pool, norm_mix_pre, norm_mix_post, norm_ffn_pre, norm_ffn_post, w_in, conv_w, dn_a_log, dn_dt_bias, dn_out_norm, ssm_a_re, ssm_a_im, ssm_log_dt, ssm_b_re, ssm_b_im, ssm_c_re, ssm_c_im, ssm_d, ssm_glu_w, ssm_glu_b, pool_w, pool_scale, w_out, ffn_w_gate, ffn_w_up, ffn_w_down):
    depth = w_in.shape[0]
    bp, tp, _ = x_prompt.shape
    bs, ts, _ = x_sample.shape

    w_qkv = w_in[:, :, :QKV_WIDTH].astype(BF16)
    w_rest = w_in[:, :, _OFF_G:].astype(BF16)
    w_ab = jnp.pad(w_in[:, :, _OFF_A:_OFF_G], ((0, 0), (0, 0), (0, LANES - 2 * DN_HEADS))).astype(BF16)
    w_out_b = w_out.astype(BF16)
    wg_b, wu_b, wd_b = ffn_w_gate.astype(BF16), ffn_w_up.astype(BF16), ffn_w_down.astype(BF16)
    row = lambda a: a.reshape(depth, 1, -1)
    nmp, nmo, nfp, nfo = row(norm_mix_pre), row(norm_mix_post), row(norm_ffn_pre), row(norm_ffn_post)
    alog = jnp.pad(dn_a_log, ((0, 0), (0, LANES - DN_HEADS))).reshape(depth, 1, LANES)
    dtb = jnp.pad(dn_dt_bias, ((0, 0), (0, LANES - DN_HEADS))).reshape(depth, 1, LANES)
    onorm = row(dn_out_norm)
    dt_full = jnp.repeat(ssm_log_dt, SSM_STATE, axis=1).reshape(depth, 1, SSM_NS)
    b_t = lambda b: jnp.transpose(b, (0, 3, 1, 2)).reshape(depth, SSM_GROUP, SSM_NS)
    lam_re, lam_im, bmat = _ssm_prep(ssm_a_re.reshape(depth, 1, SSM_NS), ssm_a_im.reshape(depth, 1, SSM_NS),
                                     dt_full, b_t(ssm_b_re), b_t(ssm_b_im))
    c_bd = lambda cc: jax.vmap(_block_diag)(jnp.transpose(cc, (0, 1, 3, 2))).astype(BF16)
    cre, cim = c_bd(ssm_c_re), c_bd(ssm_c_im)
    dskip, glu_b = row(ssm_d), row(ssm_glu_b)
    glu_w = ssm_glu_w.astype(BF16)
    pool_bd = jax.vmap(_block_diag)(pool_w).astype(BF16)
    pscale = row(pool_scale)

    xp = x_prompt.reshape(bp * tp, D_MODEL)
    xs = x_sample.reshape(bs * ts, D_MODEL)
    zero_conv = jnp.zeros((bp, SUBLANES, QKV_WIDTH), F32)
    zero_delta = jnp.zeros((bp, DN_HEADS, DN_HEAD_DIM, DN_HEAD_DIM), F32)
    zero_h = jnp.zeros((bp, 2 * SSM_NS), F32)
    zero_pool = jnp.zeros((bp, 2 * SUBLANES, POOL_WIDTH), F32)
    pad_rows = SAMPLE_CHUNK - ts - (DN_CONV - 1)
    pool_group = 24
    pool_first = 1 + POOL_BUF
    h0_s = jnp.concatenate([state_ssm_re.reshape(depth, bs, SSM_NS), state_ssm_im.reshape(depth, bs, SSM_NS)],
                           axis=2)

    outs_p, outs_s = [], []
    delta_s = jnp.zeros(state_delta.shape, F32)
    for l in range(depth):
        y, rest, craw = _in_proj_conv(xp, nmp, w_qkv, w_rest, w_ab, zero_conv, conv_w, l, 512, tp)
        y3 = y.reshape(bp, tp, QKV_WIDTH)
        r3 = rest.reshape(bp, tp, REST_WIDTH)
        o_dn, delta_new = _delta_prompt(y3, r3, zero_delta, alog, dtb, onorm, l, 4, 2, 256)
        o_ssm, h_fin = _ssm(r3, REST_SSM, zero_h, bmat, lam_re, lam_im, cre, cim, dskip, glu_w, glu_b, l, 128)
        o_pool = _pool_prompt(r3, REST_POOL, zero_pool, pool_bd, pscale, l, 512, 0)
        xp = _post_mix(o_dn.reshape(bp * tp, DN_WIDTH), o_ssm.reshape(bp * tp, SSM_WIDTH),
                       o_pool.reshape(bp * tp, POOL_WIDTH), xp,
                       w_out_b, nmo, nfp, wg_b, wu_b, wd_b, nfo, l, 512)
        outs_p.append((delta_new, craw[:, SUBLANES - (DN_CONV - 1):, :],
                       h_fin[:, :SSM_NS].reshape(bp, SSM_GROUPS, SSM_STATE),
                       h_fin[:, SSM_NS:].reshape(bp, SSM_GROUPS, SSM_STATE),
                       r3[:, tp - POOL_BUF:, REST_POOL:REST_AB]))

        proj = _in_proj(xs, nmp, w_qkv, w_rest, w_ab, l, 256)
        s3 = proj.reshape(bs, ts, PROJ_WIDTH)
        head = jnp.concatenate([jnp.zeros((bs, pad_rows, QKV_WIDTH), F32), state_conv[l]], axis=1)
        head = jnp.pad(head, ((0, 0), (0, 0), (0, PROJ_WIDTH - QKV_WIDTH)))
        ext = jnp.concatenate([head, s3], axis=1).reshape(bs * SAMPLE_CHUNK, PROJ_WIDTH)
        o_ext, delta_s = _delta_sample(ext, state_delta, delta_s, conv_w, alog, dtb, onorm, l, 16,
                                       SAMPLE_CHUNK - ts)
        o_dn = o_ext.reshape(bs, SAMPLE_CHUNK, DN_WIDTH)[:, SAMPLE_CHUNK - ts:].reshape(bs * ts, DN_WIDTH)
        o_ssm, h_fin = _ssm(proj, COL_SSM, h0_s[l], bmat, lam_re, lam_im, cre, cim, dskip, glu_w, glu_b, l, ts)
        pool_u = s3[:, :, COL_POOL:COL_AB]
        pext = jnp.concatenate([jnp.zeros((bs, 1, POOL_WIDTH), F32), state_pool[l], pool_u,
                                jnp.zeros((bs, pool_group - pool_first - ts, POOL_WIDTH), F32)], axis=1)
        y_ext = _pool_sample(pext.reshape(bs * pool_group, POOL_WIDTH), pool_bd, pscale, l, pool_group,
                             pool_first, PAST_LEN)
        o_pool = y_ext.reshape(bs, pool_group, POOL_WIDTH)[:, pool_first:pool_first + ts].reshape(bs * ts, POOL_WIDTH)
        xs = _post_mix(o_dn, o_ssm, o_pool, xs, w_out_b, nmo, nfp, wg_b, wu_b, wd_b, nfo, l, 256)
        outs_s.append((s3[:, ts - (DN_CONV - 1):, :QKV_WIDTH],
                       h_fin[:, :SSM_NS].reshape(bs, SSM_GROUPS, SSM_STATE),
                       h_fin[:, SSM_NS:].reshape(bs, SSM_GROUPS, SSM_STATE),
                       jnp.concatenate([state_pool[l][:, ts:], pool_u], axis=1)))

    stack = lambda outs, k: jnp.stack([o[k] for o in outs])
    return (xp.reshape(bp, tp, D_MODEL), xs.reshape(bs, ts, D_MODEL),
            stack(outs_p, 0), stack(outs_p, 1), stack(outs_p, 2), stack(outs_p, 3), stack(outs_p, 4),
            delta_s, stack(outs_s, 0), stack(outs_s, 1), stack(outs_s, 2), stack(outs_s, 3))
```

```python
import functools
import math

import jax
import jax.numpy as jnp
from jax import lax
from jax.experimental import pallas as pl
from jax.experimental.pallas import tpu as pltpu

F32 = jnp.float32
BF16 = jnp.bfloat16
HIGHEST = lax.Precision.HIGHEST

D_MODEL = 1024
DN_HEADS = 4
DN_HEAD_DIM = 128
DN_WIDTH = DN_HEADS * DN_HEAD_DIM
DN_CONV = 4
QKV_WIDTH = 3 * DN_WIDTH
SSM_WIDTH = 256
SSM_GROUP = 16
SSM_GROUPS = 16
SSM_STATE = 64
SSM_NS = SSM_GROUPS * SSM_STATE
POOL_WIDTH = 256
POOL_WINDOWS = (2, 4, 8, 16)
POOL_GROUP = 64
POOL_BUF = 15
D_FF = 2816
EPS = 1e-6
PAST_LEN = 16384

LANES = 128
SUBLANES = 8

COL_GATE = QKV_WIDTH
COL_SSM = COL_GATE + DN_WIDTH
COL_POOL = COL_SSM + SSM_WIDTH
COL_AB = COL_POOL + POOL_WIDTH
PROJ_WIDTH = COL_AB + LANES
REST_GATE = 0
REST_SSM = COL_SSM - COL_GATE
REST_POOL = COL_POOL - COL_GATE
REST_AB = COL_AB - COL_GATE
REST_WIDTH = PROJ_WIDTH - COL_GATE

_OFF_A = QKV_WIDTH
_OFF_G = _OFF_A + 2 * DN_HEADS

VMEM_LIMIT_BYTES = 56 * 1024 * 1024

DELTA_CHUNK = 64
SAMPLE_CHUNK = 8
DELTA_SUB = 16
POST_MIX_PARTS = 2

IN_PROJ_ROWS = 512
POST_MIX_ROWS = 512
SAMPLE_ROWS = 256
DELTA_SEQS, DELTA_GROUP, DELTA_TOKENS = 4, 2, 256
DELTA_SAMPLE_SEQS = 16
SSM_STEPS = 128
POOL_TOKENS = 1024
POOL_SAMPLE_GROUP = 24
SSM_PARTS = 4


def _cparams(n_axes):
    return pltpu.CompilerParams(dimension_semantics=("arbitrary",) * n_axes,
                                vmem_limit_bytes=VMEM_LIMIT_BYTES)


def _layer(a, l, single=False):
    zeros = (0,) * (a.ndim - 1)
    mode = dict(pipeline_mode=pl.Buffered(1)) if single else {}
    return pl.BlockSpec((None,) + a.shape[1:], lambda *_: (l,) + zeros, **mode)


def _mm(a, b):
    return jnp.dot(a, b, preferred_element_type=F32)


def _bdot(a, b):
    return jnp.dot(a.astype(BF16), b.astype(BF16), preferred_element_type=F32)


def _bdot_nt(a, b):
    return lax.dot_general(a.astype(BF16), b.astype(BF16), (((1,), (1,)), ((), ())),
                           preferred_element_type=F32)


def _bdot_tn(a, b):
    return lax.dot_general(a.astype(BF16), b.astype(BF16), (((0,), (0,)), ((), ())),
                           preferred_element_type=F32)


def _hdot(a, b):
    return jnp.dot(a, b, precision=HIGHEST, preferred_element_type=F32)


def _rms(x, w):
    return x * lax.rsqrt(jnp.mean(x * x, axis=-1, keepdims=True) + EPS) * w


def _silu(x):
    h = 0.5 * x
    return h * jnp.tanh(h) + h


def _in_proj_kernel(x_ref, nw_ref, wqkv_ref, wrest_ref, wab_ref, o_ref):
    h = _rms(x_ref[...], nw_ref[...]).astype(BF16)
    o_ref[:, :QKV_WIDTH] = _mm(h, wqkv_ref[...])
    o_ref[:, COL_GATE:COL_AB] = _mm(h, wrest_ref[...])
    o_ref[:, COL_AB:] = _mm(h, wab_ref[...])


def _in_proj(x, nw, wqkv, wrest, wab, l, tm):
    m = x.shape[0]
    return pl.pallas_call(
        _in_proj_kernel,
        grid=(m // tm,),
        in_specs=[pl.BlockSpec((tm, D_MODEL), lambda i: (i, 0)), _layer(nw, l),
                  _layer(wqkv, l, True), _layer(wrest, l, True), _layer(wab, l, True)],
        out_specs=pl.BlockSpec((tm, PROJ_WIDTH), lambda i: (i, 0)),
        out_shape=jax.ShapeDtypeStruct((m, PROJ_WIDTH), F32),
        compiler_params=_cparams(1),
        name="in_proj",
    )(x, nw, wqkv, wrest, wab)


def _in_proj_conv_kernel(x_ref, nw_ref, wqkv_ref, wrest_ref, wab_ref, cst_ref, cw_ref,
                         y_ref, o_ref, craw_ref, tail, *, tm, nt, nc):
    i = pl.program_id(0)

    @pl.when(i % nt == 0)
    def _():
        tail[...] = cst_ref[0]

    h = _rms(x_ref[...], nw_ref[...]).astype(BF16)
    for cc in range(QKV_WIDTH // nc):
        cs = slice(cc * nc, (cc + 1) * nc)
        r = _mm(h, wqkv_ref[:, cs])
        xfull = jnp.concatenate([tail[:, cs], r], axis=0)
        tail[:, cs] = r[tm - SUBLANES:, :]
        y = _silu(_conv4(xfull, cw_ref[:, cs], tm, SUBLANES))
        if cc * nc < 2 * DN_WIDTH:
            scale = DN_HEAD_DIM ** -0.5 if cc * nc < DN_WIDTH else 1.0
            parts = []
            for j in range(nc // DN_HEAD_DIM):
                yh = y[:, j * DN_HEAD_DIM:(j + 1) * DN_HEAD_DIM]
                parts.append(yh * lax.rsqrt(jnp.sum(yh * yh, axis=-1, keepdims=True) + EPS) * scale)
            y = jnp.concatenate(parts, axis=1)
        y_ref[:, cs] = y
    o_ref[:, :REST_AB] = _mm(h, wrest_ref[...])
    o_ref[:, REST_AB:] = _mm(h, wab_ref[...])
    craw_ref[0] = tail[...]


def _in_proj_conv(x, nw, wqkv, wrest, wab, cst, cw, l, tm, seq_len):
    m = x.shape[0]
    nt = seq_len // tm
    kern = functools.partial(_in_proj_conv_kernel, tm=tm, nt=nt, nc=2 * DN_HEAD_DIM)
    tail_spec = pl.BlockSpec((1, SUBLANES, QKV_WIDTH), lambda i: (i // nt, 0, 0))
    return pl.pallas_call(
        kern,
        grid=(m // tm,),
        in_specs=[pl.BlockSpec((tm, D_MODEL), lambda i: (i, 0)), _layer(nw, l),
                  _layer(wqkv, l, True), _layer(wrest, l, True), _layer(wab, l, True), tail_spec, _layer(cw, l)],
        out_specs=[pl.BlockSpec((tm, QKV_WIDTH), lambda i: (i, 0)),
                   pl.BlockSpec((tm, REST_WIDTH), lambda i: (i, 0)), tail_spec],
        out_shape=[jax.ShapeDtypeStruct((m, QKV_WIDTH), F32), jax.ShapeDtypeStruct((m, REST_WIDTH), F32),
                   jax.ShapeDtypeStruct((m // seq_len, SUBLANES, QKV_WIDTH), F32)],
        scratch_shapes=[pltpu.VMEM((SUBLANES, QKV_WIDTH), F32)],
        compiler_params=_cparams(1),
        name="in_proj_conv",
    )(x, nw, wqkv, wrest, wab, cst, cw)


def _post_mix_kernel(odn_ref, ossm_ref, opool_ref, x_ref, wo_ref, nmp_ref, nfp_ref, wg_ref, wu_ref, wd_ref,
                     nfo_ref, o_ref):
    tm = x_ref.shape[0]
    parts = [slice(p * tm // POST_MIX_PARTS, (p + 1) * tm // POST_MIX_PARTS) for p in range(POST_MIX_PARTS)]
    mix = [(_bdot(odn_ref[rs, :], wo_ref[0:DN_WIDTH, :])
            + _bdot(ossm_ref[rs, :], wo_ref[DN_WIDTH:DN_WIDTH + SSM_WIDTH, :])
            + _bdot(opool_ref[rs, :], wo_ref[DN_WIDTH + SSM_WIDTH:, :])) for rs in parts]
    x1 = [x_ref[rs, :] + _rms(m, nmp_ref[...]) for rs, m in zip(parts, mix)]
    h = [_rms(x, nfp_ref[...]).astype(BF16) for x in x1]
    g = [_mm(y, wg_ref[...]) for y in h]
    u = [_mm(y, wu_ref[...]) for y in h]
    f = [_bdot(_silu(a) * b, wd_ref[...]) for a, b in zip(g, u)]
    for rs, x, y in zip(parts, x1, f):
        o_ref[rs, :] = x + _rms(y, nfo_ref[...])


def _post_mix(odn, ossm, opool, x, wo, nmp, nfp, wg, wu, wd, nfo, l, tm):
    m = x.shape[0]
    row = lambda w: pl.BlockSpec((tm, w), lambda i: (i, 0))
    return pl.pallas_call(
        _post_mix_kernel,
        grid=(m // tm,),
        in_specs=[row(DN_WIDTH), row(SSM_WIDTH), row(POOL_WIDTH), row(D_MODEL),
                  _layer(wo, l, True), _layer(nmp, l), _layer(nfp, l), _layer(wg, l, True), _layer(wu, l, True),
                  _layer(wd, l, True), _layer(nfo, l)],
        out_specs=row(D_MODEL),
        out_shape=jax.ShapeDtypeStruct((m, D_MODEL), F32),
        compiler_params=_cparams(1),
        name="post_mix",
    )(odn, ossm, opool, x, wo, nmp, nfp, wg, wu, wd, nfo)


def _split2(x):
    hi = x.astype(BF16)
    return hi, (x - hi.astype(F32)).astype(BF16)


def _group_cumsum(x, group, reverse=False):
    rows = x.shape[0]
    pos = lax.broadcasted_iota(jnp.int32, x.shape, 0) % group
    d = 1
    while d < group:
        if reverse:
            x = x + jnp.where(pos + d < group, pltpu.roll(x, rows - d, 0), 0.0)
        else:
            x = x + jnp.where(pos >= d, pltpu.roll(x, d, 0), 0.0)
        d *= 2
    return x


def _chunk_masks(rows, blk):
    r = lax.broadcasted_iota(jnp.int32, (rows, rows), 0)
    s = lax.broadcasted_iota(jnp.int32, (rows, rows), 1)
    d = r - s
    if blk < rows:
        d = jnp.where(r // blk == s // blk, d, -1)
    sub = None
    if blk > DELTA_SUB:
        sub = r // DELTA_SUB == s // DELTA_SUB
    return d >= 0, d > 0, (r == s).astype(F32), sub


def _run(gen):
    try:
        while True:
            next(gen)
    except StopIteration as stop:
        return stop.value


def _run_interleaved(gens):
    results = [None] * len(gens)
    live = list(range(len(gens)))
    while live:
        for i in list(live):
            try:
                next(gens[i])
            except StopIteration as stop:
                results[i] = stop.value
                live.remove(i)
    return results


def _neumann(a, eye, index):
    t = [eye - x for x in a]
    p = a
    n = 2
    while n < index:
        p = [_bdot(x, x) for x in p]
        yield
        t = [x + _bdot(x, y) for x, y in zip(t, p)]
        yield
        n *= 2
    return t


def _unit_lower_inverse(a, eye, sub, index):
    if sub is None:
        return (yield from _neumann(a, eye, index))
    assert index <= 4 * DELTA_SUB
    d = [jnp.where(sub, x, 0.0) for x in a]
    td = yield from _neumann(d, eye, DELTA_SUB)
    n = [_bdot(t, x - y) for t, x, y in zip(td, a, d)]
    yield
    n2 = [_bdot(x, x) for x in n]
    yield
    m = [_bdot(eye - x, eye + y) for x, y in zip(n, n2)]
    yield
    t = [_bdot(x, t) for x, t in zip(m, td)]
    yield
    return t


def _delta_local(probs, masks, index):
    incl, strict, eye, sub = masks
    rows = probs[0][0].shape[0]
    kb = [k * beta for (_, k, _, beta, _, _, _) in probs]
    kq = [_bdot_nt(jnp.concatenate([x, q], axis=0), k) for x, (q, k, _, _, _, _, _) in zip(kb, probs)]
    yield
    decay = [jnp.where(incl, jnp.exp(jnp.where(incl, gam - grow, 0.0)), 0.0) for (_, _, _, _, gam, _, grow) in probs]
    a = [jnp.where(strict, x[:rows] * d, 0.0) for x, d in zip(kq, decay)]
    qk = [x[rows:] * d for x, d in zip(kq, decay)]
    egam = [jnp.exp(p[4]) for p in probs]
    rhs = [jnp.concatenate([p[2] * p[3], x * e], axis=1) for p, x, e in zip(probs, kb, egam)]
    t = yield from _unit_lower_inverse(a, eye, sub, index)
    sol = [_bdot(x, r) for x, r in zip(t, rhs)]
    yield
    a_sp = [_split2(x) for x in a]
    s_sp = [_split2(x) for x in sol]
    asol = [_mm(ah, sh) + _mm(ah, sl) + _mm(al, sh) for (ah, al), (sh, sl) in zip(a_sp, s_sp)]
    yield
    resid = [r - s - x for r, s, x in zip(rhs, sol, asol)]
    sol = [s + _bdot(x, r) for s, x, r in zip(sol, t, resid)]
    yield
    return [(s[:, :DN_HEAD_DIM], s[:, DN_HEAD_DIM:], x, p[0] * e, p[1] * jnp.exp(p[5]))
            for s, x, p, e in zip(sol, qk, probs, egam)]


def _delta_scan(local, s, glast, nchunk, c):
    chains = range(len(s))
    outs = [[] for _ in chains]
    for j in range(nchunk):
        ks_ = [_bdot(jnp.concatenate([local[n][j][1], local[n][j][3]], axis=0), s[n]) for n in chains]
        yield
        u = [local[n][j][0] - ks_[n][:c] for n in chains]
        for n in chains:
            outs[n].append(ks_[n][c:] + _bdot(local[n][j][2], u[n]))
        yield
        s = [s[n] * jnp.exp(glast[n][j]) + _bdot_tn(local[n][j][4], u[n]) for n in chains]
        yield
    return s, [jnp.concatenate(o, axis=0) for o in outs]


def _delta_front(y, ab, alog, dtb):
    y = _silu(y)
    qs, ks = [], []
    for h in range(DN_HEADS):
        qh = y[:, h * DN_HEAD_DIM:(h + 1) * DN_HEAD_DIM]
        kh = y[:, DN_WIDTH + h * DN_HEAD_DIM:DN_WIDTH + (h + 1) * DN_HEAD_DIM]
        qs.append(qh * lax.rsqrt(jnp.sum(qh * qh, axis=-1, keepdims=True) + EPS) * (DN_HEAD_DIM ** -0.5))
        ks.append(kh * lax.rsqrt(jnp.sum(kh * kh, axis=-1, keepdims=True) + EPS))
    v = y[:, 2 * DN_WIDTH:]
    logg, beta = _delta_gates(ab, alog, dtb)
    return qs, ks, v, logg, beta


def _delta_gates(ab, alog, dtb):
    z = ab + dtb
    softplus = jnp.maximum(z, 0.0) + jnp.log1p(jnp.exp(-jnp.abs(z)))
    return -jnp.exp(alog) * softplus, jax.nn.sigmoid(ab)


def _delta_out(o, gate, onorm):
    return o * lax.rsqrt(jnp.mean(o * o, axis=-1, keepdims=True) + EPS) * onorm * _silu(gate)


def _conv4(xfull, w, rows, off):
    z = xfull * w[0:1]
    z = xfull * w[1:2] + pltpu.roll(z, 1, 0)
    z = xfull * w[2:3] + pltpu.roll(z, 1, 0)
    return (xfull * w[3:4] + pltpu.roll(z, 1, 0))[off:off + rows]


def _delta_prompt_kernel(qkv_ref, ab_ref, gate_ref, s0_ref, alog_ref, dtb_ref, onorm_ref,
                         o_ref, sfin_ref, s_scr, *, nseq, group, tb, c):
    i = pl.program_id(1)

    @pl.when(i == 0)
    def _():
        s_scr[...] = s0_ref[...]

    masks = _chunk_masks(c, c)
    nchunk = tb // c
    heads = range(DN_HEADS)
    hsl = [slice(h * DN_HEAD_DIM, (h + 1) * DN_HEAD_DIM) for h in heads]

    def chunk_problems(chains):
        probs, glast = [], []
        per_seq = {}
        for r, h in chains:
            if r not in per_seq:
                logg, beta = _delta_gates(ab_ref[r], alog_ref[...], dtb_ref[...])
                gam = _group_cumsum(logg, c)
                gexc = _group_cumsum(logg, c, reverse=True) - logg
                per_seq[r] = (beta, gam, gexc, gam.T)
            beta, gam, gexc, gam_t = per_seq[r]
            ksl = slice(DN_WIDTH + h * DN_HEAD_DIM, DN_WIDTH + (h + 1) * DN_HEAD_DIM)
            vsl = slice(2 * DN_WIDTH + h * DN_HEAD_DIM, 2 * DN_WIDTH + (h + 1) * DN_HEAD_DIM)
            for j in range(nchunk):
                rs = slice(j * c, (j + 1) * c)
                probs.append((qkv_ref[r, rs, hsl[h]], qkv_ref[r, rs, ksl], qkv_ref[r, rs, vsl],
                              beta[rs, DN_HEADS + h:DN_HEADS + h + 1], gam[rs, h:h + 1], gexc[rs, h:h + 1],
                              gam_t[h:h + 1, rs]))
            glast.append([gam[(j + 1) * c - 1:(j + 1) * c, h:h + 1] for j in range(nchunk)])
        return probs, glast

    def finish(chains, result):
        s, outs = result
        for n, (r, h) in enumerate(chains):
            s_scr[r, h] = s[n]
            o = _delta_out(outs[n], gate_ref[r, :, hsl[h]], onorm_ref[...])
            o_ref[r, :, hsl[h]] = o.astype(o_ref.dtype)

    groups = [[(r, h) for r in range(g, g + group) for h in heads] for g in range(0, nseq, group)]
    scan, prev = None, None
    for chains in groups:
        probs, glast = chunk_problems(chains)
        local_gen = _delta_local(probs, masks, c)
        if scan is None:
            flat = _run(local_gen)
        else:
            flat, result = _run_interleaved([local_gen, scan])
            finish(prev, result)
        local = [flat[n * nchunk:(n + 1) * nchunk] for n in range(len(chains))]
        scan = _delta_scan(local, [s_scr[r, h] for r, h in chains], glast, nchunk, c)
        prev = chains
    finish(prev, _run(scan))
    sfin_ref[...] = s_scr[...]


def _delta_prompt(y3, rest3, s0, alog, dtb, onorm, l, nseq, group, tb):
    nb, t, _ = y3.shape
    kern = functools.partial(_delta_prompt_kernel, nseq=nseq, group=group, tb=tb, c=DELTA_CHUNK)
    sspec = pl.BlockSpec((nseq, DN_HEADS, DN_HEAD_DIM, DN_HEAD_DIM), lambda b, i: (b, 0, 0, 0))
    return pl.pallas_call(
        kern,
        grid=(nb // nseq, t // tb),
        in_specs=[pl.BlockSpec((nseq, tb, QKV_WIDTH), lambda b, i: (b, i, 0)),
                  pl.BlockSpec((nseq, tb, LANES), lambda b, i: (b, i, REST_AB // LANES)),
                  pl.BlockSpec((nseq, tb, DN_WIDTH), lambda b, i: (b, i, REST_GATE // DN_WIDTH)),
                  sspec, _layer(alog, l), _layer(dtb, l), _layer(onorm, l)],
        out_specs=[pl.BlockSpec((nseq, tb, DN_WIDTH), lambda b, i: (b, i, 0)), sspec],
        out_shape=[jax.ShapeDtypeStruct((nb, t, DN_WIDTH), BF16),
                   jax.ShapeDtypeStruct((nb, DN_HEADS, DN_HEAD_DIM, DN_HEAD_DIM), F32)],
        scratch_shapes=[pltpu.VMEM((nseq, DN_HEADS, DN_HEAD_DIM, DN_HEAD_DIM), F32)],
        compiler_params=_cparams(2),
        name="delta_prompt",
    )(y3, rest3, rest3, s0, alog, dtb, onorm)


def _delta_sample_kernel(qkv_ref, ab_ref, gate_ref, s0_ref, cw_ref, alog_ref, dtb_ref, onorm_ref, *rest,
                         nseq, c, first):
    o_ref, snew_ref = rest[-2:]
    rows = nseq * c
    x = qkv_ref[...]
    y = _conv4(x, cw_ref[...], rows, 0)
    qs, ks, v, logg, beta = _delta_front(y, ab_ref[...], alog_ref[...], dtb_ref[...])
    valid = (lax.broadcasted_iota(jnp.int32, (rows, LANES), 0) % c) >= first
    logg = jnp.where(valid, logg, 0.0)
    beta = jnp.where(valid, beta, 0.0)
    gam = _group_cumsum(logg, c)
    gexc = _group_cumsum(logg, c, reverse=True) - logg
    gam_t = gam.T
    masks = _chunk_masks(rows, c)
    gate = gate_ref[...]
    heads = range(DN_HEADS)
    hsl = [slice(h * DN_HEAD_DIM, (h + 1) * DN_HEAD_DIM) for h in heads]
    local = _run(_delta_local([(qs[h], ks[h], v[:, hsl[h]], beta[:, DN_HEADS + h:DN_HEADS + h + 1],
                                gam[:, h:h + 1], gexc[:, h:h + 1], gam_t[h:h + 1, :]) for h in heads], masks, c))
    pairs = [(b, h) for b in range(nseq) for h in heads]
    rsl = [slice(b * c, (b + 1) * c) for b in range(nseq)]
    ks_ = {(b, h): _bdot(jnp.concatenate([local[h][1][rsl[b]], local[h][3][rsl[b]]], axis=0), s0_ref[b, h])
           for b, h in pairs}
    u = {(b, h): local[h][0][rsl[b]] - ks_[b, h][:c] for b, h in pairs}
    for b, h in pairs:
        last = (b + 1) * c - 1
        snew_ref[b, h] = (s0_ref[b, h] * jnp.exp(gam[last:last + 1, h:h + 1])
                          + _bdot_tn(local[h][4][rsl[b]], u[b, h]))
    for h in heads:
        o = (jnp.concatenate([ks_[b, h][c:] for b in range(nseq)], axis=0)
             + _bdot(local[h][2], jnp.concatenate([u[b, h] for b in range(nseq)], axis=0)))
        o_ref[:, hsl[h]] = _delta_out(o, gate[:, hsl[h]], onorm_ref[...]).astype(o_ref.dtype)


def _delta_sample(ext, s0_all, new_all, cw, alog, dtb, onorm, l, nseq, first):
    c = SAMPLE_CHUNK
    nb = s0_all.shape[1]
    rows = nseq * c
    kern = functools.partial(_delta_sample_kernel, nseq=nseq, c=c, first=first)
    sspec = pl.BlockSpec((None, nseq, DN_HEADS, DN_HEAD_DIM, DN_HEAD_DIM), lambda i: (l, i, 0, 0, 0))
    in_specs = [pl.BlockSpec((rows, QKV_WIDTH), lambda i: (i, 0)),
                pl.BlockSpec((rows, LANES), lambda i: (i, COL_AB // LANES)),
                pl.BlockSpec((rows, DN_WIDTH), lambda i: (i, COL_GATE // DN_WIDTH)),
                sspec, _layer(cw, l), _layer(alog, l), _layer(dtb, l), _layer(onorm, l)]
    args = [ext, ext, ext, s0_all, cw, alog, dtb, onorm]
    aliases = {}
    if new_all is not None:
        in_specs.append(pl.BlockSpec(memory_space=pl.ANY))
        args.append(new_all)
        aliases = {len(args) - 1: 1}
    return pl.pallas_call(
        kern,
        grid=(nb // nseq,),
        in_specs=in_specs,
        out_specs=[pl.BlockSpec((rows, DN_WIDTH), lambda i: (i, 0)), sspec],
        out_shape=[jax.ShapeDtypeStruct((nb * c, DN_WIDTH), BF16),
                   jax.ShapeDtypeStruct(s0_all.shape, F32)],
        input_output_aliases=aliases,
        compiler_params=_cparams(1),
        name="delta_sample",
    )(*args)


def _ssm_prep_kernel(are_ref, aim_ref, dt_ref, bre_ref, bim_ref, lre_ref, lim_ref, bmat_ref):
    ar, ai, dt = are_ref[0], aim_ref[0], jnp.exp(dt_ref[0])
    mag = jnp.exp(ar * dt)
    lr = mag * jnp.cos(ai * dt)
    li = mag * jnp.sin(ai * dt)
    lre_ref[0] = lr
    lim_ref[0] = li
    den = ar * ar + ai * ai
    fr = ((lr - 1.0) * ar + li * ai) / den
    fi = (li * ar - (lr - 1.0) * ai) / den
    br, bi = bre_ref[0], bim_ref[0]
    bbr = fr * br - fi * bi
    bbi = fr * bi + fi * br
    lane_group = lax.broadcasted_iota(jnp.int32, (SSM_GROUP, SSM_NS), 1) // SSM_STATE
    for g in range(SSM_GROUPS):
        m = lane_group == g
        bmat_ref[0, g * SSM_GROUP:(g + 1) * SSM_GROUP, 0:SSM_NS] = jnp.where(m, bbr, 0.0)
        bmat_ref[0, g * SSM_GROUP:(g + 1) * SSM_GROUP, SSM_NS:] = jnp.where(m, bbi, 0.0)


def _ssm_prep(are, aim, dt, bre, bim):
    depth = are.shape[0]
    vec = pl.BlockSpec((1, 1, SSM_NS), lambda l: (l, 0, 0))
    mat = pl.BlockSpec((1, SSM_GROUP, SSM_NS), lambda l: (l, 0, 0))
    return pl.pallas_call(
        _ssm_prep_kernel,
        grid=(depth,),
        in_specs=[vec, vec, vec, mat, mat],
        out_specs=[vec, vec, pl.BlockSpec((1, SSM_WIDTH, 2 * SSM_NS), lambda l: (l, 0, 0))],
        out_shape=[jax.ShapeDtypeStruct((depth, 1, SSM_NS), F32), jax.ShapeDtypeStruct((depth, 1, SSM_NS), F32),
                   jax.ShapeDtypeStruct((depth, SSM_WIDTH, 2 * SSM_NS), F32)],
        compiler_params=_cparams(1),
        name="ssm_prep",
    )(are, aim, dt, bre, bim)


def _gelu_tanh(x):
    return 0.5 * x * (1.0 + jnp.tanh(math.sqrt(2.0 / math.pi) * (x + 0.044715 * (x * x * x))))


def _ssm_kernel(ulo_ref, uhi_ref, h0_ref, bmat_ref, lre_ref, lim_ref, cre_ref, cim_ref, d_ref, gw_ref, gb_ref,
                y_ref, hl_ref, buf, h_scr, tm_scr, bm_scr, *, nb, nt, nparts, exact_in):
    i = pl.program_id(0)
    halves = range(SSM_WIDTH // LANES)
    lanes = [slice(s * LANES, (s + 1) * LANES) for s in halves]

    @pl.when(i == 0)
    def _():
        h_scr[...] = h0_ref[...]

    for s, u_ref in enumerate((ulo_ref, uhi_ref)):
        if nb <= nt:
            for b in range(nb):
                tm_scr[s, pl.ds(b, nt, stride=nb), :] = u_ref[b]
        else:
            for t in range(nt):
                tm_scr[s, t * nb:(t + 1) * nb, :] = u_ref[pl.ds(t, nb, stride=nt), :]
    u = jnp.concatenate([tm_scr[s] for s in halves], axis=1)
    lr = jnp.broadcast_to(lre_ref[...], (SUBLANES, SSM_NS))
    li = jnp.broadcast_to(lim_ref[...], (SUBLANES, SSM_NS))
    pt = nt // nparts
    prow = [slice(p * pt * nb, (p + 1) * pt * nb) for p in range(nparts)]
    groups = range(nb // SUBLANES)
    h = [h_scr[g * SUBLANES:(g + 1) * SUBLANES, :] for g in groups]

    bmat = bmat_ref[...] if exact_in else bmat_ref[...].astype(BF16)

    def in_map(p):
        up = u[prow[p]]
        buf[prow[p], :] = _hdot(up, bmat) if exact_in else _bdot(up, bmat)

    def recur(p):
        for t in range(p * pt, (p + 1) * pt):
            for g in groups:
                rs = slice(t * nb + g * SUBLANES, t * nb + (g + 1) * SUBLANES)
                bu = buf[rs, :]
                hre, him = h[g][:, :SSM_NS], h[g][:, SSM_NS:]
                h[g] = jnp.concatenate([lr * hre - li * him + bu[:, :SSM_NS],
                                        lr * him + li * hre + bu[:, SSM_NS:]], axis=1)
                buf[rs, :] = h[g]

    def out_map(p):
        rs = prow[p]
        y = _bdot(buf[rs, :SSM_NS], cre_ref[...]) - _bdot(buf[rs, SSM_NS:], cim_ref[...]) + d_ref[...] * u[rs]
        y = _gelu_tanh(y)
        y = y * jax.nn.sigmoid(_bdot(y, gw_ref[...]) + gb_ref[...])
        for s in halves:
            tm_scr[s, rs, :] = y[:, lanes[s]]

    in_map(0)
    for p in range(nparts):
        if p + 1 < nparts:
            in_map(p + 1)
        recur(p)
        if p >= 1:
            out_map(p - 1)
    out_map(nparts - 1)
    for g in groups:
        h_scr[g * SUBLANES:(g + 1) * SUBLANES, :] = h[g]
    hl_ref[...] = h_scr[...]
    for s in halves:
        if nb <= nt:
            for b in range(nb):
                y_ref[b, :, lanes[s]] = tm_scr[s, pl.ds(b, nt, stride=nb), :].astype(y_ref.dtype)
        else:
            for t in range(nt):
                bm_scr[s, pl.ds(t, nb, stride=nt), :] = tm_scr[s, t * nb:(t + 1) * nb, :]
            y_ref[:, lanes[s]] = bm_scr[s].astype(y_ref.dtype)


def _ssm(proj, col, h0, bmat, lre, lim, cre, cim, dskip, gw, gb, l, nt):
    nb = h0.shape[0]
    rows = nt * nb
    lo = col // LANES
    if proj.ndim == 3:
        t = proj.shape[1]
        grid = (t // nt,)
        u_spec = lambda c: pl.BlockSpec((nb, nt, LANES), lambda i: (0, i, c))
        y_spec = pl.BlockSpec((nb, nt, SSM_WIDTH), lambda i: (0, i, 0))
        y_shape = (nb, t, SSM_WIDTH)
    else:
        grid = (1,)
        u_spec = lambda c: pl.BlockSpec((rows, LANES), lambda i: (0, c))
        y_spec = pl.BlockSpec((rows, SSM_WIDTH), lambda i: (0, 0))
        y_shape = (rows, SSM_WIDTH)
    kern = functools.partial(_ssm_kernel, nb=nb, nt=nt, nparts=SSM_PARTS if proj.ndim == 3 else 1,
                             exact_in=proj.ndim == 2)
    const = lambda a: pl.BlockSpec(a.shape, lambda i: (0,) * a.ndim)
    return pl.pallas_call(
        kern,
        grid=grid,
        in_specs=[u_spec(lo), u_spec(lo + 1), const(h0), _layer(bmat, l), _layer(lre, l), _layer(lim, l),
                  _layer(cre, l), _layer(cim, l), _layer(dskip, l), _layer(gw, l), _layer(gb, l)],
        out_specs=[y_spec, const(h0)],
        out_shape=[jax.ShapeDtypeStruct(y_shape, BF16), jax.ShapeDtypeStruct(h0.shape, F32)],
        scratch_shapes=[pltpu.VMEM((rows, 2 * SSM_NS), F32), pltpu.VMEM(h0.shape, F32),
                        pltpu.VMEM((SSM_WIDTH // LANES, rows, LANES), F32),
                        pltpu.VMEM((SSM_WIDTH // LANES, rows, LANES), F32)],
        compiler_params=_cparams(1),
        name="ssm",
    )(proj, proj, h0, bmat, lre, lim, cre, cim, dskip, gw, gb)


def _pool_windows(xfull):
    s2 = xfull + pltpu.roll(xfull, 1, 0)
    s4 = s2 + pltpu.roll(s2, 2, 0)
    s8 = s4 + pltpu.roll(s4, 4, 0)
    s16 = s8 + pltpu.roll(s8, 8, 0)
    return s2, s4, s8, s16


def _pool_mix(sums, x, pos, w_ref, scale_ref):
    lane = lax.broadcasted_iota(jnp.int32, (1, POOL_WIDTH), 1) // POOL_GROUP
    win = None
    for gidx in reversed(range(len(POOL_WINDOWS))):
        cnt = jnp.minimum(pos + 1, POOL_WINDOWS[gidx]).astype(F32)
        term = sums[gidx] / cnt
        win = term if win is None else jnp.where(lane == gidx, term, win)
    r = win - x
    return _bdot(r, w_ref[...]) * scale_ref[...]


def _pool_prompt_kernel(u_ref, st_ref, w_ref, scale_ref, y_ref, tail, *, tb, pos0):
    i = pl.program_id(1)
    halo = 2 * SUBLANES

    @pl.when(i == 0)
    def _():
        tail[...] = st_ref[0]

    x = u_ref[0]
    xfull = jnp.concatenate([tail[...], x], axis=0)
    tail[...] = x[tb - halo:, :]
    sums = [s[halo:] for s in _pool_windows(xfull)]
    pos = pos0 + i * tb + lax.broadcasted_iota(jnp.int32, (tb, 1), 0)
    y_ref[0] = _pool_mix(sums, x, pos, w_ref, scale_ref).astype(y_ref.dtype)


def _pool_prompt(p3, col, st, wbd, scale, l, tb, pos0):
    nb, t, _ = p3.shape
    kern = functools.partial(_pool_prompt_kernel, tb=tb, pos0=pos0)
    return pl.pallas_call(
        kern,
        grid=(nb, t // tb),
        in_specs=[pl.BlockSpec((1, tb, POOL_WIDTH), lambda b, i: (b, i, col // POOL_WIDTH)),
                  pl.BlockSpec((1, 2 * SUBLANES, POOL_WIDTH), lambda b, i: (b, 0, 0)),
                  _layer(wbd, l), _layer(scale, l)],
        out_specs=pl.BlockSpec((1, tb, POOL_WIDTH), lambda b, i: (b, i, 0)),
        out_shape=jax.ShapeDtypeStruct((nb, t, POOL_WIDTH), BF16),
        scratch_shapes=[pltpu.VMEM((2 * SUBLANES, POOL_WIDTH), F32)],
        compiler_params=_cparams(2),
        name="pool_prompt",
    )(p3, st, wbd, scale)


def _pool_sample_kernel(x_ref, w_ref, scale_ref, y_ref, *, group, first, pos0):
    x = x_ref[...]
    rows = x.shape[0]
    sums = _pool_windows(x)
    pos = pos0 + (lax.broadcasted_iota(jnp.int32, (rows, 1), 0) % group) - first
    y_ref[...] = _pool_mix(sums, x, jnp.maximum(pos, 0), w_ref, scale_ref).astype(y_ref.dtype)


def _pool_sample(ext, wbd, scale, l, group, first, pos0):
    kern = functools.partial(_pool_sample_kernel, group=group, first=first, pos0=pos0)
    const = lambda a: pl.BlockSpec(a.shape, lambda i: (0,) * a.ndim)
    return pl.pallas_call(
        kern,
        grid=(1,),
        in_specs=[const(ext), _layer(wbd, l), _layer(scale, l)],
        out_specs=const(ext),
        out_shape=jax.ShapeDtypeStruct(ext.shape, BF16),
        compiler_params=_cparams(1),
        name="pool_sample",
    )(ext, wbd, scale)


def _block_diag(blocks):
    g, r, c = blocks.shape
    eye = jnp.eye(g, dtype=blocks.dtype)
    return (eye[:, None, :, None] * blocks[:, :, None, :]).reshape(g * r, g * c)


def kernel(x_prompt, x_sample, state_delta, state_conv, state_ssm_re, state_ssm_im, state_pool, norm_mix_pre, norm_mix_post, norm_ffn_pre, norm_ffn_post, w_in, conv_w, dn_a_log, dn_dt_bias, dn_out_norm, ssm_a_re, ssm_a_im, ssm_log_dt, ssm_b_re, ssm_b_im, ssm_c_re, ssm_c_im, ssm_d, ssm_glu_w, ssm_glu_b, pool_w, pool_scale, w_out, ffn_w_gate, ffn_w_up, ffn_w_down):
    depth = w_in.shape[0]
    bp, tp, _ = x_prompt.shape
    bs, ts, _ = x_sample.shape

    w_qkv = w_in[:, :, :QKV_WIDTH].astype(BF16)
    w_rest = w_in[:, :, _OFF_G:].astype(BF16)
    w_ab = jnp.pad(w_in[:, :, _OFF_A:_OFF_G], ((0, 0), (0, 0), (0, LANES - 2 * DN_HEADS))).astype(BF16)
    w_out_b = w_out.astype(BF16)
    wg_b, wu_b, wd_b = ffn_w_gate.astype(BF16), ffn_w_up.astype(BF16), ffn_w_down.astype(BF16)
    row = lambda a: a.reshape(depth, 1, -1)
    nmp, nmo, nfp, nfo = row(norm_mix_pre), row(norm_mix_post), row(norm_ffn_pre), row(norm_ffn_post)
    alog = jnp.pad(dn_a_log, ((0, 0), (0, LANES - DN_HEADS))).reshape(depth, 1, LANES)
    dtb = jnp.pad(dn_dt_bias, ((0, 0), (0, LANES - DN_HEADS))).reshape(depth, 1, LANES)
    onorm = row(dn_out_norm)
    dt_full = jnp.repeat(ssm_log_dt, SSM_STATE, axis=1).reshape(depth, 1, SSM_NS)
    b_t = lambda b: jnp.transpose(b, (0, 3, 1, 2)).reshape(depth, SSM_GROUP, SSM_NS)
    lam_re, lam_im, bmat = _ssm_prep(ssm_a_re.reshape(depth, 1, SSM_NS), ssm_a_im.reshape(depth, 1, SSM_NS),
                                     dt_full, b_t(ssm_b_re), b_t(ssm_b_im))
    c_bd = lambda cc: jax.vmap(_block_diag)(jnp.transpose(cc, (0, 1, 3, 2))).astype(BF16)
    cre, cim = c_bd(ssm_c_re), c_bd(ssm_c_im)
    dskip, glu_b = row(ssm_d), row(ssm_glu_b)
    glu_w = ssm_glu_w.astype(BF16)
    pool_bd = jax.vmap(_block_diag)(pool_w).astype(BF16)
    pscale = row(pool_scale)

    xp = x_prompt.reshape(bp * tp, D_MODEL)
    xs = x_sample.reshape(bs * ts, D_MODEL)
    zero_conv = jnp.zeros((bp, SUBLANES, QKV_WIDTH), F32)
    zero_delta = jnp.zeros((bp, DN_HEADS, DN_HEAD_DIM, DN_HEAD_DIM), F32)
    zero_h = jnp.zeros((bp, 2 * SSM_NS), F32)
    zero_pool = jnp.zeros((bp, 2 * SUBLANES, POOL_WIDTH), F32)
    pad_rows = SAMPLE_CHUNK - ts - (DN_CONV - 1)
    pool_group = POOL_SAMPLE_GROUP
    pool_first = 1 + POOL_BUF
    h0_s = jnp.concatenate([state_ssm_re.reshape(depth, bs, SSM_NS), state_ssm_im.reshape(depth, bs, SSM_NS)],
                           axis=2)

    outs_p, outs_s = [], []
    delta_s = jnp.zeros(state_delta.shape, F32)
    for l in range(depth):
        y, rest, craw = _in_proj_conv(xp, nmp, w_qkv, w_rest, w_ab, zero_conv, conv_w, l, IN_PROJ_ROWS, tp)
        y3 = y.reshape(bp, tp, QKV_WIDTH)
        r3 = rest.reshape(bp, tp, REST_WIDTH)
        o_dn, delta_new = _delta_prompt(y3, r3, zero_delta, alog, dtb, onorm, l, DELTA_SEQS, DELTA_GROUP,
                                        DELTA_TOKENS)
        o_ssm, h_fin = _ssm(r3, REST_SSM, zero_h, bmat, lam_re, lam_im, cre, cim, dskip, glu_w, glu_b, l,
                            SSM_STEPS)
        o_pool = _pool_prompt(r3, REST_POOL, zero_pool, pool_bd, pscale, l, POOL_TOKENS, 0)
        xp = _post_mix(o_dn.reshape(bp * tp, DN_WIDTH), o_ssm.reshape(bp * tp, SSM_WIDTH),
                       o_pool.reshape(bp * tp, POOL_WIDTH), xp,
                       w_out_b, nmo, nfp, wg_b, wu_b, wd_b, nfo, l, POST_MIX_ROWS)
        outs_p.append((delta_new, craw[:, SUBLANES - (DN_CONV - 1):, :],
                       h_fin[:, :SSM_NS].reshape(bp, SSM_GROUPS, SSM_STATE),
                       h_fin[:, SSM_NS:].reshape(bp, SSM_GROUPS, SSM_STATE),
                       r3[:, tp - POOL_BUF:, REST_POOL:REST_AB]))

        proj = _in_proj(xs, nmp, w_qkv, w_rest, w_ab, l, SAMPLE_ROWS)
        s3 = proj.reshape(bs, ts, PROJ_WIDTH)
        head = jnp.concatenate([jnp.zeros((bs, pad_rows, QKV_WIDTH), F32), state_conv[l]], axis=1)
        head = jnp.pad(head, ((0, 0), (0, 0), (0, PROJ_WIDTH - QKV_WIDTH)))
        ext = jnp.concatenate([head, s3], axis=1).reshape(bs * SAMPLE_CHUNK, PROJ_WIDTH)
        o_ext, delta_s = _delta_sample(ext, state_delta, delta_s, conv_w, alog, dtb, onorm, l,
                                       DELTA_SAMPLE_SEQS,
                                       SAMPLE_CHUNK - ts)
        o_dn = o_ext.reshape(bs, SAMPLE_CHUNK, DN_WIDTH)[:, SAMPLE_CHUNK - ts:].reshape(bs * ts, DN_WIDTH)
        o_ssm, h_fin = _ssm(proj, COL_SSM, h0_s[l], bmat, lam_re, lam_im, cre, cim, dskip, glu_w, glu_b, l, ts)
        pool_u = s3[:, :, COL_POOL:COL_AB]
        pext = jnp.concatenate([jnp.zeros((bs, 1, POOL_WIDTH), F32), state_pool[l], pool_u,
                                jnp.zeros((bs, pool_group - pool_first - ts, POOL_WIDTH), F32)], axis=1)
        y_ext = _pool_sample(pext.reshape(bs * pool_group, POOL_WIDTH), pool_bd, pscale, l, pool_group,
                             pool_first, PAST_LEN)
        o_pool = y_ext.reshape(bs, pool_group, POOL_WIDTH)[:, pool_first:pool_first + ts].reshape(bs * ts, POOL_WIDTH)
        xs = _post_mix(o_dn, o_ssm, o_pool, xs, w_out_b, nmo, nfp, wg_b, wu_b, wd_b, nfo, l, SAMPLE_ROWS)
        outs_s.append((s3[:, ts - (DN_CONV - 1):, :QKV_WIDTH],
                       h_fin[:, :SSM_NS].reshape(bs, SSM_GROUPS, SSM_STATE),
                       h_fin[:, SSM_NS:].reshape(bs, SSM_GROUPS, SSM_STATE),
                       jnp.concatenate([state_pool[l][:, ts:], pool_u], axis=1)))

    stack = lambda outs, k: jnp.stack([o[k] for o in outs])
    return (xp.reshape(bp, tp, D_MODEL), xs.reshape(bs, ts, D_MODEL),
            stack(outs_p, 0), stack(outs_p, 1), stack(outs_p, 2), stack(outs_p, 3), stack(outs_p, 4),
            delta_s, stack(outs_s, 0), stack(outs_s, 1), stack(outs_s, 2), stack(outs_s, 3))
```

```python
import functools
import math

import jax
import jax.numpy as jnp
from jax import lax
from jax.experimental import pallas as pl
from jax.experimental.pallas import tpu as pltpu

F32 = jnp.float32
BF16 = jnp.bfloat16
HIGHEST = lax.Precision.HIGHEST

D_MODEL = 1024
DN_HEADS = 4
DN_HEAD_DIM = 128
DN_WIDTH = DN_HEADS * DN_HEAD_DIM
DN_CONV = 4
QKV_WIDTH = 3 * DN_WIDTH
SSM_WIDTH = 256
SSM_GROUP = 16
SSM_GROUPS = 16
SSM_STATE = 64
SSM_NS = SSM_GROUPS * SSM_STATE
POOL_WIDTH = 256
POOL_WINDOWS = (2, 4, 8, 16)
POOL_GROUP = 64
POOL_BUF = 15
D_FF = 2816
EPS = 1e-6
PAST_LEN = 16384

LANES = 128
SUBLANES = 8

COL_GATE = QKV_WIDTH
COL_SSM = COL_GATE + DN_WIDTH
COL_POOL = COL_SSM + SSM_WIDTH
COL_AB = COL_POOL + POOL_WIDTH
PROJ_WIDTH = COL_AB + LANES
REST_GATE = 0
REST_SSM = COL_SSM - COL_GATE
REST_POOL = COL_POOL - COL_GATE
REST_AB = COL_AB - COL_GATE
REST_WIDTH = PROJ_WIDTH - COL_GATE

_OFF_A = QKV_WIDTH
_OFF_G = _OFF_A + 2 * DN_HEADS

VMEM_LIMIT_BYTES = 56 * 1024 * 1024

DELTA_CHUNK = 64
SAMPLE_CHUNK = 8
DELTA_SUB = 4
POST_MIX_PARTS = 2

IN_PROJ_ROWS = 512
POST_MIX_ROWS = 512
SAMPLE_ROWS = 256
DELTA_SEQS, DELTA_GROUP, DELTA_TOKENS = 4, 2, 256
DELTA_SAMPLE_SEQS = 16
SSM_STEPS = 128
POOL_TOKENS = 1024
POOL_SAMPLE_GROUP = 24
SSM_PARTS = 4


def _cparams(n_axes):
    return pltpu.CompilerParams(dimension_semantics=("arbitrary",) * n_axes,
                                vmem_limit_bytes=VMEM_LIMIT_BYTES)


def _layer(a, l, single=False):
    zeros = (0,) * (a.ndim - 1)
    mode = dict(pipeline_mode=pl.Buffered(1)) if single else {}
    return pl.BlockSpec((None,) + a.shape[1:], lambda *_: (l,) + zeros, **mode)


def _mm(a, b):
    return jnp.dot(a, b, preferred_element_type=F32)


def _bdot(a, b):
    return jnp.dot(a.astype(BF16), b.astype(BF16), preferred_element_type=F32)


def _bdot_nt(a, b):
    return lax.dot_general(a.astype(BF16), b.astype(BF16), (((1,), (1,)), ((), ())),
                           preferred_element_type=F32)


def _bdot_tn(a, b):
    return lax.dot_general(a.astype(BF16), b.astype(BF16), (((0,), (0,)), ((), ())),
                           preferred_element_type=F32)


def _hdot(a, b):
    return jnp.dot(a, b, precision=HIGHEST, preferred_element_type=F32)


def _rms(x, w):
    return x * lax.rsqrt(jnp.mean(x * x, axis=-1, keepdims=True) + EPS) * w


def _silu(x):
    h = 0.5 * x
    return h * jnp.tanh(h) + h


def _in_proj_kernel(x_ref, nw_ref, wqkv_ref, wrest_ref, wab_ref, o_ref):
    h = _rms(x_ref[...], nw_ref[...]).astype(BF16)
    o_ref[:, :QKV_WIDTH] = _mm(h, wqkv_ref[...])
    o_ref[:, COL_GATE:COL_AB] = _mm(h, wrest_ref[...])
    o_ref[:, COL_AB:] = _mm(h, wab_ref[...])


def _in_proj(x, nw, wqkv, wrest, wab, l, tm):
    m = x.shape[0]
    return pl.pallas_call(
        _in_proj_kernel,
        grid=(m // tm,),
        in_specs=[pl.BlockSpec((tm, D_MODEL), lambda i: (i, 0)), _layer(nw, l),
                  _layer(wqkv, l, True), _layer(wrest, l, True), _layer(wab, l, True)],
        out_specs=pl.BlockSpec((tm, PROJ_WIDTH), lambda i: (i, 0)),
        out_shape=jax.ShapeDtypeStruct((m, PROJ_WIDTH), F32),
        compiler_params=_cparams(1),
        name="in_proj",
    )(x, nw, wqkv, wrest, wab)


def _in_proj_conv_kernel(x_ref, nw_ref, wqkv_ref, wrest_ref, wab_ref, cst_ref, cw_ref,
                         y_ref, o_ref, craw_ref, tail, *, tm, nt, nc):
    i = pl.program_id(0)

    @pl.when(i % nt == 0)
    def _():
        tail[...] = cst_ref[0]

    h = _rms(x_ref[...], nw_ref[...]).astype(BF16)
    for cc in range(QKV_WIDTH // nc):
        cs = slice(cc * nc, (cc + 1) * nc)
        r = _mm(h, wqkv_ref[:, cs])
        xfull = jnp.concatenate([tail[:, cs], r], axis=0)
        tail[:, cs] = r[tm - SUBLANES:, :]
        y = _silu(_conv4(xfull, cw_ref[:, cs], tm, SUBLANES))
        if cc * nc < 2 * DN_WIDTH:
            scale = DN_HEAD_DIM ** -0.5 if cc * nc < DN_WIDTH else 1.0
            parts = []
            for j in range(nc // DN_HEAD_DIM):
                yh = y[:, j * DN_HEAD_DIM:(j + 1) * DN_HEAD_DIM]
                parts.append(yh * lax.rsqrt(jnp.sum(yh * yh, axis=-1, keepdims=True) + EPS) * scale)
            y = jnp.concatenate(parts, axis=1)
        y_ref[:, cs] = y
    o_ref[:, :REST_AB] = _mm(h, wrest_ref[...])
    o_ref[:, REST_AB:] = _mm(h, wab_ref[...])
    craw_ref[0] = tail[...]


def _in_proj_conv(x, nw, wqkv, wrest, wab, cst, cw, l, tm, seq_len):
    m = x.shape[0]
    nt = seq_len // tm
    kern = functools.partial(_in_proj_conv_kernel, tm=tm, nt=nt, nc=2 * DN_HEAD_DIM)
    tail_spec = pl.BlockSpec((1, SUBLANES, QKV_WIDTH), lambda i: (i // nt, 0, 0))
    return pl.pallas_call(
        kern,
        grid=(m // tm,),
        in_specs=[pl.BlockSpec((tm, D_MODEL), lambda i: (i, 0)), _layer(nw, l),
                  _layer(wqkv, l, True), _layer(wrest, l, True), _layer(wab, l, True), tail_spec, _layer(cw, l)],
        out_specs=[pl.BlockSpec((tm, QKV_WIDTH), lambda i: (i, 0)),
                   pl.BlockSpec((tm, REST_WIDTH), lambda i: (i, 0)), tail_spec],
        out_shape=[jax.ShapeDtypeStruct((m, QKV_WIDTH), F32), jax.ShapeDtypeStruct((m, REST_WIDTH), F32),
                   jax.ShapeDtypeStruct((m // seq_len, SUBLANES, QKV_WIDTH), F32)],
        scratch_shapes=[pltpu.VMEM((SUBLANES, QKV_WIDTH), F32)],
        compiler_params=_cparams(1),
        name="in_proj_conv",
    )(x, nw, wqkv, wrest, wab, cst, cw)


def _post_mix_kernel(odn_ref, ossm_ref, opool_ref, x_ref, wo_ref, nmp_ref, nfp_ref, wg_ref, wu_ref, wd_ref,
                     nfo_ref, o_ref):
    tm = x_ref.shape[0]
    parts = [slice(p * tm // POST_MIX_PARTS, (p + 1) * tm // POST_MIX_PARTS) for p in range(POST_MIX_PARTS)]
    mix = [(_bdot(odn_ref[rs, :], wo_ref[0:DN_WIDTH, :])
            + _bdot(ossm_ref[rs, :], wo_ref[DN_WIDTH:DN_WIDTH + SSM_WIDTH, :])
            + _bdot(opool_ref[rs, :], wo_ref[DN_WIDTH + SSM_WIDTH:, :])) for rs in parts]
    x1 = [x_ref[rs, :] + _rms(m, nmp_ref[...]) for rs, m in zip(parts, mix)]
    h = [_rms(x, nfp_ref[...]).astype(BF16) for x in x1]
    g = [_mm(y, wg_ref[...]) for y in h]
    u = [_mm(y, wu_ref[...]) for y in h]
    f = [_bdot(_silu(a) * b, wd_ref[...]) for a, b in zip(g, u)]
    for rs, x, y in zip(parts, x1, f):
        o_ref[rs, :] = x + _rms(y, nfo_ref[...])


def _post_mix(odn, ossm, opool, x, wo, nmp, nfp, wg, wu, wd, nfo, l, tm):
    m = x.shape[0]
    row = lambda w: pl.BlockSpec((tm, w), lambda i: (i, 0))
    return pl.pallas_call(
        _post_mix_kernel,
        grid=(m // tm,),
        in_specs=[row(DN_WIDTH), row(SSM_WIDTH), row(POOL_WIDTH), row(D_MODEL),
                  _layer(wo, l, True), _layer(nmp, l), _layer(nfp, l), _layer(wg, l, True), _layer(wu, l, True),
                  _layer(wd, l, True), _layer(nfo, l)],
        out_specs=row(D_MODEL),
        out_shape=jax.ShapeDtypeStruct((m, D_MODEL), F32),
        compiler_params=_cparams(1),
        name="post_mix",
    )(odn, ossm, opool, x, wo, nmp, nfp, wg, wu, wd, nfo)


def _split2(x):
    hi = x.astype(BF16)
    return hi, (x - hi.astype(F32)).astype(BF16)


def _group_cumsum(x, group, reverse=False):
    rows = x.shape[0]
    pos = lax.broadcasted_iota(jnp.int32, x.shape, 0) % group
    d = 1
    while d < group:
        if reverse:
            x = x + jnp.where(pos + d < group, pltpu.roll(x, rows - d, 0), 0.0)
        else:
            x = x + jnp.where(pos >= d, pltpu.roll(x, d, 0), 0.0)
        d *= 2
    return x


def _chunk_masks(rows, blk):
    r = lax.broadcasted_iota(jnp.int32, (rows, rows), 0)
    s = lax.broadcasted_iota(jnp.int32, (rows, rows), 1)
    d = r - s
    if blk < rows:
        d = jnp.where(r // blk == s // blk, d, -1)
    levels = []
    w = 2 * DELTA_SUB
    while w <= blk:
        levels.append(jnp.where(r // (w // 2) != s // (w // 2), r // w, -1) == s // w)
        w *= 2
    return d >= 0, d > 0, (r == s).astype(F32), (r // DELTA_SUB == s // DELTA_SUB, levels)


def _run(gen):
    try:
        while True:
            next(gen)
    except StopIteration as stop:
        return stop.value


def _run_interleaved(gens):
    results = [None] * len(gens)
    live = list(range(len(gens)))
    while live:
        for i in list(live):
            try:
                next(gens[i])
            except StopIteration as stop:
                results[i] = stop.value
                live.remove(i)
    return results


def _unit_lower_inverse(a, eye, inv_masks):
    assert DELTA_SUB == 4
    sub, levels = inv_masks
    d = [jnp.where(sub, x, 0.0) for x in a]
    d2 = [_bdot(x, x) for x in d]
    yield
    t = [_bdot(eye - x, eye + y) for x, y in zip(d, d2)]
    yield
    for mask in levels:
        lt = [_bdot(jnp.where(mask, x, 0.0), y) for x, y in zip(a, t)]
        yield
        t = [y - _bdot(y, z) for y, z in zip(t, lt)]
        yield
    return t


def _delta_local(probs, masks):
    incl, strict, eye, inv_masks = masks
    rows = probs[0][0].shape[0]
    kb = [k * beta for (_, k, _, beta, _, _, _) in probs]
    kq = [_bdot_nt(jnp.concatenate([x, q], axis=0), k) for x, (q, k, _, _, _, _, _) in zip(kb, probs)]
    yield
    decay = [jnp.where(incl, jnp.exp(jnp.where(incl, gam - grow, 0.0)), 0.0) for (_, _, _, _, gam, _, grow) in probs]
    a = [jnp.where(strict, x[:rows] * d, 0.0) for x, d in zip(kq, decay)]
    qk = [x[rows:] * d for x, d in zip(kq, decay)]
    egam = [jnp.exp(p[4]) for p in probs]
    rhs = [jnp.concatenate([p[2] * p[3], x * e], axis=1) for p, x, e in zip(probs, kb, egam)]
    t = yield from _unit_lower_inverse(a, eye, inv_masks)
    sol = [_bdot(x, r) for x, r in zip(t, rhs)]
    yield
    a_sp = [_split2(x) for x in a]
    s_sp = [_split2(x) for x in sol]
    asol = [_mm(ah, sh) + _mm(ah, sl) + _mm(al, sh) for (ah, al), (sh, sl) in zip(a_sp, s_sp)]
    yield
    resid = [r - s - x for r, s, x in zip(rhs, sol, asol)]
    sol = [s + _bdot(x, r) for s, x, r in zip(sol, t, resid)]
    yield
    return [(s[:, :DN_HEAD_DIM], s[:, DN_HEAD_DIM:], x, p[0] * e, p[1] * jnp.exp(p[5]))
            for s, x, p, e in zip(sol, qk, probs, egam)]


def _delta_scan(local, s, glast, nchunk, c):
    chains = range(len(s))
    outs = [[] for _ in chains]
    for j in range(nchunk):
        ks_ = [_bdot(jnp.concatenate([local[n][j][1], local[n][j][3]], axis=0), s[n]) for n in chains]
        yield
        u = [local[n][j][0] - ks_[n][:c] for n in chains]
        for n in chains:
            outs[n].append(ks_[n][c:] + _bdot(local[n][j][2], u[n]))
        yield
        s = [s[n] * jnp.exp(glast[n][j]) + _bdot_tn(local[n][j][4], u[n]) for n in chains]
        yield
    return s, [jnp.concatenate(o, axis=0) for o in outs]


def _delta_front(y, ab, alog, dtb):
    y = _silu(y)
    qs, ks = [], []
    for h in range(DN_HEADS):
        qh = y[:, h * DN_HEAD_DIM:(h + 1) * DN_HEAD_DIM]
        kh = y[:, DN_WIDTH + h * DN_HEAD_DIM:DN_WIDTH + (h + 1) * DN_HEAD_DIM]
        qs.append(qh * lax.rsqrt(jnp.sum(qh * qh, axis=-1, keepdims=True) + EPS) * (DN_HEAD_DIM ** -0.5))
        ks.append(kh * lax.rsqrt(jnp.sum(kh * kh, axis=-1, keepdims=True) + EPS))
    v = y[:, 2 * DN_WIDTH:]
    logg, beta = _delta_gates(ab, alog, dtb)
    return qs, ks, v, logg, beta


def _delta_gates(ab, alog, dtb):
    z = ab + dtb
    softplus = jnp.maximum(z, 0.0) + jnp.log1p(jnp.exp(-jnp.abs(z)))
    return -jnp.exp(alog) * softplus, jax.nn.sigmoid(ab)


def _delta_out(o, gate, onorm):
    return o * lax.rsqrt(jnp.mean(o * o, axis=-1, keepdims=True) + EPS) * onorm * _silu(gate)


def _conv4(xfull, w, rows, off):
    z = xfull * w[0:1]
    z = xfull * w[1:2] + pltpu.roll(z, 1, 0)
    z = xfull * w[2:3] + pltpu.roll(z, 1, 0)
    return (xfull * w[3:4] + pltpu.roll(z, 1, 0))[off:off + rows]


def _delta_prompt_kernel(qkv_ref, ab_ref, gate_ref, s0_ref, alog_ref, dtb_ref, onorm_ref,
                         o_ref, sfin_ref, s_scr, *, nseq, group, tb, c):
    i = pl.program_id(1)

    @pl.when(i == 0)
    def _():
        s_scr[...] = s0_ref[...]

    masks = _chunk_masks(c, c)
    nchunk = tb // c
    heads = range(DN_HEADS)
    hsl = [slice(h * DN_HEAD_DIM, (h + 1) * DN_HEAD_DIM) for h in heads]

    def chunk_problems(chains):
        probs, glast = [], []
        per_seq = {}
        for r, h in chains:
            if r not in per_seq:
                logg, beta = _delta_gates(ab_ref[r], alog_ref[...], dtb_ref[...])
                gam = _group_cumsum(logg, c)
                gexc = _group_cumsum(logg, c, reverse=True) - logg
                per_seq[r] = (beta, gam, gexc, gam.T)
            beta, gam, gexc, gam_t = per_seq[r]
            ksl = slice(DN_WIDTH + h * DN_HEAD_DIM, DN_WIDTH + (h + 1) * DN_HEAD_DIM)
            vsl = slice(2 * DN_WIDTH + h * DN_HEAD_DIM, 2 * DN_WIDTH + (h + 1) * DN_HEAD_DIM)
            for j in range(nchunk):
                rs = slice(j * c, (j + 1) * c)
                probs.append((qkv_ref[r, rs, hsl[h]], qkv_ref[r, rs, ksl], qkv_ref[r, rs, vsl],
                              beta[rs, DN_HEADS + h:DN_HEADS + h + 1], gam[rs, h:h + 1], gexc[rs, h:h + 1],
                              gam_t[h:h + 1, rs]))
            glast.append([gam[(j + 1) * c - 1:(j + 1) * c, h:h + 1] for j in range(nchunk)])
        return probs, glast

    def finish(chains, result):
        s, outs = result
        for n, (r, h) in enumerate(chains):
            s_scr[r, h] = s[n]
            o = _delta_out(outs[n], gate_ref[r, :, hsl[h]], onorm_ref[...])
            o_ref[r, :, hsl[h]] = o.astype(o_ref.dtype)

    groups = [[(r, h) for r in range(g, g + group) for h in heads] for g in range(0, nseq, group)]
    scan, prev = None, None
    for chains in groups:
        probs, glast = chunk_problems(chains)
        local_gen = _delta_local(probs, masks)
        if scan is None:
            flat = _run(local_gen)
        else:
            flat, result = _run_interleaved([local_gen, scan])
            finish(prev, result)
        local = [flat[n * nchunk:(n + 1) * nchunk] for n in range(len(chains))]
        scan = _delta_scan(local, [s_scr[r, h] for r, h in chains], glast, nchunk, c)
        prev = chains
    finish(prev, _run(scan))
    sfin_ref[...] = s_scr[...]


def _delta_prompt(y3, rest3, s0, alog, dtb, onorm, l, nseq, group, tb):
    nb, t, _ = y3.shape
    kern = functools.partial(_delta_prompt_kernel, nseq=nseq, group=group, tb=tb, c=DELTA_CHUNK)
    sspec = pl.BlockSpec((nseq, DN_HEADS, DN_HEAD_DIM, DN_HEAD_DIM), lambda b, i: (b, 0, 0, 0))
    return pl.pallas_call(
        kern,
        grid=(nb // nseq, t // tb),
        in_specs=[pl.BlockSpec((nseq, tb, QKV_WIDTH), lambda b, i: (b, i, 0)),
                  pl.BlockSpec((nseq, tb, LANES), lambda b, i: (b, i, REST_AB // LANES)),
                  pl.BlockSpec((nseq, tb, DN_WIDTH), lambda b, i: (b, i, REST_GATE // DN_WIDTH)),
                  sspec, _layer(alog, l), _layer(dtb, l), _layer(onorm, l)],
        out_specs=[pl.BlockSpec((nseq, tb, DN_WIDTH), lambda b, i: (b, i, 0)), sspec],
        out_shape=[jax.ShapeDtypeStruct((nb, t, DN_WIDTH), BF16),
                   jax.ShapeDtypeStruct((nb, DN_HEADS, DN_HEAD_DIM, DN_HEAD_DIM), F32)],
        scratch_shapes=[pltpu.VMEM((nseq, DN_HEADS, DN_HEAD_DIM, DN_HEAD_DIM), F32)],
        compiler_params=_cparams(2),
        name="delta_prompt",
    )(y3, rest3, rest3, s0, alog, dtb, onorm)


def _delta_sample_kernel(qkv_ref, ab_ref, gate_ref, s0_ref, cw_ref, alog_ref, dtb_ref, onorm_ref, *rest,
                         nseq, c, first):
    o_ref, snew_ref = rest[-2:]
    rows = nseq * c
    x = qkv_ref[...]
    y = _conv4(x, cw_ref[...], rows, 0)
    qs, ks, v, logg, beta = _delta_front(y, ab_ref[...], alog_ref[...], dtb_ref[...])
    valid = (lax.broadcasted_iota(jnp.int32, (rows, LANES), 0) % c) >= first
    logg = jnp.where(valid, logg, 0.0)
    beta = jnp.where(valid, beta, 0.0)
    gam = _group_cumsum(logg, c)
    gexc = _group_cumsum(logg, c, reverse=True) - logg
    gam_t = gam.T
    masks = _chunk_masks(rows, c)
    gate = gate_ref[...]
    heads = range(DN_HEADS)
    hsl = [slice(h * DN_HEAD_DIM, (h + 1) * DN_HEAD_DIM) for h in heads]
    local = _run(_delta_local([(qs[h], ks[h], v[:, hsl[h]], beta[:, DN_HEADS + h:DN_HEADS + h + 1],
                                gam[:, h:h + 1], gexc[:, h:h + 1], gam_t[h:h + 1, :]) for h in heads], masks))
    pairs = [(b, h) for b in range(nseq) for h in heads]
    rsl = [slice(b * c, (b + 1) * c) for b in range(nseq)]
    ks_ = {(b, h): _bdot(jnp.concatenate([local[h][1][rsl[b]], local[h][3][rsl[b]]], axis=0), s0_ref[b, h])
           for b, h in pairs}
    u = {(b, h): local[h][0][rsl[b]] - ks_[b, h][:c] for b, h in pairs}
    for b, h in pairs:
        last = (b + 1) * c - 1
        snew_ref[b, h] = (s0_ref[b, h] * jnp.exp(gam[last:last + 1, h:h + 1])
                          + _bdot_tn(local[h][4][rsl[b]], u[b, h]))
    for h in heads:
        o = (jnp.concatenate([ks_[b, h][c:] for b in range(nseq)], axis=0)
             + _bdot(local[h][2], jnp.concatenate([u[b, h] for b in range(nseq)], axis=0)))
        o_ref[:, hsl[h]] = _delta_out(o, gate[:, hsl[h]], onorm_ref[...]).astype(o_ref.dtype)


def _delta_sample(ext, s0_all, new_all, cw, alog, dtb, onorm, l, nseq, first):
    c = SAMPLE_CHUNK
    nb = s0_all.shape[1]
    rows = nseq * c
    kern = functools.partial(_delta_sample_kernel, nseq=nseq, c=c, first=first)
    sspec = pl.BlockSpec((None, nseq, DN_HEADS, DN_HEAD_DIM, DN_HEAD_DIM), lambda i: (l, i, 0, 0, 0))
    in_specs = [pl.BlockSpec((rows, QKV_WIDTH), lambda i: (i, 0)),
                pl.BlockSpec((rows, LANES), lambda i: (i, COL_AB // LANES)),
                pl.BlockSpec((rows, DN_WIDTH), lambda i: (i, COL_GATE // DN_WIDTH)),
                sspec, _layer(cw, l), _layer(alog, l), _layer(dtb, l), _layer(onorm, l)]
    args = [ext, ext, ext, s0_all, cw, alog, dtb, onorm]
    aliases = {}
    if new_all is not None:
        in_specs.append(pl.BlockSpec(memory_space=pl.ANY))
        args.append(new_all)
        aliases = {len(args) - 1: 1}
    return pl.pallas_call(
        kern,
        grid=(nb // nseq,),
        in_specs=in_specs,
        out_specs=[pl.BlockSpec((rows, DN_WIDTH), lambda i: (i, 0)), sspec],
        out_shape=[jax.ShapeDtypeStruct((nb * c, DN_WIDTH), BF16),
                   jax.ShapeDtypeStruct(s0_all.shape, F32)],
        input_output_aliases=aliases,
        compiler_params=_cparams(1),
        name="delta_sample",
    )(*args)


def _ssm_prep_kernel(are_ref, aim_ref, dt_ref, bre_ref, bim_ref, lre_ref, lim_ref, bmat_ref):
    ar, ai, dt = are_ref[0], aim_ref[0], jnp.exp(dt_ref[0])
    mag = jnp.exp(ar * dt)
    lr = mag * jnp.cos(ai * dt)
    li = mag * jnp.sin(ai * dt)
    lre_ref[0] = lr
    lim_ref[0] = li
    den = ar * ar + ai * ai
    fr = ((lr - 1.0) * ar + li * ai) / den
    fi = (li * ar - (lr - 1.0) * ai) / den
    br, bi = bre_ref[0], bim_ref[0]
    bbr = fr * br - fi * bi
    bbi = fr * bi + fi * br
    lane_group = lax.broadcasted_iota(jnp.int32, (SSM_GROUP, SSM_NS), 1) // SSM_STATE
    for g in range(SSM_GROUPS):
        m = lane_group == g
        bmat_ref[0, g * SSM_GROUP:(g + 1) * SSM_GROUP, 0:SSM_NS] = jnp.where(m, bbr, 0.0)
        bmat_ref[0, g * SSM_GROUP:(g + 1) * SSM_GROUP, SSM_NS:] = jnp.where(m, bbi, 0.0)


def _ssm_prep(are, aim, dt, bre, bim):
    depth = are.shape[0]
    vec = pl.BlockSpec((1, 1, SSM_NS), lambda l: (l, 0, 0))
    mat = pl.BlockSpec((1, SSM_GROUP, SSM_NS), lambda l: (l, 0, 0))
    return pl.pallas_call(
        _ssm_prep_kernel,
        grid=(depth,),
        in_specs=[vec, vec, vec, mat, mat],
        out_specs=[vec, vec, pl.BlockSpec((1, SSM_WIDTH, 2 * SSM_NS), lambda l: (l, 0, 0))],
        out_shape=[jax.ShapeDtypeStruct((depth, 1, SSM_NS), F32), jax.ShapeDtypeStruct((depth, 1, SSM_NS), F32),
                   jax.ShapeDtypeStruct((depth, SSM_WIDTH, 2 * SSM_NS), F32)],
        compiler_params=_cparams(1),
        name="ssm_prep",
    )(are, aim, dt, bre, bim)


def _gelu_tanh(x):
    return 0.5 * x * (1.0 + jnp.tanh(math.sqrt(2.0 / math.pi) * (x + 0.044715 * (x * x * x))))


def _ssm_kernel(ulo_ref, uhi_ref, h0_ref, bmat_ref, lre_ref, lim_ref, cre_ref, cim_ref, d_ref, gw_ref, gb_ref,
                y_ref, hl_ref, buf, h_scr, tm_scr, bm_scr, *, nb, nt, nparts, exact_in):
    i = pl.program_id(0)
    halves = range(SSM_WIDTH // LANES)
    lanes = [slice(s * LANES, (s + 1) * LANES) for s in halves]

    @pl.when(i == 0)
    def _():
        h_scr[...] = h0_ref[...]

    for s, u_ref in enumerate((ulo_ref, uhi_ref)):
        if nb <= nt:
            for b in range(nb):
                tm_scr[s, pl.ds(b, nt, stride=nb), :] = u_ref[b]
        else:
            for t in range(nt):
                tm_scr[s, t * nb:(t + 1) * nb, :] = u_ref[pl.ds(t, nb, stride=nt), :]
    u = jnp.concatenate([tm_scr[s] for s in halves], axis=1)
    lr = jnp.broadcast_to(lre_ref[...], (SUBLANES, SSM_NS))
    li = jnp.broadcast_to(lim_ref[...], (SUBLANES, SSM_NS))
    pt = nt // nparts
    prow = [slice(p * pt * nb, (p + 1) * pt * nb) for p in range(nparts)]
    groups = range(nb // SUBLANES)
    h = [h_scr[g * SUBLANES:(g + 1) * SUBLANES, :] for g in groups]

    bmat = bmat_ref[...] if exact_in else bmat_ref[...].astype(BF16)

    def in_map(p):
        up = u[prow[p]]
        buf[prow[p], :] = _hdot(up, bmat) if exact_in else _bdot(up, bmat)

    def recur(p):
        for t in range(p * pt, (p + 1) * pt):
            for g in groups:
                rs = slice(t * nb + g * SUBLANES, t * nb + (g + 1) * SUBLANES)
                bu = buf[rs, :]
                hre, him = h[g][:, :SSM_NS], h[g][:, SSM_NS:]
                h[g] = jnp.concatenate([lr * hre - li * him + bu[:, :SSM_NS],
                                        lr * him + li * hre + bu[:, SSM_NS:]], axis=1)
                buf[rs, :] = h[g]

    def out_map(p):
        rs = prow[p]
        y = _bdot(buf[rs, :SSM_NS], cre_ref[...]) - _bdot(buf[rs, SSM_NS:], cim_ref[...]) + d_ref[...] * u[rs]
        y = _gelu_tanh(y)
        y = y * jax.nn.sigmoid(_bdot(y, gw_ref[...]) + gb_ref[...])
        for s in halves:
            tm_scr[s, rs, :] = y[:, lanes[s]]

    in_map(0)
    for p in range(nparts):
        if p + 1 < nparts:
            in_map(p + 1)
        recur(p)
        if p >= 1:
            out_map(p - 1)
    out_map(nparts - 1)
    for g in groups:
        h_scr[g * SUBLANES:(g + 1) * SUBLANES, :] = h[g]
    hl_ref[...] = h_scr[...]
    for s in halves:
        if nb <= nt:
            for b in range(nb):
                y_ref[b, :, lanes[s]] = tm_scr[s, pl.ds(b, nt, stride=nb), :].astype(y_ref.dtype)
        else:
            for t in range(nt):
                bm_scr[s, pl.ds(t, nb, stride=nt), :] = tm_scr[s, t * nb:(t + 1) * nb, :]
            y_ref[:, lanes[s]] = bm_scr[s].astype(y_ref.dtype)


def _ssm(proj, col, h0, bmat, lre, lim, cre, cim, dskip, gw, gb, l, nt):
    nb = h0.shape[0]
    rows = nt * nb
    lo = col // LANES
    if proj.ndim == 3:
        t = proj.shape[1]
        grid = (t // nt,)
        u_spec = lambda c: pl.BlockSpec((nb, nt, LANES), lambda i: (0, i, c))
        y_spec = pl.BlockSpec((nb, nt, SSM_WIDTH), lambda i: (0, i, 0))
        y_shape = (nb, t, SSM_WIDTH)
    else:
        grid = (1,)
        u_spec = lambda c: pl.BlockSpec((rows, LANES), lambda i: (0, c))
        y_spec = pl.BlockSpec((rows, SSM_WIDTH), lambda i: (0, 0))
        y_shape = (rows, SSM_WIDTH)
    kern = functools.partial(_ssm_kernel, nb=nb, nt=nt, nparts=SSM_PARTS if proj.ndim == 3 else 1,
                             exact_in=proj.ndim == 2)
    const = lambda a: pl.BlockSpec(a.shape, lambda i: (0,) * a.ndim)
    return pl.pallas_call(
        kern,
        grid=grid,
        in_specs=[u_spec(lo), u_spec(lo + 1), const(h0), _layer(bmat, l), _layer(lre, l), _layer(lim, l),
                  _layer(cre, l), _layer(cim, l), _layer(dskip, l), _layer(gw, l), _layer(gb, l)],
        out_specs=[y_spec, const(h0)],
        out_shape=[jax.ShapeDtypeStruct(y_shape, BF16), jax.ShapeDtypeStruct(h0.shape, F32)],
        scratch_shapes=[pltpu.VMEM((rows, 2 * SSM_NS), F32), pltpu.VMEM(h0.shape, F32),
                        pltpu.VMEM((SSM_WIDTH // LANES, rows, LANES), F32),
                        pltpu.VMEM((SSM_WIDTH // LANES, rows, LANES), F32)],
        compiler_params=_cparams(1),
        name="ssm",
    )(proj, proj, h0, bmat, lre, lim, cre, cim, dskip, gw, gb)


def _pool_windows(xfull):
    s2 = xfull + pltpu.roll(xfull, 1, 0)
    s4 = s2 + pltpu.roll(s2, 2, 0)
    s8 = s4 + pltpu.roll(s4, 4, 0)
    s16 = s8 + pltpu.roll(s8, 8, 0)
    return s2, s4, s8, s16


def _pool_mix(sums, x, pos, w_ref, scale_ref):
    lane = lax.broadcasted_iota(jnp.int32, (1, POOL_WIDTH), 1) // POOL_GROUP
    win = None
    for gidx in reversed(range(len(POOL_WINDOWS))):
        cnt = jnp.minimum(pos + 1, POOL_WINDOWS[gidx]).astype(F32)
        term = sums[gidx] / cnt
        win = term if win is None else jnp.where(lane == gidx, term, win)
    r = win - x
    return _bdot(r, w_ref[...]) * scale_ref[...]


def _pool_prompt_kernel(u_ref, st_ref, w_ref, scale_ref, y_ref, tail, *, tb, pos0):
    i = pl.program_id(1)
    halo = 2 * SUBLANES

    @pl.when(i == 0)
    def _():
        tail[...] = st_ref[0]

    x = u_ref[0]
    xfull = jnp.concatenate([tail[...], x], axis=0)
    tail[...] = x[tb - halo:, :]
    sums = [s[halo:] for s in _pool_windows(xfull)]
    pos = pos0 + i * tb + lax.broadcasted_iota(jnp.int32, (tb, 1), 0)
    y_ref[0] = _pool_mix(sums, x, pos, w_ref, scale_ref).astype(y_ref.dtype)


def _pool_prompt(p3, col, st, wbd, scale, l, tb, pos0):
    nb, t, _ = p3.shape
    kern = functools.partial(_pool_prompt_kernel, tb=tb, pos0=pos0)
    return pl.pallas_call(
        kern,
        grid=(nb, t // tb),
        in_specs=[pl.BlockSpec((1, tb, POOL_WIDTH), lambda b, i: (b, i, col // POOL_WIDTH)),
                  pl.BlockSpec((1, 2 * SUBLANES, POOL_WIDTH), lambda b, i: (b, 0, 0)),
                  _layer(wbd, l), _layer(scale, l)],
        out_specs=pl.BlockSpec((1, tb, POOL_WIDTH), lambda b, i: (b, i, 0)),
        out_shape=jax.ShapeDtypeStruct((nb, t, POOL_WIDTH), BF16),
        scratch_shapes=[pltpu.VMEM((2 * SUBLANES, POOL_WIDTH), F32)],
        compiler_params=_cparams(2),
        name="pool_prompt",
    )(p3, st, wbd, scale)


def _pool_sample_kernel(x_ref, w_ref, scale_ref, y_ref, *, group, first, pos0):
    x = x_ref[...]
    rows = x.shape[0]
    sums = _pool_windows(x)
    pos = pos0 + (lax.broadcasted_iota(jnp.int32, (rows, 1), 0) % group) - first
    y_ref[...] = _pool_mix(sums, x, jnp.maximum(pos, 0), w_ref, scale_ref).astype(y_ref.dtype)


def _pool_sample(ext, wbd, scale, l, group, first, pos0):
    kern = functools.partial(_pool_sample_kernel, group=group, first=first, pos0=pos0)
    const = lambda a: pl.BlockSpec(a.shape, lambda i: (0,) * a.ndim)
    return pl.pallas_call(
        kern,
        grid=(1,),
        in_specs=[const(ext), _layer(wbd, l), _layer(scale, l)],
        out_specs=const(ext),
        out_shape=jax.ShapeDtypeStruct(ext.shape, BF16),
        compiler_params=_cparams(1),
        name="pool_sample",
    )(ext, wbd, scale)


def _block_diag(blocks):
    g, r, c = blocks.shape
    eye = jnp.eye(g, dtype=blocks.dtype)
    return (eye[:, None, :, None] * blocks[:, :, None, :]).reshape(g * r, g * c)


def kernel(x_prompt, x_sample, state_delta, state_conv, state_ssm_re, state_ssm_im, state_pool, norm_mix_pre, norm_mix_post, norm_ffn_pre, norm_ffn_post, w_in, conv_w, dn_a_log, dn_dt_bias, dn_out_norm, ssm_a_re, ssm_a_im, ssm_log_dt, ssm_b_re, ssm_b_im, ssm_c_re, ssm_c_im, ssm_d, ssm_glu_w, ssm_glu_b, pool_w, pool_scale, w_out, ffn_w_gate, ffn_w_up, ffn_w_down):
    depth = w_in.shape[0]
    bp, tp, _ = x_prompt.shape
    bs, ts, _ = x_sample.shape

    w_qkv = w_in[:, :, :QKV_WIDTH].astype(BF16)
    w_rest = w_in[:, :, _OFF_G:].astype(BF16)
    w_ab = jnp.pad(w_in[:, :, _OFF_A:_OFF_G], ((0, 0), (0, 0), (0, LANES - 2 * DN_HEADS))).astype(BF16)
    w_out_b = w_out.astype(BF16)
    wg_b, wu_b, wd_b = ffn_w_gate.astype(BF16), ffn_w_up.astype(BF16), ffn_w_down.astype(BF16)
    row = lambda a: a.reshape(depth, 1, -1)
    nmp, nmo, nfp, nfo = row(norm_mix_pre), row(norm_mix_post), row(norm_ffn_pre), row(norm_ffn_post)
    alog = jnp.pad(dn_a_log, ((0, 0), (0, LANES - DN_HEADS))).reshape(depth, 1, LANES)
    dtb = jnp.pad(dn_dt_bias, ((0, 0), (0, LANES - DN_HEADS))).reshape(depth, 1, LANES)
    onorm = row(dn_out_norm)
    dt_full = jnp.repeat(ssm_log_dt, SSM_STATE, axis=1).reshape(depth, 1, SSM_NS)
    b_t = lambda b: jnp.transpose(b, (0, 3, 1, 2)).reshape(depth, SSM_GROUP, SSM_NS)
    lam_re, lam_im, bmat = _ssm_prep(ssm_a_re.reshape(depth, 1, SSM_NS), ssm_a_im.reshape(depth, 1, SSM_NS),
                                     dt_full, b_t(ssm_b_re), b_t(ssm_b_im))
    c_bd = lambda cc: jax.vmap(_block_diag)(jnp.transpose(cc, (0, 1, 3, 2))).astype(BF16)
    cre, cim = c_bd(ssm_c_re), c_bd(ssm_c_im)
    dskip, glu_b = row(ssm_d), row(ssm_glu_b)
    glu_w = ssm_glu_w.astype(BF16)
    pool_bd = jax.vmap(_block_diag)(pool_w).astype(BF16)
    pscale = row(pool_scale)

    xp = x_prompt.reshape(bp * tp, D_MODEL)
    xs = x_sample.reshape(bs * ts, D_MODEL)
    zero_conv = jnp.zeros((bp, SUBLANES, QKV_WIDTH), F32)
    zero_delta = jnp.zeros((bp, DN_HEADS, DN_HEAD_DIM, DN_HEAD_DIM), F32)
    zero_h = jnp.zeros((bp, 2 * SSM_NS), F32)
    zero_pool = jnp.zeros((bp, 2 * SUBLANES, POOL_WIDTH), F32)
    pad_rows = SAMPLE_CHUNK - ts - (DN_CONV - 1)
    pool_group = POOL_SAMPLE_GROUP
    pool_first = 1 + POOL_BUF
    h0_s = jnp.concatenate([state_ssm_re.reshape(depth, bs, SSM_NS), state_ssm_im.reshape(depth, bs, SSM_NS)],
                           axis=2)

    outs_p, outs_s = [], []
    delta_s = jnp.zeros(state_delta.shape, F32)
    for l in range(depth):
        y, rest, craw = _in_proj_conv(xp, nmp, w_qkv, w_rest, w_ab, zero_conv, conv_w, l, IN_PROJ_ROWS, tp)
        y3 = y.reshape(bp, tp, QKV_WIDTH)
        r3 = rest.reshape(bp, tp, REST_WIDTH)
        o_dn, delta_new = _delta_prompt(y3, r3, zero_delta, alog, dtb, onorm, l, DELTA_SEQS, DELTA_GROUP,
                                        DELTA_TOKENS)
        o_ssm, h_fin = _ssm(r3, REST_SSM, zero_h, bmat, lam_re, lam_im, cre, cim, dskip, glu_w, glu_b, l,
                            SSM_STEPS)
        o_pool = _pool_prompt(r3, REST_POOL, zero_pool, pool_bd, pscale, l, POOL_TOKENS, 0)
        xp = _post_mix(o_dn.reshape(bp * tp, DN_WIDTH), o_ssm.reshape(bp * tp, SSM_WIDTH),
                       o_pool.reshape(bp * tp, POOL_WIDTH), xp,
                       w_out_b, nmo, nfp, wg_b, wu_b, wd_b, nfo, l, POST_MIX_ROWS)
        outs_p.append((delta_new, craw[:, SUBLANES - (DN_CONV - 1):, :],
                       h_fin[:, :SSM_NS].reshape(bp, SSM_GROUPS, SSM_STATE),
                       h_fin[:, SSM_NS:].reshape(bp, SSM_GROUPS, SSM_STATE),
                       r3[:, tp - POOL_BUF:, REST_POOL:REST_AB]))

        proj = _in_proj(xs, nmp, w_qkv, w_rest, w_ab, l, SAMPLE_ROWS)
        s3 = proj.reshape(bs, ts, PROJ_WIDTH)
        head = jnp.concatenate([jnp.zeros((bs, pad_rows, QKV_WIDTH), F32), state_conv[l]], axis=1)
        head = jnp.pad(head, ((0, 0), (0, 0), (0, PROJ_WIDTH - QKV_WIDTH)))
        ext = jnp.concatenate([head, s3], axis=1).reshape(bs * SAMPLE_CHUNK, PROJ_WIDTH)
        o_ext, delta_s = _delta_sample(ext, state_delta, delta_s, conv_w, alog, dtb, onorm, l,
                                       DELTA_SAMPLE_SEQS,
                                       SAMPLE_CHUNK - ts)
        o_dn = o_ext.reshape(bs, SAMPLE_CHUNK, DN_WIDTH)[:, SAMPLE_CHUNK - ts:].reshape(bs * ts, DN_WIDTH)
        o_ssm, h_fin = _ssm(proj, COL_SSM, h0_s[l], bmat, lam_re, lam_im, cre, cim, dskip, glu_w, glu_b, l, ts)
        pool_u = s3[:, :, COL_POOL:COL_AB]
        pext = jnp.concatenate([jnp.zeros((bs, 1, POOL_WIDTH), F32), state_pool[l], pool_u,
                                jnp.zeros((bs, pool_group - pool_first - ts, POOL_WIDTH), F32)], axis=1)
        y_ext = _pool_sample(pext.reshape(bs * pool_group, POOL_WIDTH), pool_bd, pscale, l, pool_group,
                             pool_first, PAST_LEN)
        o_pool = y_ext.reshape(bs, pool_group, POOL_WIDTH)[:, pool_first:pool_first + ts].reshape(bs * ts, POOL_WIDTH)
        xs = _post_mix(o_dn, o_ssm, o_pool, xs, w_out_b, nmo, nfp, wg_b, wu_b, wd_b, nfo, l, SAMPLE_ROWS)
        outs_s.append((s3[:, ts - (DN_CONV - 1):, :QKV_WIDTH],
                       h_fin[:, :SSM_NS].reshape(bs, SSM_GROUPS, SSM_STATE),
                       h_fin[:, SSM_NS:].reshape(bs, SSM_GROUPS, SSM_STATE),
                       jnp.concatenate([state_pool[l][:, ts:], pool_u], axis=1)))

    stack = lambda outs, k: jnp.stack([o[k] for o in outs])
    return (xp.reshape(bp, tp, D_MODEL), xs.reshape(bs, ts, D_MODEL),
            stack(outs_p, 0), stack(outs_p, 1), stack(outs_p, 2), stack(outs_p, 3), stack(outs_p, 4),
            delta_s, stack(outs_s, 0), stack(outs_s, 1), stack(outs_s, 2), stack(outs_s, 3))
```

```python
import functools
import math

import jax
import jax.numpy as jnp
from jax import lax
from jax.experimental import pallas as pl
from jax.experimental.pallas import tpu as pltpu

F32 = jnp.float32
BF16 = jnp.bfloat16
HIGHEST = lax.Precision.HIGHEST

D_MODEL = 1024
DN_HEADS = 4
DN_HEAD_DIM = 128
DN_WIDTH = DN_HEADS * DN_HEAD_DIM
DN_CONV = 4
QKV_WIDTH = 3 * DN_WIDTH
SSM_WIDTH = 256
SSM_GROUP = 16
SSM_GROUPS = 16
SSM_STATE = 64
SSM_NS = SSM_GROUPS * SSM_STATE
POOL_WIDTH = 256
POOL_WINDOWS = (2, 4, 8, 16)
POOL_GROUP = 64
POOL_BUF = 15
D_FF = 2816
EPS = 1e-6
PAST_LEN = 16384

LANES = 128
SUBLANES = 8

COL_GATE = QKV_WIDTH
COL_SSM = COL_GATE + DN_WIDTH
COL_POOL = COL_SSM + SSM_WIDTH
COL_AB = COL_POOL + POOL_WIDTH
PROJ_WIDTH = COL_AB + LANES
REST_GATE = 0
REST_SSM = COL_SSM - COL_GATE
REST_POOL = COL_POOL - COL_GATE
REST_AB = COL_AB - COL_GATE
REST_WIDTH = PROJ_WIDTH - COL_GATE

_OFF_A = QKV_WIDTH
_OFF_G = _OFF_A + 2 * DN_HEADS

VMEM_LIMIT_BYTES = 56 * 1024 * 1024

DELTA_CHUNK = 64
SAMPLE_CHUNK = 8
DELTA_SUB = 4
POST_MIX_PARTS = 2

IN_PROJ_ROWS = 512
POST_MIX_ROWS = 512
SAMPLE_ROWS = 256
DELTA_SEQS, DELTA_GROUP, DELTA_TOKENS = 4, 2, 256
DELTA_SAMPLE_SEQS = 16
SSM_STEPS = 128
POOL_TOKENS = 1024
POOL_SAMPLE_GROUP = 24
SSM_PARTS = 4


def _cparams(n_axes, fuse_inputs=None):
    return pltpu.CompilerParams(dimension_semantics=("arbitrary",) * n_axes,
                                vmem_limit_bytes=VMEM_LIMIT_BYTES, allow_input_fusion=fuse_inputs)


def _layer(a, l, single=False):
    zeros = (0,) * (a.ndim - 1)
    mode = dict(pipeline_mode=pl.Buffered(1)) if single else {}
    return pl.BlockSpec((None,) + a.shape[1:], lambda *_: (l,) + zeros, **mode)


def _mm(a, b):
    return jnp.dot(a, b, preferred_element_type=F32)


def _bdot(a, b):
    return jnp.dot(a.astype(BF16), b.astype(BF16), preferred_element_type=F32)


def _bdot_nt(a, b):
    return lax.dot_general(a.astype(BF16), b.astype(BF16), (((1,), (1,)), ((), ())),
                           preferred_element_type=F32)


def _bdot_tn(a, b):
    return lax.dot_general(a.astype(BF16), b.astype(BF16), (((0,), (0,)), ((), ())),
                           preferred_element_type=F32)


def _hdot(a, b):
    return jnp.dot(a, b, precision=HIGHEST, preferred_element_type=F32)


def _rms(x, w):
    return x * lax.rsqrt(jnp.mean(x * x, axis=-1, keepdims=True) + EPS) * w


def _silu(x):
    h = 0.5 * x
    return h * jnp.tanh(h) + h


def _in_proj_kernel(x_ref, nw_ref, wqkv_ref, wrest_ref, wab_ref, o_ref):
    h = _rms(x_ref[...], nw_ref[...]).astype(BF16)
    o_ref[:, :QKV_WIDTH] = _mm(h, wqkv_ref[...])
    o_ref[:, COL_GATE:COL_AB] = _mm(h, wrest_ref[...])
    o_ref[:, COL_AB:] = _mm(h, wab_ref[...])


def _in_proj(x, nw, wqkv, wrest, wab, l, tm):
    m = x.shape[0]
    return pl.pallas_call(
        _in_proj_kernel,
        grid=(m // tm,),
        in_specs=[pl.BlockSpec((tm, D_MODEL), lambda i: (i, 0)), _layer(nw, l),
                  _layer(wqkv, l, True), _layer(wrest, l, True), _layer(wab, l, True)],
        out_specs=pl.BlockSpec((tm, PROJ_WIDTH), lambda i: (i, 0)),
        out_shape=jax.ShapeDtypeStruct((m, PROJ_WIDTH), F32),
        compiler_params=_cparams(1),
        name="in_proj",
    )(x, nw, wqkv, wrest, wab)


def _in_proj_conv_kernel(x_ref, nw_ref, wqkv_ref, wrest_ref, wab_ref, cst_ref, cw_ref,
                         y_ref, o_ref, craw_ref, tail, *, tm, nt, nc):
    i = pl.program_id(0)

    @pl.when(i % nt == 0)
    def _():
        tail[...] = cst_ref[0]

    h = _rms(x_ref[...], nw_ref[...]).astype(BF16)
    for cc in range(QKV_WIDTH // nc):
        cs = slice(cc * nc, (cc + 1) * nc)
        r = _mm(h, wqkv_ref[:, cs])
        xfull = jnp.concatenate([tail[:, cs], r], axis=0)
        tail[:, cs] = r[tm - SUBLANES:, :]
        y = _silu(_conv4(xfull, cw_ref[:, cs], tm, SUBLANES))
        if cc * nc < 2 * DN_WIDTH:
            scale = DN_HEAD_DIM ** -0.5 if cc * nc < DN_WIDTH else 1.0
            parts = []
            for j in range(nc // DN_HEAD_DIM):
                yh = y[:, j * DN_HEAD_DIM:(j + 1) * DN_HEAD_DIM]
                parts.append(yh * lax.rsqrt(jnp.sum(yh * yh, axis=-1, keepdims=True) + EPS) * scale)
            y = jnp.concatenate(parts, axis=1)
        y_ref[:, cs] = y
    o_ref[:, :REST_AB] = _mm(h, wrest_ref[...])
    o_ref[:, REST_AB:] = _mm(h, wab_ref[...])
    craw_ref[0] = tail[...]


def _in_proj_conv(x, nw, wqkv, wrest, wab, cst, cw, l, tm, seq_len):
    m = x.shape[0]
    nt = seq_len // tm
    kern = functools.partial(_in_proj_conv_kernel, tm=tm, nt=nt, nc=2 * DN_HEAD_DIM)
    tail_spec = pl.BlockSpec((1, SUBLANES, QKV_WIDTH), lambda i: (i // nt, 0, 0))
    return pl.pallas_call(
        kern,
        grid=(m // tm,),
        in_specs=[pl.BlockSpec((tm, D_MODEL), lambda i: (i, 0)), _layer(nw, l),
                  _layer(wqkv, l, True), _layer(wrest, l, True), _layer(wab, l, True), tail_spec, _layer(cw, l)],
        out_specs=[pl.BlockSpec((tm, QKV_WIDTH), lambda i: (i, 0)),
                   pl.BlockSpec((tm, REST_WIDTH), lambda i: (i, 0)), tail_spec],
        out_shape=[jax.ShapeDtypeStruct((m, QKV_WIDTH), F32), jax.ShapeDtypeStruct((m, REST_WIDTH), F32),
                   jax.ShapeDtypeStruct((m // seq_len, SUBLANES, QKV_WIDTH), F32)],
        scratch_shapes=[pltpu.VMEM((SUBLANES, QKV_WIDTH), F32)],
        compiler_params=_cparams(1),
        name="in_proj_conv",
    )(x, nw, wqkv, wrest, wab, cst, cw)


def _post_mix_kernel(odn_ref, ossm_ref, opool_ref, x_ref, wo_ref, nmp_ref, nfp_ref, wg_ref, wu_ref, wd_ref,
                     nfo_ref, o_ref):
    tm = x_ref.shape[0]
    parts = [slice(p * tm // POST_MIX_PARTS, (p + 1) * tm // POST_MIX_PARTS) for p in range(POST_MIX_PARTS)]
    mix = [(_bdot(odn_ref[rs, :], wo_ref[0:DN_WIDTH, :])
            + _bdot(ossm_ref[rs, :], wo_ref[DN_WIDTH:DN_WIDTH + SSM_WIDTH, :])
            + _bdot(opool_ref[rs, :], wo_ref[DN_WIDTH + SSM_WIDTH:, :])) for rs in parts]
    x1 = [x_ref[rs, :] + _rms(m, nmp_ref[...]) for rs, m in zip(parts, mix)]
    h = [_rms(x, nfp_ref[...]).astype(BF16) for x in x1]
    g = [_mm(y, wg_ref[...]) for y in h]
    u = [_mm(y, wu_ref[...]) for y in h]
    f = [_bdot(_silu(a) * b, wd_ref[...]) for a, b in zip(g, u)]
    for rs, x, y in zip(parts, x1, f):
        o_ref[rs, :] = x + _rms(y, nfo_ref[...])


def _post_mix(odn, ossm, opool, x, wo, nmp, nfp, wg, wu, wd, nfo, l, tm):
    m = x.shape[0]
    row = lambda w: pl.BlockSpec((tm, w), lambda i: (i, 0))
    return pl.pallas_call(
        _post_mix_kernel,
        grid=(m // tm,),
        in_specs=[row(DN_WIDTH), row(SSM_WIDTH), row(POOL_WIDTH), row(D_MODEL),
                  _layer(wo, l, True), _layer(nmp, l), _layer(nfp, l), _layer(wg, l, True), _layer(wu, l, True),
                  _layer(wd, l, True), _layer(nfo, l)],
        out_specs=row(D_MODEL),
        out_shape=jax.ShapeDtypeStruct((m, D_MODEL), F32),
        compiler_params=_cparams(1, [False] * 4 + [True, False, False, True, True, True, False]),
        name="post_mix",
    )(odn, ossm, opool, x, wo, nmp, nfp, wg, wu, wd, nfo)


def _split2(x):
    hi = x.astype(BF16)
    return hi, (x - hi.astype(F32)).astype(BF16)


def _group_cumsum(x, group, reverse=False):
    rows = x.shape[0]
    pos = lax.broadcasted_iota(jnp.int32, x.shape, 0) % group
    d = 1
    while d < group:
        if reverse:
            x = x + jnp.where(pos + d < group, pltpu.roll(x, rows - d, 0), 0.0)
        else:
            x = x + jnp.where(pos >= d, pltpu.roll(x, d, 0), 0.0)
        d *= 2
    return x


def _chunk_masks(rows, blk):
    r = lax.broadcasted_iota(jnp.int32, (rows, rows), 0)
    s = lax.broadcasted_iota(jnp.int32, (rows, rows), 1)
    d = r - s
    if blk < rows:
        d = jnp.where(r // blk == s // blk, d, -1)
    levels = []
    w = 2 * DELTA_SUB
    while w <= blk:
        levels.append(jnp.where(r // (w // 2) != s // (w // 2), r // w, -1) == s // w)
        w *= 2
    return d >= 0, d > 0, (r == s).astype(F32), (r // DELTA_SUB == s // DELTA_SUB, levels)


def _run(gen):
    try:
        while True:
            next(gen)
    except StopIteration as stop:
        return stop.value


def _run_interleaved(gens):
    results = [None] * len(gens)
    live = list(range(len(gens)))
    while live:
        for i in list(live):
            try:
                next(gens[i])
            except StopIteration as stop:
                results[i] = stop.value
                live.remove(i)
    return results


def _unit_lower_inverse(a, eye, inv_masks):
    assert DELTA_SUB == 4
    sub, levels = inv_masks
    d = [jnp.where(sub, x, 0.0) for x in a]
    d2 = [_bdot(x, x) for x in d]
    yield
    t = [_bdot(eye - x, eye + y) for x, y in zip(d, d2)]
    yield
    for mask in levels:
        lt = [_bdot(jnp.where(mask, x, 0.0), y) for x, y in zip(a, t)]
        yield
        t = [y - _bdot(y, z) for y, z in zip(t, lt)]
        yield
    return t


def _delta_local(probs, masks):
    incl, strict, eye, inv_masks = masks
    rows = probs[0][0].shape[0]
    kb = [k * beta for (_, k, _, beta, _, _, _) in probs]
    kq = [_bdot_nt(jnp.concatenate([x, q], axis=0), k) for x, (q, k, _, _, _, _, _) in zip(kb, probs)]
    yield
    decay = [jnp.where(incl, jnp.exp(jnp.where(incl, gam - grow, 0.0)), 0.0) for (_, _, _, _, gam, _, grow) in probs]
    a = [jnp.where(strict, x[:rows] * d, 0.0) for x, d in zip(kq, decay)]
    qk = [x[rows:] * d for x, d in zip(kq, decay)]
    egam = [jnp.exp(p[4]) for p in probs]
    rhs = [jnp.concatenate([p[2] * p[3], x * e], axis=1) for p, x, e in zip(probs, kb, egam)]
    t = yield from _unit_lower_inverse(a, eye, inv_masks)
    sol = [_bdot(x, r) for x, r in zip(t, rhs)]
    yield
    a_sp = [_split2(x) for x in a]
    s_sp = [_split2(x) for x in sol]
    asol = [_mm(ah, sh) + _mm(ah, sl) + _mm(al, sh) for (ah, al), (sh, sl) in zip(a_sp, s_sp)]
    yield
    resid = [r - s - x for r, s, x in zip(rhs, sol, asol)]
    sol = [s + _bdot(x, r) for s, x, r in zip(sol, t, resid)]
    yield
    return [(s[:, :DN_HEAD_DIM], s[:, DN_HEAD_DIM:], x, p[0] * e, p[1] * jnp.exp(p[5]))
            for s, x, p, e in zip(sol, qk, probs, egam)]


def _delta_scan(local, s, glast, nchunk, c):
    chains = range(len(s))
    outs = [[] for _ in chains]
    for j in range(nchunk):
        ks_ = [_bdot(jnp.concatenate([local[n][j][1], local[n][j][3]], axis=0), s[n]) for n in chains]
        yield
        u = [local[n][j][0] - ks_[n][:c] for n in chains]
        for n in chains:
            outs[n].append(ks_[n][c:] + _bdot(local[n][j][2], u[n]))
        yield
        s = [s[n] * jnp.exp(glast[n][j]) + _bdot_tn(local[n][j][4], u[n]) for n in chains]
        yield
    return s, [jnp.concatenate(o, axis=0) for o in outs]


def _delta_front(y, ab, alog, dtb):
    y = _silu(y)
    qs, ks = [], []
    for h in range(DN_HEADS):
        qh = y[:, h * DN_HEAD_DIM:(h + 1) * DN_HEAD_DIM]
        kh = y[:, DN_WIDTH + h * DN_HEAD_DIM:DN_WIDTH + (h + 1) * DN_HEAD_DIM]
        qs.append(qh * lax.rsqrt(jnp.sum(qh * qh, axis=-1, keepdims=True) + EPS) * (DN_HEAD_DIM ** -0.5))
        ks.append(kh * lax.rsqrt(jnp.sum(kh * kh, axis=-1, keepdims=True) + EPS))
    v = y[:, 2 * DN_WIDTH:]
    logg, beta = _delta_gates(ab, alog, dtb)
    return qs, ks, v, logg, beta


def _delta_gates(ab, alog, dtb):
    z = ab + dtb
    softplus = jnp.maximum(z, 0.0) + jnp.log1p(jnp.exp(-jnp.abs(z)))
    return -jnp.exp(alog) * softplus, jax.nn.sigmoid(ab)


def _delta_out(o, gate, onorm):
    return o * lax.rsqrt(jnp.mean(o * o, axis=-1, keepdims=True) + EPS) * onorm * _silu(gate)


def _conv4(xfull, w, rows, off):
    z = xfull * w[0:1]
    z = xfull * w[1:2] + pltpu.roll(z, 1, 0)
    z = xfull * w[2:3] + pltpu.roll(z, 1, 0)
    return (xfull * w[3:4] + pltpu.roll(z, 1, 0))[off:off + rows]


def _delta_prompt_kernel(qkv_ref, ab_ref, gate_ref, s0_ref, alog_ref, dtb_ref, onorm_ref,
                         o_ref, sfin_ref, s_scr, *, nseq, group, tb, c):
    i = pl.program_id(1)

    @pl.when(i == 0)
    def _():
        s_scr[...] = s0_ref[...]

    masks = _chunk_masks(c, c)
    nchunk = tb // c
    heads = range(DN_HEADS)
    hsl = [slice(h * DN_HEAD_DIM, (h + 1) * DN_HEAD_DIM) for h in heads]

    def chunk_problems(chains):
        probs, glast = [], []
        per_seq = {}
        for r, h in chains:
            if r not in per_seq:
                logg, beta = _delta_gates(ab_ref[r], alog_ref[...], dtb_ref[...])
                gam = _group_cumsum(logg, c)
                gexc = _group_cumsum(logg, c, reverse=True) - logg
                per_seq[r] = (beta, gam, gexc, gam.T)
            beta, gam, gexc, gam_t = per_seq[r]
            ksl = slice(DN_WIDTH + h * DN_HEAD_DIM, DN_WIDTH + (h + 1) * DN_HEAD_DIM)
            vsl = slice(2 * DN_WIDTH + h * DN_HEAD_DIM, 2 * DN_WIDTH + (h + 1) * DN_HEAD_DIM)
            for j in range(nchunk):
                rs = slice(j * c, (j + 1) * c)
                probs.append((qkv_ref[r, rs, hsl[h]], qkv_ref[r, rs, ksl], qkv_ref[r, rs, vsl],
                              beta[rs, DN_HEADS + h:DN_HEADS + h + 1], gam[rs, h:h + 1], gexc[rs, h:h + 1],
                              gam_t[h:h + 1, rs]))
            glast.append([gam[(j + 1) * c - 1:(j + 1) * c, h:h + 1] for j in range(nchunk)])
        return probs, glast

    def finish(chains, result):
        s, outs = result
        for n, (r, h) in enumerate(chains):
            s_scr[r, h] = s[n]
            o = _delta_out(outs[n], gate_ref[r, :, hsl[h]], onorm_ref[...])
            o_ref[r, :, hsl[h]] = o.astype(o_ref.dtype)

    groups = [[(r, h) for r in range(g, g + group) for h in heads] for g in range(0, nseq, group)]
    scan, prev = None, None
    for chains in groups:
        probs, glast = chunk_problems(chains)
        local_gen = _delta_local(probs, masks)
        if scan is None:
            flat = _run(local_gen)
        else:
            flat, result = _run_interleaved([local_gen, scan])
            finish(prev, result)
        local = [flat[n * nchunk:(n + 1) * nchunk] for n in range(len(chains))]
        scan = _delta_scan(local, [s_scr[r, h] for r, h in chains], glast, nchunk, c)
        prev = chains
    finish(prev, _run(scan))
    sfin_ref[...] = s_scr[...]


def _delta_prompt(y3, rest3, s0, alog, dtb, onorm, l, nseq, group, tb):
    nb, t, _ = y3.shape
    kern = functools.partial(_delta_prompt_kernel, nseq=nseq, group=group, tb=tb, c=DELTA_CHUNK)
    sspec = pl.BlockSpec((nseq, DN_HEADS, DN_HEAD_DIM, DN_HEAD_DIM), lambda b, i: (b, 0, 0, 0))
    return pl.pallas_call(
        kern,
        grid=(nb // nseq, t // tb),
        in_specs=[pl.BlockSpec((nseq, tb, QKV_WIDTH), lambda b, i: (b, i, 0)),
                  pl.BlockSpec((nseq, tb, LANES), lambda b, i: (b, i, REST_AB // LANES)),
                  pl.BlockSpec((nseq, tb, DN_WIDTH), lambda b, i: (b, i, REST_GATE // DN_WIDTH)),
                  sspec, _layer(alog, l), _layer(dtb, l), _layer(onorm, l)],
        out_specs=[pl.BlockSpec((nseq, tb, DN_WIDTH), lambda b, i: (b, i, 0)), sspec],
        out_shape=[jax.ShapeDtypeStruct((nb, t, DN_WIDTH), BF16),
                   jax.ShapeDtypeStruct((nb, DN_HEADS, DN_HEAD_DIM, DN_HEAD_DIM), F32)],
        scratch_shapes=[pltpu.VMEM((nseq, DN_HEADS, DN_HEAD_DIM, DN_HEAD_DIM), F32)],
        compiler_params=_cparams(2),
        name="delta_prompt",
    )(y3, rest3, rest3, s0, alog, dtb, onorm)


def _delta_sample_kernel(qkv_ref, ab_ref, gate_ref, s0_ref, cw_ref, alog_ref, dtb_ref, onorm_ref, *rest,
                         nseq, c, first):
    o_ref, snew_ref = rest[-2:]
    rows = nseq * c
    x = qkv_ref[...]
    y = _conv4(x, cw_ref[...], rows, 0)
    qs, ks, v, logg, beta = _delta_front(y, ab_ref[...], alog_ref[...], dtb_ref[...])
    valid = (lax.broadcasted_iota(jnp.int32, (rows, LANES), 0) % c) >= first
    logg = jnp.where(valid, logg, 0.0)
    beta = jnp.where(valid, beta, 0.0)
    gam = _group_cumsum(logg, c)
    gexc = _group_cumsum(logg, c, reverse=True) - logg
    gam_t = gam.T
    masks = _chunk_masks(rows, c)
    gate = gate_ref[...]
    heads = range(DN_HEADS)
    hsl = [slice(h * DN_HEAD_DIM, (h + 1) * DN_HEAD_DIM) for h in heads]
    local = _run(_delta_local([(qs[h], ks[h], v[:, hsl[h]], beta[:, DN_HEADS + h:DN_HEADS + h + 1],
                                gam[:, h:h + 1], gexc[:, h:h + 1], gam_t[h:h + 1, :]) for h in heads], masks))
    pairs = [(b, h) for b in range(nseq) for h in heads]
    rsl = [slice(b * c, (b + 1) * c) for b in range(nseq)]
    ks_ = {(b, h): _bdot(jnp.concatenate([local[h][1][rsl[b]], local[h][3][rsl[b]]], axis=0), s0_ref[b, h])
           for b, h in pairs}
    u = {(b, h): local[h][0][rsl[b]] - ks_[b, h][:c] for b, h in pairs}
    for b, h in pairs:
        last = (b + 1) * c - 1
        snew_ref[b, h] = (s0_ref[b, h] * jnp.exp(gam[last:last + 1, h:h + 1])
                          + _bdot_tn(local[h][4][rsl[b]], u[b, h]))
    for h in heads:
        o = (jnp.concatenate([ks_[b, h][c:] for b in range(nseq)], axis=0)
             + _bdot(local[h][2], jnp.concatenate([u[b, h] for b in range(nseq)], axis=0)))
        o_ref[:, hsl[h]] = _delta_out(o, gate[:, hsl[h]], onorm_ref[...]).astype(o_ref.dtype)


def _delta_sample(ext, s0_all, new_all, cw, alog, dtb, onorm, l, nseq, first):
    c = SAMPLE_CHUNK
    nb = s0_all.shape[1]
    rows = nseq * c
    kern = functools.partial(_delta_sample_kernel, nseq=nseq, c=c, first=first)
    sspec = pl.BlockSpec((None, nseq, DN_HEADS, DN_HEAD_DIM, DN_HEAD_DIM), lambda i: (l, i, 0, 0, 0))
    in_specs = [pl.BlockSpec((rows, QKV_WIDTH), lambda i: (i, 0)),
                pl.BlockSpec((rows, LANES), lambda i: (i, COL_AB // LANES)),
                pl.BlockSpec((rows, DN_WIDTH), lambda i: (i, COL_GATE // DN_WIDTH)),
                sspec, _layer(cw, l), _layer(alog, l), _layer(dtb, l), _layer(onorm, l)]
    args = [ext, ext, ext, s0_all, cw, alog, dtb, onorm]
    aliases = {}
    if new_all is not None:
        in_specs.append(pl.BlockSpec(memory_space=pl.ANY))
        args.append(new_all)
        aliases = {len(args) - 1: 1}
    return pl.pallas_call(
        kern,
        grid=(nb // nseq,),
        in_specs=in_specs,
        out_specs=[pl.BlockSpec((rows, DN_WIDTH), lambda i: (i, 0)), sspec],
        out_shape=[jax.ShapeDtypeStruct((nb * c, DN_WIDTH), BF16),
                   jax.ShapeDtypeStruct(s0_all.shape, F32)],
        input_output_aliases=aliases,
        compiler_params=_cparams(1),
        name="delta_sample",
    )(*args)


def _ssm_prep_kernel(are_ref, aim_ref, dt_ref, bre_ref, bim_ref, lre_ref, lim_ref, bmat_ref):
    ar, ai, dt = are_ref[0], aim_ref[0], jnp.exp(dt_ref[0])
    mag = jnp.exp(ar * dt)
    lr = mag * jnp.cos(ai * dt)
    li = mag * jnp.sin(ai * dt)
    lre_ref[0] = lr
    lim_ref[0] = li
    den = ar * ar + ai * ai
    fr = ((lr - 1.0) * ar + li * ai) / den
    fi = (li * ar - (lr - 1.0) * ai) / den
    br, bi = bre_ref[0], bim_ref[0]
    bbr = fr * br - fi * bi
    bbi = fr * bi + fi * br
    lane_group = lax.broadcasted_iota(jnp.int32, (SSM_GROUP, SSM_NS), 1) // SSM_STATE
    for g in range(SSM_GROUPS):
        m = lane_group == g
        bmat_ref[0, g * SSM_GROUP:(g + 1) * SSM_GROUP, 0:SSM_NS] = jnp.where(m, bbr, 0.0)
        bmat_ref[0, g * SSM_GROUP:(g + 1) * SSM_GROUP, SSM_NS:] = jnp.where(m, bbi, 0.0)


def _ssm_prep(are, aim, dt, bre, bim):
    depth = are.shape[0]
    vec = pl.BlockSpec((1, 1, SSM_NS), lambda l: (l, 0, 0))
    mat = pl.BlockSpec((1, SSM_GROUP, SSM_NS), lambda l: (l, 0, 0))
    return pl.pallas_call(
        _ssm_prep_kernel,
        grid=(depth,),
        in_specs=[vec, vec, vec, mat, mat],
        out_specs=[vec, vec, pl.BlockSpec((1, SSM_WIDTH, 2 * SSM_NS), lambda l: (l, 0, 0))],
        out_shape=[jax.ShapeDtypeStruct((depth, 1, SSM_NS), F32), jax.ShapeDtypeStruct((depth, 1, SSM_NS), F32),
                   jax.ShapeDtypeStruct((depth, SSM_WIDTH, 2 * SSM_NS), F32)],
        compiler_params=_cparams(1),
        name="ssm_prep",
    )(are, aim, dt, bre, bim)


def _gelu_tanh(x):
    return 0.5 * x * (1.0 + jnp.tanh(math.sqrt(2.0 / math.pi) * (x + 0.044715 * (x * x * x))))


def _ssm_kernel(ulo_ref, uhi_ref, h0_ref, bmat_ref, lre_ref, lim_ref, cre_ref, cim_ref, d_ref, gw_ref, gb_ref,
                y_ref, hl_ref, buf, h_scr, tm_scr, bm_scr, *, nb, nt, nparts, exact_in):
    i = pl.program_id(0)
    halves = range(SSM_WIDTH // LANES)
    lanes = [slice(s * LANES, (s + 1) * LANES) for s in halves]

    @pl.when(i == 0)
    def _():
        h_scr[...] = h0_ref[...]

    for s, u_ref in enumerate((ulo_ref, uhi_ref)):
        if nb <= nt:
            for b in range(nb):
                tm_scr[s, pl.ds(b, nt, stride=nb), :] = u_ref[b]
        else:
            for t in range(nt):
                tm_scr[s, t * nb:(t + 1) * nb, :] = u_ref[pl.ds(t, nb, stride=nt), :]
    u = jnp.concatenate([tm_scr[s] for s in halves], axis=1)
    lr = jnp.broadcast_to(lre_ref[...], (SUBLANES, SSM_NS))
    li = jnp.broadcast_to(lim_ref[...], (SUBLANES, SSM_NS))
    pt = nt // nparts
    prow = [slice(p * pt * nb, (p + 1) * pt * nb) for p in range(nparts)]
    groups = range(nb // SUBLANES)
    h = [h_scr[g * SUBLANES:(g + 1) * SUBLANES, :] for g in groups]

    bmat = bmat_ref[...] if exact_in else bmat_ref[...].astype(BF16)

    def in_map(p):
        up = u[prow[p]]
        buf[prow[p], :] = _hdot(up, bmat) if exact_in else _bdot(up, bmat)

    def recur(p):
        for t in range(p * pt, (p + 1) * pt):
            for g in groups:
                rs = slice(t * nb + g * SUBLANES, t * nb + (g + 1) * SUBLANES)
                bu = buf[rs, :]
                hre, him = h[g][:, :SSM_NS], h[g][:, SSM_NS:]
                h[g] = jnp.concatenate([lr * hre - li * him + bu[:, :SSM_NS],
                                        lr * him + li * hre + bu[:, SSM_NS:]], axis=1)
                buf[rs, :] = h[g]

    def out_map(p):
        rs = prow[p]
        y = _bdot(buf[rs, :SSM_NS], cre_ref[...]) - _bdot(buf[rs, SSM_NS:], cim_ref[...]) + d_ref[...] * u[rs]
        y = _gelu_tanh(y)
        y = y * jax.nn.sigmoid(_bdot(y, gw_ref[...]) + gb_ref[...])
        for s in halves:
            tm_scr[s, rs, :] = y[:, lanes[s]]

    in_map(0)
    for p in range(nparts):
        if p + 1 < nparts:
            in_map(p + 1)
        recur(p)
        if p >= 1:
            out_map(p - 1)
    out_map(nparts - 1)
    for g in groups:
        h_scr[g * SUBLANES:(g + 1) * SUBLANES, :] = h[g]
    hl_ref[...] = h_scr[...]
    for s in halves:
        if nb <= nt:
            for b in range(nb):
                y_ref[b, :, lanes[s]] = tm_scr[s, pl.ds(b, nt, stride=nb), :].astype(y_ref.dtype)
        else:
            for t in range(nt):
                bm_scr[s, pl.ds(t, nb, stride=nt), :] = tm_scr[s, t * nb:(t + 1) * nb, :]
            y_ref[:, lanes[s]] = bm_scr[s].astype(y_ref.dtype)


def _ssm(proj, col, h0, bmat, lre, lim, cre, cim, dskip, gw, gb, l, nt):
    nb = h0.shape[0]
    rows = nt * nb
    lo = col // LANES
    if proj.ndim == 3:
        t = proj.shape[1]
        grid = (t // nt,)
        u_spec = lambda c: pl.BlockSpec((nb, nt, LANES), lambda i: (0, i, c))
        y_spec = pl.BlockSpec((nb, nt, SSM_WIDTH), lambda i: (0, i, 0))
        y_shape = (nb, t, SSM_WIDTH)
    else:
        grid = (1,)
        u_spec = lambda c: pl.BlockSpec((rows, LANES), lambda i: (0, c))
        y_spec = pl.BlockSpec((rows, SSM_WIDTH), lambda i: (0, 0))
        y_shape = (rows, SSM_WIDTH)
    kern = functools.partial(_ssm_kernel, nb=nb, nt=nt, nparts=SSM_PARTS if proj.ndim == 3 else 1,
                             exact_in=proj.ndim == 2)
    const = lambda a: pl.BlockSpec(a.shape, lambda i: (0,) * a.ndim)
    return pl.pallas_call(
        kern,
        grid=grid,
        in_specs=[u_spec(lo), u_spec(lo + 1), const(h0), _layer(bmat, l), _layer(lre, l), _layer(lim, l),
                  _layer(cre, l), _layer(cim, l), _layer(dskip, l), _layer(gw, l), _layer(gb, l)],
        out_specs=[y_spec, const(h0)],
        out_shape=[jax.ShapeDtypeStruct(y_shape, BF16), jax.ShapeDtypeStruct(h0.shape, F32)],
        scratch_shapes=[pltpu.VMEM((rows, 2 * SSM_NS), F32), pltpu.VMEM(h0.shape, F32),
                        pltpu.VMEM((SSM_WIDTH // LANES, rows, LANES), F32),
                        pltpu.VMEM((SSM_WIDTH // LANES, rows, LANES), F32)],
        compiler_params=_cparams(1),
        name="ssm",
    )(proj, proj, h0, bmat, lre, lim, cre, cim, dskip, gw, gb)


def _pool_windows(xfull):
    s2 = xfull + pltpu.roll(xfull, 1, 0)
    s4 = s2 + pltpu.roll(s2, 2, 0)
    s8 = s4 + pltpu.roll(s4, 4, 0)
    s16 = s8 + pltpu.roll(s8, 8, 0)
    return s2, s4, s8, s16


def _pool_mix(sums, x, pos, w_ref, scale_ref):
    lane = lax.broadcasted_iota(jnp.int32, (1, POOL_WIDTH), 1) // POOL_GROUP
    win = None
    for gidx in reversed(range(len(POOL_WINDOWS))):
        cnt = jnp.minimum(pos + 1, POOL_WINDOWS[gidx]).astype(F32)
        term = sums[gidx] / cnt
        win = term if win is None else jnp.where(lane == gidx, term, win)
    r = win - x
    return _bdot(r, w_ref[...]) * scale_ref[...]


def _pool_prompt_kernel(u_ref, st_ref, w_ref, scale_ref, y_ref, tail, *, tb, pos0):
    i = pl.program_id(1)
    halo = 2 * SUBLANES

    @pl.when(i == 0)
    def _():
        tail[...] = st_ref[0]

    x = u_ref[0]
    xfull = jnp.concatenate([tail[...], x], axis=0)
    tail[...] = x[tb - halo:, :]
    sums = [s[halo:] for s in _pool_windows(xfull)]
    pos = pos0 + i * tb + lax.broadcasted_iota(jnp.int32, (tb, 1), 0)
    y_ref[0] = _pool_mix(sums, x, pos, w_ref, scale_ref).astype(y_ref.dtype)


def _pool_prompt(p3, col, st, wbd, scale, l, tb, pos0):
    nb, t, _ = p3.shape
    kern = functools.partial(_pool_prompt_kernel, tb=tb, pos0=pos0)
    return pl.pallas_call(
        kern,
        grid=(nb, t // tb),
        in_specs=[pl.BlockSpec((1, tb, POOL_WIDTH), lambda b, i: (b, i, col // POOL_WIDTH)),
                  pl.BlockSpec((1, 2 * SUBLANES, POOL_WIDTH), lambda b, i: (b, 0, 0)),
                  _layer(wbd, l), _layer(scale, l)],
        out_specs=pl.BlockSpec((1, tb, POOL_WIDTH), lambda b, i: (b, i, 0)),
        out_shape=jax.ShapeDtypeStruct((nb, t, POOL_WIDTH), BF16),
        scratch_shapes=[pltpu.VMEM((2 * SUBLANES, POOL_WIDTH), F32)],
        compiler_params=_cparams(2),
        name="pool_prompt",
    )(p3, st, wbd, scale)


def _pool_sample_kernel(x_ref, w_ref, scale_ref, y_ref, *, group, first, pos0):
    x = x_ref[...]
    rows = x.shape[0]
    sums = _pool_windows(x)
    pos = pos0 + (lax.broadcasted_iota(jnp.int32, (rows, 1), 0) % group) - first
    y_ref[...] = _pool_mix(sums, x, jnp.maximum(pos, 0), w_ref, scale_ref).astype(y_ref.dtype)


def _pool_sample(ext, wbd, scale, l, group, first, pos0):
    kern = functools.partial(_pool_sample_kernel, group=group, first=first, pos0=pos0)
    const = lambda a: pl.BlockSpec(a.shape, lambda i: (0,) * a.ndim)
    return pl.pallas_call(
        kern,
        grid=(1,),
        in_specs=[const(ext), _layer(wbd, l), _layer(scale, l)],
        out_specs=const(ext),
        out_shape=jax.ShapeDtypeStruct(ext.shape, BF16),
        compiler_params=_cparams(1),
        name="pool_sample",
    )(ext, wbd, scale)


def _block_diag(blocks):
    g, r, c = blocks.shape
    eye = jnp.eye(g, dtype=blocks.dtype)
    return (eye[:, None, :, None] * blocks[:, :, None, :]).reshape(g * r, g * c)


def kernel(x_prompt, x_sample, state_delta, state_conv, state_ssm_re, state_ssm_im, state_pool, norm_mix_pre, norm_mix_post, norm_ffn_pre, norm_ffn_post, w_in, conv_w, dn_a_log, dn_dt_bias, dn_out_norm, ssm_a_re, ssm_a_im, ssm_log_dt, ssm_b_re, ssm_b_im, ssm_c_re, ssm_c_im, ssm_d, ssm_glu_w, ssm_glu_b, pool_w, pool_scale, w_out, ffn_w_gate, ffn_w_up, ffn_w_down):
    depth = w_in.shape[0]
    bp, tp, _ = x_prompt.shape
    bs, ts, _ = x_sample.shape

    w_qkv = w_in[:, :, :QKV_WIDTH].astype(BF16)
    w_rest = w_in[:, :, _OFF_G:].astype(BF16)
    w_ab = jnp.pad(w_in[:, :, _OFF_A:_OFF_G], ((0, 0), (0, 0), (0, LANES - 2 * DN_HEADS))).astype(BF16)
    w_out_b = w_out.astype(BF16)
    wg_b, wu_b, wd_b = ffn_w_gate.astype(BF16), ffn_w_up.astype(BF16), ffn_w_down.astype(BF16)
    row = lambda a: a.reshape(depth, 1, -1)
    nmp, nmo, nfp, nfo = row(norm_mix_pre), row(norm_mix_post), row(norm_ffn_pre), row(norm_ffn_post)
    alog = jnp.pad(dn_a_log, ((0, 0), (0, LANES - DN_HEADS))).reshape(depth, 1, LANES)
    dtb = jnp.pad(dn_dt_bias, ((0, 0), (0, LANES - DN_HEADS))).reshape(depth, 1, LANES)
    onorm = row(dn_out_norm)
    dt_full = jnp.repeat(ssm_log_dt, SSM_STATE, axis=1).reshape(depth, 1, SSM_NS)
    b_t = lambda b: jnp.transpose(b, (0, 3, 1, 2)).reshape(depth, SSM_GROUP, SSM_NS)
    lam_re, lam_im, bmat = _ssm_prep(ssm_a_re.reshape(depth, 1, SSM_NS), ssm_a_im.reshape(depth, 1, SSM_NS),
                                     dt_full, b_t(ssm_b_re), b_t(ssm_b_im))
    c_bd = lambda cc: jax.vmap(_block_diag)(jnp.transpose(cc, (0, 1, 3, 2))).astype(BF16)
    cre, cim = c_bd(ssm_c_re), c_bd(ssm_c_im)
    dskip, glu_b = row(ssm_d), row(ssm_glu_b)
    glu_w = ssm_glu_w.astype(BF16)
    pool_bd = jax.vmap(_block_diag)(pool_w).astype(BF16)
    pscale = row(pool_scale)

    xp = x_prompt.reshape(bp * tp, D_MODEL)
    xs = x_sample.reshape(bs * ts, D_MODEL)
    zero_conv = jnp.zeros((bp, SUBLANES, QKV_WIDTH), F32)
    zero_delta = jnp.zeros((bp, DN_HEADS, DN_HEAD_DIM, DN_HEAD_DIM), F32)
    zero_h = jnp.zeros((bp, 2 * SSM_NS), F32)
    zero_pool = jnp.zeros((bp, 2 * SUBLANES, POOL_WIDTH), F32)
    pad_rows = SAMPLE_CHUNK - ts - (DN_CONV - 1)
    pool_group = POOL_SAMPLE_GROUP
    pool_first = 1 + POOL_BUF
    h0_s = jnp.concatenate([state_ssm_re.reshape(depth, bs, SSM_NS), state_ssm_im.reshape(depth, bs, SSM_NS)],
                           axis=2)

    outs_p, outs_s = [], []
    delta_s = jnp.zeros(state_delta.shape, F32)
    for l in range(depth):
        y, rest, craw = _in_proj_conv(xp, nmp, w_qkv, w_rest, w_ab, zero_conv, conv_w, l, IN_PROJ_ROWS, tp)
        y3 = y.reshape(bp, tp, QKV_WIDTH)
        r3 = rest.reshape(bp, tp, REST_WIDTH)
        o_dn, delta_new = _delta_prompt(y3, r3, zero_delta, alog, dtb, onorm, l, DELTA_SEQS, DELTA_GROUP,
                                        DELTA_TOKENS)
        o_ssm, h_fin = _ssm(r3, REST_SSM, zero_h, bmat, lam_re, lam_im, cre, cim, dskip, glu_w, glu_b, l,
                            SSM_STEPS)
        o_pool = _pool_prompt(r3, REST_POOL, zero_pool, pool_bd, pscale, l, POOL_TOKENS, 0)
        xp = _post_mix(o_dn.reshape(bp * tp, DN_WIDTH), o_ssm.reshape(bp * tp, SSM_WIDTH),
                       o_pool.reshape(bp * tp, POOL_WIDTH), xp,
                       w_out_b, nmo, nfp, wg_b, wu_b, wd_b, nfo, l, POST_MIX_ROWS)
        outs_p.append((delta_new, craw[:, SUBLANES - (DN_CONV - 1):, :],
                       h_fin[:, :SSM_NS].reshape(bp, SSM_GROUPS, SSM_STATE),
                       h_fin[:, SSM_NS:].reshape(bp, SSM_GROUPS, SSM_STATE),
                       r3[:, tp - POOL_BUF:, REST_POOL:REST_AB]))

        proj = _in_proj(xs, nmp, w_qkv, w_rest, w_ab, l, SAMPLE_ROWS)
        s3 = proj.reshape(bs, ts, PROJ_WIDTH)
        head = jnp.concatenate([jnp.zeros((bs, pad_rows, QKV_WIDTH), F32), state_conv[l]], axis=1)
        head = jnp.pad(head, ((0, 0), (0, 0), (0, PROJ_WIDTH - QKV_WIDTH)))
        ext = jnp.concatenate([head, s3], axis=1).reshape(bs * SAMPLE_CHUNK, PROJ_WIDTH)
        o_ext, delta_s = _delta_sample(ext, state_delta, delta_s, conv_w, alog, dtb, onorm, l,
                                       DELTA_SAMPLE_SEQS,
                                       SAMPLE_CHUNK - ts)
        o_dn = o_ext.reshape(bs, SAMPLE_CHUNK, DN_WIDTH)[:, SAMPLE_CHUNK - ts:].reshape(bs * ts, DN_WIDTH)
        o_ssm, h_fin = _ssm(proj, COL_SSM, h0_s[l], bmat, lam_re, lam_im, cre, cim, dskip, glu_w, glu_b, l, ts)
        pool_u = s3[:, :, COL_POOL:COL_AB]
        pext = jnp.concatenate([jnp.zeros((bs, 1, POOL_WIDTH), F32), state_pool[l], pool_u,
                                jnp.zeros((bs, pool_group - pool_first - ts, POOL_WIDTH), F32)], axis=1)
        y_ext = _pool_sample(pext.reshape(bs * pool_group, POOL_WIDTH), pool_bd, pscale, l, pool_group,
                             pool_first, PAST_LEN)
        o_pool = y_ext.reshape(bs, pool_group, POOL_WIDTH)[:, pool_first:pool_first + ts].reshape(bs * ts, POOL_WIDTH)
        xs = _post_mix(o_dn, o_ssm, o_pool, xs, w_out_b, nmo, nfp, wg_b, wu_b, wd_b, nfo, l, SAMPLE_ROWS)
        outs_s.append((s3[:, ts - (DN_CONV - 1):, :QKV_WIDTH],
                       h_fin[:, :SSM_NS].reshape(bs, SSM_GROUPS, SSM_STATE),
                       h_fin[:, SSM_NS:].reshape(bs, SSM_GROUPS, SSM_STATE),
                       jnp.concatenate([state_pool[l][:, ts:], pool_u], axis=1)))

    stack = lambda outs, k: jnp.stack([o[k] for o in outs])
    return (xp.reshape(bp, tp, D_MODEL), xs.reshape(bs, ts, D_MODEL),
            stack(outs_p, 0), stack(outs_p, 1), stack(outs_p, 2), stack(outs_p, 3), stack(outs_p, 4),
            delta_s, stack(outs_s, 0), stack(outs_s, 1), stack(outs_s, 2), stack(outs_s, 3))
```
